```python
import jax, jax.numpy as jnp
from jax import lax
import numpy as np

D_MODEL = 2048
BATCH = 8
SEQ = 8192
DEPTH = 4

CHUNK = 64
N_A_LAYERS = DEPTH // 2
N_B_LAYERS = DEPTH - N_A_LAYERS
POOL_WINDOWS = (2, 4, 8, 16)
N_POOL_GROUPS = len(POOL_WINDOWS)
POOL_GROUP = D_MODEL // N_POOL_GROUPS
N_HEADS = 16
HEAD_DIM = D_MODEL // N_HEADS
LEFT_CHUNKS = 8
LEFT = LEFT_CHUNKS * CHUNK
BAND = (LEFT_CHUNKS + 1) * CHUNK
REL_MAX = 128
N_REL = (CHUNK - 1) + REL_MAX + 1
D_FF = 5504
EPS = 1e-6
NEG_INF = -1e30

kernel_name = "yoco_pool_chunked_relbias_macaron"


def rmsnorm(x, g):
    xf = x.astype(jnp.float32)
    y = xf * lax.rsqrt(jnp.mean(xf * xf, axis=-1, keepdims=True) + EPS)
    return (y * g.astype(jnp.float32)).astype(x.dtype)


def swiglu(h, w_gate, w_up, w_down):
    return (jax.nn.silu(h @ w_gate) * (h @ w_up)) @ w_down


def pool_mixer(h, w_pool, scale):
    B, S, D = h.shape
    hf = h.astype(jnp.float32).reshape(B, S, N_POOL_GROUPS, POOL_GROUP)
    cs = jnp.cumsum(hf, axis=1)
    t = jnp.arange(S)
    pooled = []
    for g, w in enumerate(POOL_WINDOWS):
        csg = cs[:, :, g]
        prev = jnp.pad(csg, ((0, 0), (w, 0), (0, 0)))[:, :S]
        cnt = jnp.minimum(t + 1, w).astype(jnp.float32)[None, :, None]
        pooled.append((csg - prev) / cnt)
    pooled = jnp.stack(pooled, axis=2)
    diff = (pooled - hf).astype(h.dtype)
    y = jnp.einsum('bsgc,gcd->bsgd', diff, w_pool).reshape(B, S, D)
    return y * scale


def head_rmsnorm(t, g):
    tf = t.astype(jnp.float32)
    y = tf * lax.rsqrt(jnp.mean(tf * tf, axis=-1, keepdims=True) + EPS)
    return (y * g.astype(jnp.float32)).astype(t.dtype)


def shared_kv(x, kv_norm, w_k, w_v, k_gain):
    B, S, _ = x.shape
    hk = rmsnorm(x, kv_norm)
    k = head_rmsnorm((hk @ w_k).reshape(B, S, N_HEADS, HEAD_DIM), k_gain)
    v = (hk @ w_v).reshape(B, S, N_HEADS, HEAD_DIM)
    pad = ((0, 0), (LEFT, 0), (0, 0), (0, 0))
    return jnp.pad(k, pad), jnp.pad(v, pad)


def chunked_attention(h, w_q, q_gain, rel_table, w_o, k_pad, v_pad):
    B, S, D = h.shape
    nc = S // CHUNK
    q = head_rmsnorm((h @ w_q).reshape(B, S, N_HEADS, HEAD_DIM), q_gain)
    qc = q.reshape(B, nc, CHUNK, N_HEADS, HEAD_DIM).transpose(1, 0, 2, 3, 4)
    r = jnp.arange(CHUNK)[:, None]
    m = jnp.arange(BAND)[None, :]
    rel = r - m + LEFT
    idx = jnp.clip(rel, -(CHUNK - 1), REL_MAX) + (CHUNK - 1)
    bias = rel_table.astype(jnp.float32)[:, idx]
    scale = HEAD_DIM ** -0.5
    key_off = jnp.arange(BAND)

    def one_chunk(args):
        q_blk, c = args
        start = c * CHUNK
        kb = lax.dynamic_slice_in_dim(k_pad, start, BAND, axis=1)
        vb = lax.dynamic_slice_in_dim(v_pad, start, BAND, axis=1)
        s = jnp.einsum('bqhd,bkhd->bhqk', q_blk.astype(jnp.float32), kb.astype(jnp.float32)) * scale
        s = s + bias[None]
        valid = (start + key_off) >= LEFT
        s = jnp.where(valid[None, None, None, :], s, NEG_INF)
        p = jax.nn.softmax(s, axis=-1)
        o = jnp.einsum('bhqk,bkhd->bqhd', p, vb.astype(jnp.float32))
        return o.astype(h.dtype)

    out = lax.map(one_chunk, (qc, jnp.arange(nc)))
    out = out.transpose(1, 0, 2, 3, 4).reshape(B, S, D)
    return out @ w_o


def _fwd_setup_inputs(seed: int = 0) -> dict:
    key = jax.random.key(seed)
    ks = jax.random.split(key, 24)
    f32 = jnp.float32

    def nrm(k, shape, s):
        return jax.random.normal(k, shape, f32) * s

    def gain(k, shape):
        return jnp.ones(shape, f32) + 0.05 * jax.random.normal(k, shape, f32)

    return {
        "x": jax.random.normal(ks[0], (BATCH, SEQ, D_MODEL), f32),
        "ffn1_norm": gain(ks[1], (DEPTH, D_MODEL)),
        "ffn1_w_gate": nrm(ks[2], (DEPTH, D_MODEL, D_FF), D_MODEL ** -0.5),
        "ffn1_w_up": nrm(ks[3], (DEPTH, D_MODEL, D_FF), D_MODEL ** -0.5),
        "ffn1_w_down": nrm(ks[4], (DEPTH, D_FF, D_MODEL), D_FF ** -0.5),
        "mix_norm": gain(ks[5], (DEPTH, D_MODEL)),
        "ffn2_norm": gain(ks[6], (DEPTH, D_MODEL)),
        "ffn2_w_gate": nrm(ks[7], (DEPTH, D_MODEL, D_FF), D_MODEL ** -0.5),
        "ffn2_w_up": nrm(ks[8], (DEPTH, D_MODEL, D_FF), D_MODEL ** -0.5),
        "ffn2_w_down": nrm(ks[9], (DEPTH, D_FF, D_MODEL), D_FF ** -0.5),
        "pool_w": nrm(ks[10], (N_A_LAYERS, N_POOL_GROUPS, POOL_GROUP, POOL_GROUP), POOL_GROUP ** -0.5),
        "pool_scale": gain(ks[11], (N_A_LAYERS, D_MODEL)),
        "kv_norm": gain(ks[12], (D_MODEL,)),
        "w_k": nrm(ks[13], (D_MODEL, D_MODEL), D_MODEL ** -0.5),
        "w_v": nrm(ks[14], (D_MODEL, D_MODEL), D_MODEL ** -0.5),
        "k_gain": gain(ks[15], (HEAD_DIM,)),
        "w_q": nrm(ks[16], (N_B_LAYERS, D_MODEL, D_MODEL), D_MODEL ** -0.5),
        "q_gain": gain(ks[17], (N_B_LAYERS, HEAD_DIM)),
        "rel_bias": nrm(ks[18], (N_B_LAYERS, N_HEADS, N_REL), 0.5),
        "w_o": nrm(ks[19], (N_B_LAYERS, D_MODEL, D_MODEL), D_MODEL ** -0.5),
    }


def _fwd_reference(x, ffn1_norm, ffn1_w_gate, ffn1_w_up, ffn1_w_down, mix_norm, ffn2_norm,
              ffn2_w_gate, ffn2_w_up, ffn2_w_down, pool_w, pool_scale, kv_norm, w_k, w_v,
              k_gain, w_q, q_gain, rel_bias, w_o):
    k_pad, v_pad = None, None
    for l in range(DEPTH):
        x = x + 0.5 * swiglu(rmsnorm(x, ffn1_norm[l]), ffn1_w_gate[l], ffn1_w_up[l], ffn1_w_down[l])
        h = rmsnorm(x, mix_norm[l])
        if l < N_A_LAYERS:
            x = x + pool_mixer(h, pool_w[l], pool_scale[l])
        else:
            b = l - N_A_LAYERS
            x = x + chunked_attention(h, w_q[b], q_gain[b], rel_bias[b], w_o[b], k_pad, v_pad)
        x = x + 0.5 * swiglu(rmsnorm(x, ffn2_norm[l]), ffn2_w_gate[l], ffn2_w_up[l], ffn2_w_down[l])
        if l == N_A_LAYERS - 1:
            k_pad, v_pad = shared_kv(x, kv_norm, w_k, w_v, k_gain)
    return x


import jax as _jax
import jax.numpy as _jnp

TWIN_FORMAT = 'train_step'
FWD_PARAMS = ['x', 'ffn1_norm', 'ffn1_w_gate', 'ffn1_w_up', 'ffn1_w_down', 'mix_norm', 'ffn2_norm', 'ffn2_w_gate', 'ffn2_w_up', 'ffn2_w_down', 'pool_w', 'pool_scale', 'kv_norm', 'w_k', 'w_v', 'k_gain', 'w_q', 'q_gain', 'rel_bias', 'w_o']
TWIN_WEIGHTS = ['ffn1_norm', 'ffn1_w_gate', 'ffn1_w_up', 'ffn1_w_down', 'mix_norm', 'ffn2_norm', 'ffn2_w_gate', 'ffn2_w_up', 'ffn2_w_down', 'pool_w', 'pool_scale', 'kv_norm', 'w_k', 'w_v', 'k_gain', 'w_q', 'q_gain', 'rel_bias', 'w_o']
TWIN_DIFF_INPUT = 'x'
TWIN_INPUTS = ['x', 'ffn1_norm', 'ffn1_w_gate', 'ffn1_w_up', 'ffn1_w_down', 'mix_norm', 'ffn2_norm', 'ffn2_w_gate', 'ffn2_w_up', 'ffn2_w_down', 'pool_w', 'pool_scale', 'kv_norm', 'w_k', 'w_v', 'k_gain', 'w_q', 'q_gain', 'rel_bias', 'w_o', 'loss_target', 'm_ffn1_norm', 'm_ffn1_w_gate', 'm_ffn1_w_up', 'm_ffn1_w_down', 'm_mix_norm', 'm_ffn2_norm', 'm_ffn2_w_gate', 'm_ffn2_w_up', 'm_ffn2_w_down', 'm_pool_w', 'm_pool_scale', 'm_kv_norm', 'm_w_k', 'm_w_v', 'm_k_gain', 'm_w_q', 'm_q_gain', 'm_rel_bias', 'm_w_o', 'v_ffn1_norm', 'v_ffn1_w_gate', 'v_ffn1_w_up', 'v_ffn1_w_down', 'v_mix_norm', 'v_ffn2_norm', 'v_ffn2_w_gate', 'v_ffn2_w_up', 'v_ffn2_w_down', 'v_pool_w', 'v_pool_scale', 'v_kv_norm', 'v_w_k', 'v_w_v', 'v_k_gain', 'v_w_q', 'v_q_gain', 'v_rel_bias', 'v_w_o']
TWIN_OUTPUTS = ['loss', 'grad_x', 'grad_ffn1_norm', 'grad_ffn1_w_gate', 'grad_ffn1_w_up', 'grad_ffn1_w_down', 'grad_mix_norm', 'grad_ffn2_norm', 'grad_ffn2_w_gate', 'grad_ffn2_w_up', 'grad_ffn2_w_down', 'grad_pool_w', 'grad_pool_scale', 'grad_kv_norm', 'grad_w_k', 'grad_w_v', 'grad_k_gain', 'grad_w_q', 'grad_q_gain', 'grad_rel_bias', 'grad_w_o', 'delta_ffn1_norm', 'delta_ffn1_w_gate', 'delta_ffn1_w_up', 'delta_ffn1_w_down', 'delta_mix_norm', 'delta_ffn2_norm', 'delta_ffn2_w_gate', 'delta_ffn2_w_up', 'delta_ffn2_w_down', 'delta_pool_w', 'delta_pool_scale', 'delta_kv_norm', 'delta_w_k', 'delta_w_v', 'delta_k_gain', 'delta_w_q', 'delta_q_gain', 'delta_rel_bias', 'delta_w_o', 'new_m_ffn1_norm', 'new_m_ffn1_w_gate', 'new_m_ffn1_w_up', 'new_m_ffn1_w_down', 'new_m_mix_norm', 'new_m_ffn2_norm', 'new_m_ffn2_w_gate', 'new_m_ffn2_w_up', 'new_m_ffn2_w_down', 'new_m_pool_w', 'new_m_pool_scale', 'new_m_kv_norm', 'new_m_w_k', 'new_m_w_v', 'new_m_k_gain', 'new_m_w_q', 'new_m_q_gain', 'new_m_rel_bias', 'new_m_w_o', 'new_v_ffn1_norm', 'new_v_ffn1_w_gate', 'new_v_ffn1_w_up', 'new_v_ffn1_w_down', 'new_v_mix_norm', 'new_v_ffn2_norm', 'new_v_ffn2_w_gate', 'new_v_ffn2_w_up', 'new_v_ffn2_w_down', 'new_v_pool_w', 'new_v_pool_scale', 'new_v_kv_norm', 'new_v_w_k', 'new_v_w_v', 'new_v_k_gain', 'new_v_w_q', 'new_v_q_gain', 'new_v_rel_bias', 'new_v_w_o']
TWIN_LEAF_KINDS = {'loss': 'loss', 'grad_x': 'grad_x', 'grad_ffn1_norm': 'grad_w', 'grad_ffn1_w_gate': 'grad_w', 'grad_ffn1_w_up': 'grad_w', 'grad_ffn1_w_down': 'grad_w', 'grad_mix_norm': 'grad_w', 'grad_ffn2_norm': 'grad_w', 'grad_ffn2_w_gate': 'grad_w', 'grad_ffn2_w_up': 'grad_w', 'grad_ffn2_w_down': 'grad_w', 'grad_pool_w': 'grad_w', 'grad_pool_scale': 'grad_w', 'grad_kv_norm': 'grad_w', 'grad_w_k': 'grad_w', 'grad_w_v': 'grad_w', 'grad_k_gain': 'grad_w', 'grad_w_q': 'grad_w', 'grad_q_gain': 'grad_w', 'grad_rel_bias': 'grad_w', 'grad_w_o': 'grad_w', 'delta_ffn1_norm': 'delta_w', 'delta_ffn1_w_gate': 'delta_w', 'delta_ffn1_w_up': 'delta_w', 'delta_ffn1_w_down': 'delta_w', 'delta_mix_norm': 'delta_w', 'delta_ffn2_norm': 'delta_w', 'delta_ffn2_w_gate': 'delta_w', 'delta_ffn2_w_up': 'delta_w', 'delta_ffn2_w_down': 'delta_w', 'delta_pool_w': 'delta_w', 'delta_pool_scale': 'delta_w', 'delta_kv_norm': 'delta_w', 'delta_w_k': 'delta_w', 'delta_w_v': 'delta_w', 'delta_k_gain': 'delta_w', 'delta_w_q': 'delta_w', 'delta_q_gain': 'delta_w', 'delta_rel_bias': 'delta_w', 'delta_w_o': 'delta_w', 'new_m_ffn1_norm': 'new_m', 'new_m_ffn1_w_gate': 'new_m', 'new_m_ffn1_w_up': 'new_m', 'new_m_ffn1_w_down': 'new_m', 'new_m_mix_norm': 'new_m', 'new_m_ffn2_norm': 'new_m', 'new_m_ffn2_w_gate': 'new_m', 'new_m_ffn2_w_up': 'new_m', 'new_m_ffn2_w_down': 'new_m', 'new_m_pool_w': 'new_m', 'new_m_pool_scale': 'new_m', 'new_m_kv_norm': 'new_m', 'new_m_w_k': 'new_m', 'new_m_w_v': 'new_m', 'new_m_k_gain': 'new_m', 'new_m_w_q': 'new_m', 'new_m_q_gain': 'new_m', 'new_m_rel_bias': 'new_m', 'new_m_w_o': 'new_m', 'new_v_ffn1_norm': 'new_v', 'new_v_ffn1_w_gate': 'new_v', 'new_v_ffn1_w_up': 'new_v', 'new_v_ffn1_w_down': 'new_v', 'new_v_mix_norm': 'new_v', 'new_v_ffn2_norm': 'new_v', 'new_v_ffn2_w_gate': 'new_v', 'new_v_ffn2_w_up': 'new_v', 'new_v_ffn2_w_down': 'new_v', 'new_v_pool_w': 'new_v', 'new_v_pool_scale': 'new_v', 'new_v_kv_norm': 'new_v', 'new_v_w_k': 'new_v', 'new_v_w_v': 'new_v', 'new_v_k_gain': 'new_v', 'new_v_w_q': 'new_v', 'new_v_q_gain': 'new_v', 'new_v_rel_bias': 'new_v', 'new_v_w_o': 'new_v'}


def _forward(args):
    return _fwd_reference(*[args[k] for k in FWD_PARAMS])


def _output_shape():
    def fwd():
        inp = _fwd_setup_inputs(0)
        return _fwd_reference(*[inp[k] for k in FWD_PARAMS])
    out = _jax.eval_shape(fwd)
    return out.shape, out.dtype

N_MICROBATCH = 1
ADAM_LR = 0.001
ADAM_B1 = 0.9
ADAM_B2 = 0.999
ADAM_EPS = 1e-08
ADAM_WD = 0.01
ADAM_STEP = 10
PER_EXAMPLE_BATCH_AXIS = {'x': 0, 'loss_target': 0}
SHARED_INPUTS = []
_WEIGHT_DTYPES = {'ffn1_norm': _jnp.float32, 'ffn1_w_gate': _jnp.float32, 'ffn1_w_up': _jnp.float32, 'ffn1_w_down': _jnp.float32, 'mix_norm': _jnp.float32, 'ffn2_norm': _jnp.float32, 'ffn2_w_gate': _jnp.float32, 'ffn2_w_up': _jnp.float32, 'ffn2_w_down': _jnp.float32, 'pool_w': _jnp.float32, 'pool_scale': _jnp.float32, 'kv_norm': _jnp.float32, 'w_k': _jnp.float32, 'w_v': _jnp.float32, 'k_gain': _jnp.float32, 'w_q': _jnp.float32, 'q_gain': _jnp.float32, 'rel_bias': _jnp.float32, 'w_o': _jnp.float32}
MOMENT_SCALE = {'ffn1_norm': 6.085708e+00, 'ffn1_w_gate': 7.391806e-02, 'ffn1_w_up': 9.294437e-02, 'ffn1_w_down': 1.526254e-01, 'mix_norm': 1.884483e+01, 'ffn2_norm': 6.162952e+00, 'ffn2_w_gate': 6.187062e-02, 'ffn2_w_up': 8.586783e-02, 'ffn2_w_down': 1.384669e-01, 'pool_w': 2.124987e+00, 'pool_scale': 2.663692e+01, 'kv_norm': 2.689784e-01, 'w_k': 6.593120e-02, 'w_v': 6.389366e-02, 'k_gain': 3.713071e+00, 'w_q': 4.663675e-02, 'q_gain': 1.868751e+00, 'rel_bias': 9.299344e-02, 'w_o': 4.338910e-02}


def _to_microbatches(a, axis):
    t = _jnp.moveaxis(a, axis, 0)
    t = t.reshape((N_MICROBATCH, t.shape[0] // N_MICROBATCH) + t.shape[1:])
    return _jnp.moveaxis(t, 1, axis + 1)


def setup_inputs(seed: int = 0) -> dict:
    inp = _fwd_setup_inputs(seed)
    key = _jax.random.fold_in(_jax.random.key(seed), 7919)
    shape, _ = _output_shape()
    out = dict(inp)
    out["loss_target"] = _jax.random.normal(_jax.random.fold_in(key, 0), shape, _jnp.float32)
    for i, name in enumerate(TWIN_WEIGHTS):
        w = inp[name].astype(_jnp.float32)
        if MOMENT_SCALE is None:
            s = _jnp.sqrt(_jnp.mean(_jnp.square(w)) + 1e-30)
        else:
            s = MOMENT_SCALE[name]
        km, kv = _jax.random.split(_jax.random.fold_in(key, i + 1))
        out[name] = w
        out["m_" + name] = s * _jax.random.normal(km, w.shape, _jnp.float32)
        out["v_" + name] = (s * s) * _jax.random.uniform(kv, w.shape, _jnp.float32, 0.5, 1.5)
    if N_MICROBATCH > 1:
        for name, axis in PER_EXAMPLE_BATCH_AXIS.items():
            out[name] = _to_microbatches(out[name], axis)
    return {'x': out['x'], 'ffn1_norm': out['ffn1_norm'], 'ffn1_w_gate': out['ffn1_w_gate'], 'ffn1_w_up': out['ffn1_w_up'], 'ffn1_w_down': out['ffn1_w_down'], 'mix_norm': out['mix_norm'], 'ffn2_norm': out['ffn2_norm'], 'ffn2_w_gate': out['ffn2_w_gate'], 'ffn2_w_up': out['ffn2_w_up'], 'ffn2_w_down': out['ffn2_w_down'], 'pool_w': out['pool_w'], 'pool_scale': out['pool_scale'], 'kv_norm': out['kv_norm'], 'w_k': out['w_k'], 'w_v': out['w_v'], 'k_gain': out['k_gain'], 'w_q': out['w_q'], 'q_gain': out['q_gain'], 'rel_bias': out['rel_bias'], 'w_o': out['w_o'], 'loss_target': out['loss_target'], 'm_ffn1_norm': out['m_ffn1_norm'], 'm_ffn1_w_gate': out['m_ffn1_w_gate'], 'm_ffn1_w_up': out['m_ffn1_w_up'], 'm_ffn1_w_down': out['m_ffn1_w_down'], 'm_mix_norm': out['m_mix_norm'], 'm_ffn2_norm': out['m_ffn2_norm'], 'm_ffn2_w_gate': out['m_ffn2_w_gate'], 'm_ffn2_w_up': out['m_ffn2_w_up'], 'm_ffn2_w_down': out['m_ffn2_w_down'], 'm_pool_w': out['m_pool_w'], 'm_pool_scale': out['m_pool_scale'], 'm_kv_norm': out['m_kv_norm'], 'm_w_k': out['m_w_k'], 'm_w_v': out['m_w_v'], 'm_k_gain': out['m_k_gain'], 'm_w_q': out['m_w_q'], 'm_q_gain': out['m_q_gain'], 'm_rel_bias': out['m_rel_bias'], 'm_w_o': out['m_w_o'], 'v_ffn1_norm': out['v_ffn1_norm'], 'v_ffn1_w_gate': out['v_ffn1_w_gate'], 'v_ffn1_w_up': out['v_ffn1_w_up'], 'v_ffn1_w_down': out['v_ffn1_w_down'], 'v_mix_norm': out['v_mix_norm'], 'v_ffn2_norm': out['v_ffn2_norm'], 'v_ffn2_w_gate': out['v_ffn2_w_gate'], 'v_ffn2_w_up': out['v_ffn2_w_up'], 'v_ffn2_w_down': out['v_ffn2_w_down'], 'v_pool_w': out['v_pool_w'], 'v_pool_scale': out['v_pool_scale'], 'v_kv_norm': out['v_kv_norm'], 'v_w_k': out['v_w_k'], 'v_w_v': out['v_w_v'], 'v_k_gain': out['v_k_gain'], 'v_w_q': out['v_w_q'], 'v_q_gain': out['v_q_gain'], 'v_rel_bias': out['v_rel_bias'], 'v_w_o': out['v_w_o']}


def _loss(weights, diff, rest, loss_target):
    with _jax.named_scope("forward"):
        args = {**rest, TWIN_DIFF_INPUT: diff, **{k: w.astype(_WEIGHT_DTYPES[k]) for k, w in weights.items()}}
        y = _forward(args)
    with _jax.named_scope("loss_head"):
        err = _jnp.square(y.astype(_jnp.float32) - loss_target)
        return 0.5 * _jnp.sum(_jnp.mean(err, axis=-1)) if err.ndim else 0.5 * err


def _adamw(w, g, m, v):
    m = ADAM_B1 * m + (1.0 - ADAM_B1) * g
    v = ADAM_B2 * v + (1.0 - ADAM_B2) * _jnp.square(g)
    m_hat = m / (1.0 - ADAM_B1 ** ADAM_STEP)
    v_hat = v / (1.0 - ADAM_B2 ** ADAM_STEP)
    delta = -ADAM_LR * (m_hat / (_jnp.sqrt(v_hat) + ADAM_EPS) + ADAM_WD * w)
    return delta, m, v


def reference(x, ffn1_norm, ffn1_w_gate, ffn1_w_up, ffn1_w_down, mix_norm, ffn2_norm, ffn2_w_gate, ffn2_w_up, ffn2_w_down, pool_w, pool_scale, kv_norm, w_k, w_v, k_gain, w_q, q_gain, rel_bias, w_o, loss_target, m_ffn1_norm, m_ffn1_w_gate, m_ffn1_w_up, m_ffn1_w_down, m_mix_norm, m_ffn2_norm, m_ffn2_w_gate, m_ffn2_w_up, m_ffn2_w_down, m_pool_w, m_pool_scale, m_kv_norm, m_w_k, m_w_v, m_k_gain, m_w_q, m_q_gain, m_rel_bias, m_w_o, v_ffn1_norm, v_ffn1_w_gate, v_ffn1_w_up, v_ffn1_w_down, v_mix_norm, v_ffn2_norm, v_ffn2_w_gate, v_ffn2_w_up, v_ffn2_w_down, v_pool_w, v_pool_scale, v_kv_norm, v_w_k, v_w_v, v_k_gain, v_w_q, v_q_gain, v_rel_bias, v_w_o):
    given = dict(x=x, ffn1_norm=ffn1_norm, ffn1_w_gate=ffn1_w_gate, ffn1_w_up=ffn1_w_up, ffn1_w_down=ffn1_w_down, mix_norm=mix_norm, ffn2_norm=ffn2_norm, ffn2_w_gate=ffn2_w_gate, ffn2_w_up=ffn2_w_up, ffn2_w_down=ffn2_w_down, pool_w=pool_w, pool_scale=pool_scale, kv_norm=kv_norm, w_k=w_k, w_v=w_v, k_gain=k_gain, w_q=w_q, q_gain=q_gain, rel_bias=rel_bias, w_o=w_o, loss_target=loss_target, m_ffn1_norm=m_ffn1_norm, m_ffn1_w_gate=m_ffn1_w_gate, m_ffn1_w_up=m_ffn1_w_up, m_ffn1_w_down=m_ffn1_w_down, m_mix_norm=m_mix_norm, m_ffn2_norm=m_ffn2_norm, m_ffn2_w_gate=m_ffn2_w_gate, m_ffn2_w_up=m_ffn2_w_up, m_ffn2_w_down=m_ffn2_w_down, m_pool_w=m_pool_w, m_pool_scale=m_pool_scale, m_kv_norm=m_kv_norm, m_w_k=m_w_k, m_w_v=m_w_v, m_k_gain=m_k_gain, m_w_q=m_w_q, m_q_gain=m_q_gain, m_rel_bias=m_rel_bias, m_w_o=m_w_o, v_ffn1_norm=v_ffn1_norm, v_ffn1_w_gate=v_ffn1_w_gate, v_ffn1_w_up=v_ffn1_w_up, v_ffn1_w_down=v_ffn1_w_down, v_mix_norm=v_mix_norm, v_ffn2_norm=v_ffn2_norm, v_ffn2_w_gate=v_ffn2_w_gate, v_ffn2_w_up=v_ffn2_w_up, v_ffn2_w_down=v_ffn2_w_down, v_pool_w=v_pool_w, v_pool_scale=v_pool_scale, v_kv_norm=v_kv_norm, v_w_k=v_w_k, v_w_v=v_w_v, v_k_gain=v_k_gain, v_w_q=v_w_q, v_q_gain=v_q_gain, v_rel_bias=v_rel_bias, v_w_o=v_w_o)
    weights = {n: given[n] for n in TWIN_WEIGHTS}
    shared = {n: given[n] for n in SHARED_INPUTS}
    per_example = {n: given[n] for n in ['x']}
    grad_fn = _jax.value_and_grad(_loss, argnums=(0, 1))

    def one_microbatch(ex, loss_target):
        ex = dict(ex)
        diff = ex.pop(TWIN_DIFF_INPUT)
        return grad_fn(weights, diff, {**shared, **ex}, loss_target)

    if N_MICROBATCH == 1:
        loss, (grad_w, grad_x) = one_microbatch(per_example, given["loss_target"])
    else:
        def body(carry, xs):
            loss_sum, grad_sum = carry
            l_k, (gw_k, gx_k) = one_microbatch(xs[0], xs[1])
            with _jax.named_scope("update"):
                return (loss_sum + l_k, _jax.tree.map(_jnp.add, grad_sum, gw_k)), gx_k

        init = (_jnp.zeros((), _jnp.float32), _jax.tree.map(_jnp.zeros_like, weights))
        (loss, grad_w), grad_x = _jax.lax.scan(body, init, (per_example, given["loss_target"]))
    with _jax.named_scope("update"):
        delta_w, new_m, new_v = {}, {}, {}
        for n in TWIN_WEIGHTS:
            delta_w[n], new_m[n], new_v[n] = _adamw(weights[n], grad_w[n], given["m_" + n], given["v_" + n])
    return (loss, grad_x, *[grad_w[n] for n in TWIN_WEIGHTS], *[delta_w[n] for n in TWIN_WEIGHTS],
            *[new_m[n] for n in TWIN_WEIGHTS], *[new_v[n] for n in TWIN_WEIGHTS])
```

```python
import functools

import jax
import jax.numpy as jnp
from jax import lax
from jax.experimental import pallas as pl
from jax.experimental.pallas import tpu as pltpu

F32 = jnp.float32
BF16 = jnp.bfloat16

EPS = 1e-6
CHUNK = 64
LEFT = 512
LEFT_CHUNKS = LEFT // CHUNK
REL_MAX = 128
POOL_WINDOWS = (2, 4, 8, 16)
HALO = 16
NEG_INF = -1e30
TQ = 256
BAND_W = TQ + LEFT
N_CHIPS = 4
LANES = 128
VMEM_LIMIT = 56 * 1024 * 1024

ADAM_LR = 0.001
ADAM_B1 = 0.9
ADAM_B2 = 0.999
ADAM_EPS = 1e-08
ADAM_WD = 0.01
ADAM_STEP = 10

MESH = pl.DeviceIdType.MESH
ANY = pl.BlockSpec(memory_space=pl.ANY)


def _round_up(n, m):
    return (n + m - 1) // m * m


def _tile(n, pref, unit=LANES):
    if n <= pref:
        return n
    t = pref // unit * unit
    while t >= unit:
        if n % t == 0:
            return t
        t -= unit
    return n


def _params(sem):
    return pltpu.CompilerParams(dimension_semantics=sem, vmem_limit_bytes=VMEM_LIMIT)


def _bs(block, imap, lead=None):
    if lead is None:
        return pl.BlockSpec(tuple(block), imap)
    return pl.BlockSpec((None,) + tuple(block), lambda *g: (lead,) + tuple(imap(*g)))


_DIMS = {
    "nn": (((1,), (0,)), ((), ())),
    "nt": (((1,), (1,)), ((), ())),
    "tn": (((0,), (0,)), ((), ())),
}


def _mm(name, pairs, m, n, k, mode, outs, epilogue, extras=(), tm=512, tn=512, tk=512):
    tm, tn, tk = _tile(m, tm, LANES if mode == "tn" else 8), _tile(n, tn), _tile(k, tk)
    nk = k // tk
    n_pairs, n_extra, n_out = len(pairs), len(extras), len(outs)
    in_specs, operands = [], []
    for (a, la, oa), (b, lb, ob) in pairs:
        if mode == "tn":
            in_specs.append(_bs((tk, tm), lambda i, j, kk, o=oa: (kk + o, i), la))
        else:
            in_specs.append(_bs((tm, tk), lambda i, j, kk, o=oa: (i + o, kk), la))
        if mode == "nt":
            in_specs.append(_bs((tn, tk), lambda i, j, kk, o=ob: (j + o, kk), lb))
        else:
            in_specs.append(_bs((tk, tn), lambda i, j, kk, o=ob: (kk + o, j), lb))
        operands += [a, b]
    for e, le, kind in extras:
        if kind == "mn":
            in_specs.append(_bs((tm, tn), lambda i, j, kk: (i, j), le))
        else:
            in_specs.append(_bs((1, tn), lambda i, j, kk: (0, j), le))
        operands.append(e)
    out_specs = [pl.BlockSpec((tm, tn), lambda i, j, kk, o=off: (i + o, j)) for _, _, off in outs]
    out_shape = [jax.ShapeDtypeStruct((rows, n), dt) for rows, dt, _ in outs]
    dims = _DIMS[mode]

    def body(*refs):
        ab = refs[: 2 * n_pairs]
        ex = refs[2 * n_pairs: 2 * n_pairs + n_extra]
        out_refs = refs[2 * n_pairs + n_extra: 2 * n_pairs + n_extra + n_out]
        acc = refs[-1]
        kk = pl.program_id(2)

        @pl.when(kk == 0)
        def _():
            acc[...] = jnp.zeros_like(acc)

        for p in range(n_pairs):
            a = ab[2 * p][...].astype(BF16)
            b = ab[2 * p + 1][...].astype(BF16)
            acc[...] += lax.dot_general(a, b, dims, preferred_element_type=F32)

        @pl.when(kk == nk - 1)
        def _():
            res = epilogue(acc[...], *[e[...] for e in ex])
            for o_ref, o in zip(out_refs, res):
                o_ref[...] = o.astype(o_ref.dtype)

    return pl.pallas_call(
        body, name=name, grid=(m // tm, n // tn, nk), in_specs=in_specs, out_specs=out_specs, out_shape=out_shape,
        scratch_shapes=[pltpu.VMEM((tm, tn), F32)], compiler_params=_params(("parallel", "parallel", "arbitrary")),
    )(*operands)


def _op(a, lead=None, off=0):
    return (a, lead, off)


def _head_rms(blk, gain):
    r = lax.rsqrt(jnp.mean(blk * blk, axis=-1, keepdims=True) + EPS)
    return blk * r * gain, r


def _rms_fwd(name, x, gain_row):
    t, d = x.shape
    tm = _tile(t, 512, 8)

    def body(x_ref, g_ref, h_ref):
        xv = x_ref[...]
        r = lax.rsqrt(jnp.mean(xv * xv, axis=-1, keepdims=True) + EPS)
        h_ref[...] = (xv * r * g_ref[...]).astype(BF16)

    return pl.pallas_call(
        body, name=name, grid=(t // tm,),
        in_specs=[pl.BlockSpec((tm, d), lambda i: (i, 0)), pl.BlockSpec((1, d), lambda i: (0, 0))],
        out_specs=pl.BlockSpec((tm, d), lambda i: (i, 0)), out_shape=jax.ShapeDtypeStruct((t, d), BF16),
        compiler_params=_params(("parallel",)),
    )(x, gain_row)


def _rms_bwd_math(dh, xv, g):
    r = lax.rsqrt(jnp.mean(xv * xv, axis=-1, keepdims=True) + EPS)
    u = dh * g
    dx = r * u - xv * (r * r * r * jnp.mean(u * xv, axis=-1, keepdims=True))
    dg = jnp.sum(dh * xv * r, axis=0, keepdims=True)
    return dx, dg


def _rms_bwd(name, dh, x, gain_row, dres=None):
    t, d = x.shape
    tm = _tile(t, 256, 8)
    has_res = dres is not None

    def body(*refs):
        if has_res:
            dh_ref, x_ref, g_ref, dres_ref, dx_ref, dg_ref = refs
        else:
            dh_ref, x_ref, g_ref, dx_ref, dg_ref = refs
        dx, dg = _rms_bwd_math(dh_ref[...], x_ref[...], g_ref[...])
        if has_res:
            dx = dx + dres_ref[...]
        dx_ref[...] = dx

        @pl.when(pl.program_id(0) == 0)
        def _():
            dg_ref[...] = jnp.zeros_like(dg_ref)

        dg_ref[...] += dg

    row = pl.BlockSpec((tm, d), lambda i: (i, 0))
    vec = pl.BlockSpec((1, d), lambda i: (0, 0))
    return pl.pallas_call(
        body, name=name, grid=(t // tm,),
        in_specs=[row, row, vec] + ([row] if has_res else []),
        out_specs=[row, vec],
        out_shape=[jax.ShapeDtypeStruct((t, d), F32), jax.ShapeDtypeStruct((1, d), F32)],
        compiler_params=_params(("arbitrary",)),
    )(*([dh, x, gain_row] + ([dres] if has_res else [])))


def _ffn_up(name, h, wg, wu, layer):
    t, d = h.shape
    f = wg.shape[-1]
    tm, tn = _tile(t, 512, 8), _tile(f, 512)

    def body(h_ref, wg_ref, wu_ref, g_ref, u_ref, a_ref):
        hv = h_ref[...]
        g = jnp.dot(hv, wg_ref[...], preferred_element_type=F32)
        u = jnp.dot(hv, wu_ref[...], preferred_element_type=F32)
        g_ref[...] = g.astype(BF16)
        u_ref[...] = u.astype(BF16)
        a_ref[...] = (g * jax.nn.sigmoid(g) * u).astype(BF16)

    wspec = _bs((d, tn), lambda i, j: (0, j), layer)
    ospec = pl.BlockSpec((tm, tn), lambda i, j: (i, j))
    return pl.pallas_call(
        body, name=name, grid=(t // tm, f // tn),
        in_specs=[pl.BlockSpec((tm, d), lambda i, j: (i, 0)), wspec, wspec],
        out_specs=[ospec] * 3, out_shape=[jax.ShapeDtypeStruct((t, f), BF16)] * 3,
        compiler_params=_params(("parallel", "parallel")),
    )(h, wg, wu)


def _ffn_fwd(tag, x, gain_row, wg, wu, wd, layer):
    t, d = x.shape
    f = wg.shape[-1]
    h = _rms_fwd(f"{tag}_norm", x, gain_row)
    g, u, a = _ffn_up(f"{tag}_up", h, wg, wu, layer)
    (x_new,) = _mm(f"{tag}_down", [(_op(a), _op(wd, layer))], t, d, f, "nn", [(t, F32, 0)],
                   lambda acc, xv: (xv + 0.5 * acc,), extras=[(x, None, "mn")], tk=1408)
    return x_new, (x, h, g, u, a)


def _ffn_bwd(tag, dout, saved, gain_row, wg, wu, wd, layer):
    x, h, g, u, a = saved
    t, d = x.shape
    f = wg.shape[-1]

    def act_bwd(acc, gv, uv):
        da = 0.5 * acc
        gv = gv.astype(F32)
        uv = uv.astype(F32)
        sig = jax.nn.sigmoid(gv)
        du = da * (gv * sig)
        dg = da * uv * (sig * (1.0 + gv * (1.0 - sig)))
        return dg, du

    dg, du = _mm(f"{tag}_bwd_act", [(_op(dout), _op(wd, layer))], t, f, d, "nt", [(t, BF16, 0), (t, BF16, 0)],
                 act_bwd, extras=[(g, None, "mn"), (u, None, "mn")], tk=2048)
    (dwd,) = _mm(f"{tag}_bwd_wd", [(_op(a), _op(dout))], f, d, t, "tn", [(f, F32, 0)],
                 lambda acc: (0.5 * acc,), tn=1024, tk=1024)
    (dwg,) = _mm(f"{tag}_bwd_wg", [(_op(h), _op(dg))], d, f, t, "tn", [(d, F32, 0)], lambda acc: (acc,), tk=1024)
    (dwu,) = _mm(f"{tag}_bwd_wu", [(_op(h), _op(du))], d, f, t, "tn", [(d, F32, 0)], lambda acc: (acc,), tk=1024)
    (dh,) = _mm(f"{tag}_bwd_h", [(_op(dg), _op(wg, layer)), (_op(du), _op(wu, layer))], t, d, f, "nt",
                [(t, F32, 0)], lambda acc: (acc,), tk=1408)
    dx, dgain = _rms_bwd(f"{tag}_bwd_norm", dh, x, gain_row, dout)
    return dx, dgain, dwg, dwu, dwd


def _inv_count(t0, rows, window):
    tpos = t0 + lax.broadcasted_iota(jnp.int32, (rows, 1), 0)
    return 1.0 / jnp.minimum(tpos + 1, window).astype(F32)


def _pool_fwd(name, x, gain_row, wp, scale_row, layer):
    t, d = x.shape
    n_g = len(POOL_WINDOWS)
    p = d // n_g
    tm = _tile(t, 256, 8)
    hb = tm // HALO

    def body(x_ref, xp_ref, g_ref, wp_ref, s_ref, out_ref, diff_ref, ypre_ref):
        i = pl.program_id(0)
        gain = g_ref[...]

        def norm(v):
            return v * lax.rsqrt(jnp.mean(v * v, axis=-1, keepdims=True) + EPS) * gain

        xv = x_ref[...]
        h = norm(xv)
        hp = jnp.where(i > 0, norm(xp_ref[...]), 0.0)
        ext = jnp.concatenate([hp, h], axis=0)
        for gi, w in enumerate(POOL_WINDOWS):
            cols = slice(gi * p, (gi + 1) * p)
            s = ext[:, cols]
            k = 1
            while k < w:
                s = s + pltpu.roll(s, k, 0)
                k *= 2
            pooled = s[HALO:, :] * _inv_count(i * tm, tm, w)
            diff = (pooled - h[:, cols]).astype(BF16)
            y = jnp.dot(diff, wp_ref[gi], preferred_element_type=F32)
            diff_ref[:, cols] = diff
            ypre_ref[:, cols] = y.astype(BF16)
            out_ref[:, cols] = xv[:, cols] + y * s_ref[:, cols]

    row = pl.BlockSpec((tm, d), lambda i: (i, 0))
    vec = pl.BlockSpec((1, d), lambda i: (0, 0))
    return pl.pallas_call(
        body, name=name, grid=(t // tm,),
        in_specs=[row, pl.BlockSpec((HALO, d), lambda i: (jnp.maximum(i * hb - 1, 0), 0)), vec,
                  _bs((n_g, p, p), lambda i: (0, 0, 0), layer), vec],
        out_specs=[row, row, row],
        out_shape=[jax.ShapeDtypeStruct((t, d), F32), jax.ShapeDtypeStruct((t, d), BF16),
                   jax.ShapeDtypeStruct((t, d), BF16)],
        compiler_params=_params(("parallel",)),
    )(x, x, gain_row, wp, scale_row)


def _pool_bwd_mat(name, dout, diff, ypre, wp, scale_row, layer):
    t, d = dout.shape
    n_g = len(POOL_WINDOWS)
    p = d // n_g
    tm = _tile(t, 256, 8)

    def body(do_ref, diff_ref, ypre_ref, wp_ref, s_ref, dd_ref, dwp_ref, ds_ref):
        @pl.when(pl.program_id(0) == 0)
        def _():
            dwp_ref[...] = jnp.zeros_like(dwp_ref)
            ds_ref[...] = jnp.zeros_like(ds_ref)

        dov = do_ref[...]
        ds_ref[...] += jnp.sum(dov * ypre_ref[...].astype(F32), axis=0, keepdims=True)
        dy = (dov * s_ref[...]).astype(BF16)
        for gi in range(n_g):
            cols = slice(gi * p, (gi + 1) * p)
            dwp_ref[gi] += lax.dot_general(diff_ref[:, cols], dy[:, cols], _DIMS["tn"], preferred_element_type=F32)
            dd_ref[:, cols] = lax.dot_general(dy[:, cols], wp_ref[gi], _DIMS["nt"], preferred_element_type=F32)

    row = pl.BlockSpec((tm, d), lambda i: (i, 0))
    vec = pl.BlockSpec((1, d), lambda i: (0, 0))
    return pl.pallas_call(
        body, name=name, grid=(t // tm,),
        in_specs=[row, row, row, _bs((n_g, p, p), lambda i: (0, 0, 0), layer), vec],
        out_specs=[row, pl.BlockSpec((n_g, p, p), lambda i: (0, 0, 0)), vec],
        out_shape=[jax.ShapeDtypeStruct((t, d), F32), jax.ShapeDtypeStruct((n_g, p, p), F32),
                   jax.ShapeDtypeStruct((1, d), F32)],
        compiler_params=_params(("arbitrary",)),
    )(dout, diff, ypre, wp, scale_row)


def _pool_bwd_win(name, ddiff, x, gain_row, dout):
    t, d = x.shape
    n_g = len(POOL_WINDOWS)
    p = d // n_g
    tm = _tile(t, 256, 8)
    hb = tm // HALO
    n_t = t // tm
    rows = tm + HALO

    def body(e_ref, en_ref, x_ref, g_ref, do_ref, dx_ref, dg_ref):
        i = pl.program_id(0)
        e = e_ref[...]
        en = jnp.where(i < n_t - 1, en_ref[...], 0.0)
        ext = jnp.concatenate([e, en], axis=0)
        parts = []
        for gi, w in enumerate(POOL_WINDOWS):
            cols = slice(gi * p, (gi + 1) * p)
            s = ext[:, cols] * _inv_count(i * tm, rows, w)
            k = 1
            while k < w:
                s = s + pltpu.roll(s, rows - k, 0)
                k *= 2
            parts.append(s[:tm, :] - e[:, cols])
        dh = jnp.concatenate(parts, axis=1)
        dx, dg = _rms_bwd_math(dh, x_ref[...], g_ref[...])
        dx_ref[...] = dx + do_ref[...]

        @pl.when(i == 0)
        def _():
            dg_ref[...] = jnp.zeros_like(dg_ref)

        dg_ref[...] += dg

    row = pl.BlockSpec((tm, d), lambda i: (i, 0))
    vec = pl.BlockSpec((1, d), lambda i: (0, 0))
    return pl.pallas_call(
        body, name=name, grid=(n_t,),
        in_specs=[row, pl.BlockSpec((HALO, d), lambda i: (jnp.minimum((i + 1) * hb, t // HALO - 1), 0)), row, vec, row],
        out_specs=[row, vec],
        out_shape=[jax.ShapeDtypeStruct((t, d), F32), jax.ShapeDtypeStruct((1, d), F32)],
        compiler_params=_params(("arbitrary",)),
    )(ddiff, ddiff, x, gain_row, dout)


def _rel_index(n_rel):
    dd = lax.broadcasted_iota(jnp.int32, (n_rel, BAND_W), 1)
    e = lax.broadcasted_iota(jnp.int32, (n_rel, BAND_W), 0)
    idx = jnp.where(dd < LEFT + 2 * CHUNK, jnp.clip(LEFT - dd, -(CHUNK - 1), REL_MAX) + (CHUNK - 1), n_rel - 1)
    return (idx == e).astype(F32)


def _bias_rows(name, table):
    n_h, n_rel = table.shape

    def body(t_ref, o_ref):
        o_ref[...] = jnp.dot(t_ref[...], _rel_index(n_rel), preferred_element_type=F32,
                             precision=lax.Precision.HIGHEST)

    rows = pl.pallas_call(body, name=name, out_shape=jax.ShapeDtypeStruct((n_h, BAND_W), F32))(table)
    return rows.reshape(n_h, 1, BAND_W)


def _bias_rows_bwd(name, dbias, n_rel):
    n_h = dbias.shape[0]

    def body(db_ref, o_ref):
        rr = lax.broadcasted_iota(jnp.int32, (TQ, TQ), 0)
        cc = lax.broadcasted_iota(jnp.int32, (TQ, TQ), 1)
        flipped = jnp.dot((rr + cc == TQ - 1).astype(F32), db_ref[...], preferred_element_type=F32,
                          precision=lax.Precision.HIGHEST)
        unskew = pltpu.roll(flipped, BAND_W - TQ + 1, 1, stride=1, stride_axis=0)
        drow = jnp.sum(unskew, axis=0, keepdims=True)
        o_ref[...] = lax.dot_general(drow, _rel_index(n_rel), _DIMS["nt"], preferred_element_type=F32,
                                     precision=lax.Precision.HIGHEST)

    return pl.pallas_call(
        body, name=name, grid=(n_h,),
        in_specs=[pl.BlockSpec((None, TQ, BAND_W), lambda h: (h, 0, 0))],
        out_specs=pl.BlockSpec((None, 1, n_rel), lambda h: (h, 0, 0)),
        out_shape=jax.ShapeDtypeStruct((n_h, 1, n_rel), F32), compiler_params=_params(("parallel",)),
    )(dbias)


def _band_bias(row):
    tile = pltpu.roll(jnp.broadcast_to(row, (TQ, BAND_W)), 0, 1, stride=1, stride_axis=0)
    qc = lax.broadcasted_iota(jnp.int32, (TQ, BAND_W), 0) // CHUNK
    kc = lax.broadcasted_iota(jnp.int32, (TQ, BAND_W), 1) // CHUNK
    return jnp.where((kc >= qc) & (kc <= qc + LEFT_CHUNKS), tile, NEG_INF)


def _band_probs(qv, kb, bias, t0):
    s = lax.dot_general(qv, kb, _DIMS["nt"], preferred_element_type=F32) * (qv.shape[-1] ** -0.5) + bias
    col = lax.broadcasted_iota(jnp.int32, (TQ, BAND_W), 1)
    s = jnp.where(col >= LEFT - t0, s, NEG_INF)
    e = jnp.exp(s - jnp.max(s, axis=-1, keepdims=True))
    return e / jnp.sum(e, axis=-1, keepdims=True)


def _valid_rows(blk, t0):
    r = lax.broadcasted_iota(jnp.int32, (BAND_W, 1), 0)
    return jnp.where(r >= LEFT - t0, blk, jnp.zeros_like(blk))


def _attn_fwd(name, qr, gain_row, kpad, vpad, brow):
    t, d = qr.shape
    hd = gain_row.shape[-1]
    n_h = d // hd

    def body(q_ref, g_ref, k_ref, v_ref, b_ref, o_ref, bias_ref):
        i = pl.program_id(1)

        @pl.when(i == 0)
        def _():
            bias_ref[...] = _band_bias(b_ref[...])

        t0 = pl.multiple_of(i * TQ, TQ)
        qv = _head_rms(q_ref[...].astype(F32), g_ref[...])[0].astype(BF16)
        kb = _valid_rows(k_ref[pl.ds(t0, BAND_W), :], t0)
        vb = _valid_rows(v_ref[pl.ds(t0, BAND_W), :], t0)
        pr = _band_probs(qv, kb, bias_ref[...], t0)
        o_ref[...] = jnp.dot(pr.astype(BF16), vb, preferred_element_type=F32).astype(BF16)

    kv_spec = pl.BlockSpec((LEFT + t, hd), lambda h, i: (0, h))
    return pl.pallas_call(
        body, name=name, grid=(n_h, t // TQ),
        in_specs=[pl.BlockSpec((TQ, hd), lambda h, i: (i, h)), pl.BlockSpec((1, hd), lambda h, i: (0, 0)),
                  kv_spec, kv_spec, pl.BlockSpec((None, 1, BAND_W), lambda h, i: (h, 0, 0))],
        out_specs=pl.BlockSpec((TQ, hd), lambda h, i: (i, h)),
        out_shape=jax.ShapeDtypeStruct((t, d), BF16),
        scratch_shapes=[pltpu.VMEM((TQ, BAND_W), F32)],
        compiler_params=_params(("parallel", "arbitrary")),
    )(qr, gain_row, kpad, vpad, brow)


def _attn_bwd(name, qr, gain_row, kpad, vpad, brow, o, do, dk_in=None, dv_in=None):
    t, d = qr.shape
    hd = gain_row.shape[-1]
    n_h = d // hd
    scale = hd ** -0.5
    has_init = dk_in is not None

    def body(*refs):
        if has_init:
            (q_ref, g_ref, k_ref, v_ref, b_ref, o_ref, do_ref, dki_ref, dvi_ref,
             dq_ref, dk_ref, dv_ref, db_ref, dg_ref, bias_ref) = refs
        else:
            (q_ref, g_ref, k_ref, v_ref, b_ref, o_ref, do_ref,
             dq_ref, dk_ref, dv_ref, db_ref, dg_ref, bias_ref) = refs
        h = pl.program_id(0)
        i = pl.program_id(1)

        @pl.when(i == 0)
        def _():
            bias_ref[...] = _band_bias(b_ref[...])
            db_ref[...] = jnp.zeros_like(db_ref)
            if has_init:
                dk_ref[...] = dki_ref[...]
                dv_ref[...] = dvi_ref[...]
            else:
                dk_ref[...] = jnp.zeros_like(dk_ref)
                dv_ref[...] = jnp.zeros_like(dv_ref)

        @pl.when((i == 0) & (h == 0))
        def _():
            dg_ref[...] = jnp.zeros_like(dg_ref)

        t0 = pl.multiple_of(i * TQ, TQ)
        gain = g_ref[...]
        qraw = q_ref[...].astype(F32)
        qn, r = _head_rms(qraw, gain)
        qv = qn.astype(BF16)
        kb = _valid_rows(k_ref[pl.ds(t0, BAND_W), :], t0)
        vb = _valid_rows(v_ref[pl.ds(t0, BAND_W), :], t0)
        pr = _band_probs(qv, kb, bias_ref[...], t0)
        dov = do_ref[...]
        delta = jnp.sum(dov.astype(F32) * o_ref[...].astype(F32), axis=-1, keepdims=True)
        dp = lax.dot_general(dov, vb, _DIMS["nt"], preferred_element_type=F32)
        ds = pr * (dp - delta)
        db_ref[...] += ds
        dsb = ds.astype(BF16)
        dq = jnp.dot(dsb, kb, preferred_element_type=F32) * scale
        dk_ref[pl.ds(t0, BAND_W), :] += lax.dot_general(dsb, qv, _DIMS["tn"], preferred_element_type=F32) * scale
        dv_ref[pl.ds(t0, BAND_W), :] += lax.dot_general(pr.astype(BF16), dov, _DIMS["tn"], preferred_element_type=F32)
        u = dq * gain
        dq_ref[...] = (r * u - qraw * (r * r * r * jnp.mean(u * qraw, axis=-1, keepdims=True))).astype(BF16)
        dg_ref[...] += jnp.sum(dq * qraw * r, axis=0, keepdims=True)

    qspec = pl.BlockSpec((TQ, hd), lambda h, i: (i, h))
    kv_spec = pl.BlockSpec((LEFT + t, hd), lambda h, i: (0, h))
    gspec = pl.BlockSpec((1, hd), lambda h, i: (0, 0))
    ins = [qr, gain_row, kpad, vpad, brow, o, do] + ([dk_in, dv_in] if has_init else [])
    return pl.pallas_call(
        body, name=name, grid=(n_h, t // TQ),
        in_specs=[qspec, gspec, kv_spec, kv_spec, pl.BlockSpec((None, 1, BAND_W), lambda h, i: (h, 0, 0)), qspec, qspec]
        + ([kv_spec, kv_spec] if has_init else []),
        out_specs=[qspec, kv_spec, kv_spec, pl.BlockSpec((None, TQ, BAND_W), lambda h, i: (h, 0, 0)), gspec],
        out_shape=[jax.ShapeDtypeStruct((t, d), BF16), jax.ShapeDtypeStruct((LEFT + t, d), F32),
                   jax.ShapeDtypeStruct((LEFT + t, d), F32), jax.ShapeDtypeStruct((n_h, TQ, BAND_W), F32),
                   jax.ShapeDtypeStruct((1, hd), F32)],
        scratch_shapes=[pltpu.VMEM((TQ, BAND_W), F32)],
        compiler_params=_params(("arbitrary", "arbitrary")),
    )(*ins)


def _head_norm_bwd(name, dy, off_rows, yraw, gain_row):
    t, d = yraw.shape
    hd = gain_row.shape[-1]
    tm = _tile(t, 512, 8)
    off = off_rows // tm

    def body(dy_ref, y_ref, g_ref, dx_ref, dg_ref):
        @pl.when((pl.program_id(0) == 0) & (pl.program_id(1) == 0))
        def _():
            dg_ref[...] = jnp.zeros_like(dg_ref)

        gain = g_ref[...]
        yv = y_ref[...].astype(F32)
        dyv = dy_ref[...]
        r = lax.rsqrt(jnp.mean(yv * yv, axis=-1, keepdims=True) + EPS)
        u = dyv * gain
        dx_ref[...] = (r * u - yv * (r * r * r * jnp.mean(u * yv, axis=-1, keepdims=True))).astype(BF16)
        dg_ref[...] += jnp.sum(dyv * yv * r, axis=0, keepdims=True)

    blk = pl.BlockSpec((tm, hd), lambda i, h: (i, h))
    gspec = pl.BlockSpec((1, hd), lambda i, h: (0, 0))
    return pl.pallas_call(
        body, name=name, grid=(t // tm, d // hd),
        in_specs=[pl.BlockSpec((tm, hd), lambda i, h: (i + off, h)), blk, gspec],
        out_specs=[blk, gspec],
        out_shape=[jax.ShapeDtypeStruct((t, d), BF16), jax.ShapeDtypeStruct((1, hd), F32)],
        compiler_params=_params(("arbitrary", "arbitrary")),
    )(dy, yraw, gain_row)


def _head_norm_epilogue(hd):
    def epi(acc, gain):
        parts = [_head_rms(acc[:, c:c + hd], gain[:, c:c + hd])[0] for c in range(0, acc.shape[-1], hd)]
        return acc, jnp.concatenate(parts, axis=1)
    return epi


def _loss_head(name, y, target):
    t, d = y.shape
    tm = _tile(t, 512, 8)

    def body(y_ref, t_ref, dy_ref, l_ref):
        @pl.when(pl.program_id(0) == 0)
        def _():
            l_ref[...] = jnp.zeros_like(l_ref)

        err = y_ref[...] - t_ref[...]
        dy_ref[...] = err * (1.0 / d)
        per_tok = jnp.mean(err * err, axis=-1, keepdims=True)
        l_ref[...] += 0.5 * jnp.sum(per_tok, axis=0, keepdims=True)

    row = pl.BlockSpec((tm, d), lambda i: (i, 0))
    return pl.pallas_call(
        body, name=name, grid=(t // tm,), in_specs=[row, row],
        out_specs=[row, pl.BlockSpec((1, LANES), lambda i: (0, 0))],
        out_shape=[jax.ShapeDtypeStruct((t, d), F32), jax.ShapeDtypeStruct((1, LANES), F32)],
        compiler_params=_params(("arbitrary",)),
    )(y, target)


def _forward_backward(x, target, w, small):
    t, d = x.shape
    n_layers = small["n1"].shape[0]
    n_a = w["pool"].shape[0]
    n_b = w["wq"].shape[0]
    hd = small["kg"].shape[0]
    n_rel = small["rel"].shape[-1]
    row = lambda v: v.reshape(1, -1)
    kg_row = row(small["kg"])

    saved = []
    kv_saved = None
    kpad = vpad = None
    for l in range(n_layers):
        x, s1 = _ffn_fwd(f"l{l}_ffn1", x, row(small["n1"][l]), w["g1"], w["u1"], w["d1"], l)
        if l < n_a:
            x_in = x
            x, diff, ypre = _pool_fwd(f"l{l}_pool", x, row(small["nm"][l]), w["pool"], row(small["pscale"][l]), l)
            sm = (x_in, diff, ypre)
        else:
            b = l - n_a
            x_in = x
            h = _rms_fwd(f"l{l}_mix_norm", x, row(small["nm"][l]))
            (qr,) = _mm(f"l{l}_q", [(_op(h), _op(w["wq"], b))], t, d, d, "nn", [(t, BF16, 0)],
                        lambda acc: (acc,), tk=2048)
            brow = _bias_rows(f"l{l}_bias", small["rel"][b])
            o = _attn_fwd(f"l{l}_attn", qr, row(small["qg"][b]), kpad, vpad, brow)
            (x,) = _mm(f"l{l}_o", [(_op(o), _op(w["wo"], b))], t, d, d, "nn", [(t, F32, 0)],
                       lambda acc, xv: (xv + acc,), extras=[(x_in, None, "mn")], tk=2048)
            sm = (x_in, h, qr, brow, o)
        x, s2 = _ffn_fwd(f"l{l}_ffn2", x, row(small["n2"][l]), w["g2"], w["u2"], w["d2"], l)
        saved.append((s1, sm, s2))
        if l == n_a - 1:
            hk = _rms_fwd("kv_norm", x, row(small["kvn"]))
            pad_blocks = LEFT // _tile(t, 512, 8)
            kr, kpad = _mm("kv_k", [(_op(hk), _op(w["wk"]))], t, d, d, "nn",
                           [(t, BF16, 0), (LEFT + t, BF16, pad_blocks)],
                           _head_norm_epilogue(hd), extras=[(jnp.tile(kg_row, (1, d // hd)), None, "row")], tk=2048)
            (vpad,) = _mm("kv_v", [(_op(hk), _op(w["wv"]))], t, d, d, "nn", [(LEFT + t, BF16, pad_blocks)],
                          lambda acc: (acc,), tk=2048)
            kv_saved = (x, hk, kr)

    dx, loss = _loss_head("loss_head", x, target)

    gw = {k: [None] * n_layers for k in ("g1", "u1", "d1", "g2", "u2", "d2")}
    gw.update(pool=[None] * n_a, wq=[None] * n_b, wo=[None] * n_b)
    gs = {k: [None] * n_layers for k in ("n1", "nm", "n2")}
    gs.update(qg=[None] * n_b, rel=[None] * n_b, pscale=[None] * n_a)
    dk = dv = None
    for l in reversed(range(n_layers)):
        s1, sm, s2 = saved[l]
        if l == n_a - 1:
            x_kv, hk, kr = kv_saved
            dkr, gs["kg"] = _head_norm_bwd("kv_bwd_knorm", dk, LEFT, kr, kg_row)
            off_k = LEFT // _tile(t, 512)
            (gw["wk"],) = _mm("kv_bwd_wk", [(_op(hk), _op(dkr))], d, d, t, "tn", [(d, F32, 0)], lambda acc: (acc,),
                              tk=512)
            (gw["wv"],) = _mm("kv_bwd_wv", [(_op(hk), _op(dv, None, off_k))], d, d, t, "tn", [(d, F32, 0)],
                              lambda acc: (acc,), tk=512)
            off_m = LEFT // _tile(t, 512, 8)
            (dhk,) = _mm("kv_bwd_h", [(_op(dkr), _op(w["wk"])), (_op(dv, None, off_m), _op(w["wv"]))], t, d, d, "nt",
                         [(t, F32, 0)], lambda acc: (acc,), tk=2048)
            dx, gs["kvn"] = _rms_bwd("kv_bwd_norm", dhk, x_kv, row(small["kvn"]), dx)
        dx, gs["n2"][l], gw["g2"][l], gw["u2"][l], gw["d2"][l] = _ffn_bwd(
            f"l{l}_ffn2", dx, s2, row(small["n2"][l]), w["g2"], w["u2"], w["d2"], l)
        if l < n_a:
            x_in, diff, ypre = sm
            ddiff, gw["pool"][l], gs["pscale"][l] = _pool_bwd_mat(
                f"l{l}_pool_bwd_mat", dx, diff, ypre, w["pool"], row(small["pscale"][l]), l)
            dx, gs["nm"][l] = _pool_bwd_win(f"l{l}_pool_bwd_win", ddiff, x_in, row(small["nm"][l]), dx)
        else:
            b = l - n_a
            x_in, h, qr, brow, o = sm
            (gw["wo"][b],) = _mm(f"l{l}_bwd_wo", [(_op(o), _op(dx))], d, d, t, "tn", [(d, F32, 0)],
                                 lambda acc: (acc,), tk=512)
            (do,) = _mm(f"l{l}_bwd_o", [(_op(dx), _op(w["wo"], b))], t, d, d, "nt", [(t, BF16, 0)],
                        lambda acc: (acc,), tk=2048)
            dqr, dk, dv, dbias, gs["qg"][b] = _attn_bwd(f"l{l}_attn_bwd", qr, row(small["qg"][b]), kpad, vpad, brow,
                                                         o, do, dk, dv)
            gs["rel"][b] = _bias_rows_bwd(f"l{l}_bias_bwd", dbias, n_rel).reshape(-1, n_rel)
            (gw["wq"][b],) = _mm(f"l{l}_bwd_wq", [(_op(h), _op(dqr))], d, d, t, "tn", [(d, F32, 0)],
                                 lambda acc: (acc,), tk=512)
            (dh,) = _mm(f"l{l}_bwd_qh", [(_op(dqr), _op(w["wq"], b))], t, d, d, "nt", [(t, F32, 0)],
                        lambda acc: (acc,), tk=2048)
            dx, gs["nm"][l] = _rms_bwd(f"l{l}_mix_bwd_norm", dh, x_in, row(small["nm"][l]), dx)
        dx, gs["n1"][l], gw["g1"][l], gw["u1"][l], gw["d1"][l] = _ffn_bwd(
            f"l{l}_ffn1", dx, s1, row(small["n1"][l]), w["g1"], w["u1"], w["d1"], l)
    return loss, dx, gw, gs


def _cast_pad(name, wsh, rows_to, cols_to, dtype=BF16):
    n_l, r, c = wsh.shape
    if cols_to != c:
        tr, tc_in, tc_out = _tile(r, 256, 16), c, cols_to
    else:
        tr, tc_in, tc_out = r, _tile(c, 256), _tile(c, 256)
    r_out = rows_to if tr == r else tr

    def body(w_ref, o_ref):
        o_ref[...] = jnp.zeros_like(o_ref)
        o_ref[: w_ref.shape[0], : w_ref.shape[1]] = w_ref[...].astype(dtype)

    return pl.pallas_call(
        body, name=name, grid=(n_l, r // tr, c // tc_in),
        in_specs=[pl.BlockSpec((None, tr, tc_in), lambda l, i, j: (l, i, j))],
        out_specs=pl.BlockSpec((None, r_out, tc_out), lambda l, i, j: (l, i, j)),
        out_shape=jax.ShapeDtypeStruct((n_l, rows_to, cols_to), dtype),
        compiler_params=_params(("parallel", "parallel", "parallel")),
    )(wsh)


def _place():
    x, y, c = lax.axis_index("x"), lax.axis_index("y"), lax.axis_index("c")
    return x, y, c, 2 * x + y


def _exchange(name, srcs, out_shapes, plan):
    n_src, n_out, n_it = len(srcs), len(out_shapes), len(plan)

    def body(*refs):
        src_refs = refs[:n_src]
        out_refs = refs[n_src: n_src + n_out]
        send_sems, recv_sems, loc_sems = refs[n_src + n_out:]
        x, y, c, me = _place()
        sib = (x, y, 1 - c)
        chips = [(1 - x, y), (x, 1 - y), (1 - x, 1 - y)]

        def remote(src, dst, it, k, dev):
            return pltpu.make_async_remote_copy(src_ref=src, dst_ref=dst, send_sem=send_sems.at[it, k],
                                                recv_sem=recv_sems.at[it, k], device_id=dev, device_id_type=MESH)

        started, local = [], []
        for it, (si, oi, sp, dp) in enumerate(plan):
            src, out = src_refs[si], out_refs[oi]
            mine = sp(src, me, c)
            lc = pltpu.make_async_copy(mine, dp(out, c, me), loc_sems.at[it])
            lc.start()
            local.append(lc)
            first = [remote(sp(src, 2 * px + py, c), dp(out, c, me), it, k, (px, py, c))
                     for k, (px, py) in enumerate(chips)]
            first.append(remote(mine, dp(out, c, me), it, 3, sib))
            for cp in first:
                cp.start()
            started += first
        for it, (si, oi, sp, dp) in enumerate(plan):
            src, out = src_refs[si], out_refs[oi]
            for k, (px, py) in enumerate(chips):
                landed = dp(out, c, 2 * px + py)
                remote(landed, landed, it, k, (px, py, c)).wait_recv()
                fwd = remote(landed, landed, it, 4 + k, sib)
                fwd.start()
                started.append(fwd)
        for it, (si, oi, sp, dp) in enumerate(plan):
            out = out_refs[oi]
            theirs = dp(out, 1 - c, me)
            remote(theirs, theirs, it, 3, sib).wait_recv()
            for k, (px, py) in enumerate(chips):
                theirs = dp(out, 1 - c, 2 * px + py)
                remote(theirs, theirs, it, 4 + k, sib).wait_recv()
        for cp in started:
            cp.wait_send()
        for lc in local:
            lc.wait()

    return pl.pallas_call(
        body, name=name, in_specs=[ANY] * n_src, out_specs=[ANY] * n_out,
        out_shape=[jax.ShapeDtypeStruct(s, dt) for s, dt in out_shapes],
        scratch_shapes=[pltpu.SemaphoreType.DMA((n_it, 7)), pltpu.SemaphoreType.DMA((n_it, 7)),
                        pltpu.SemaphoreType.DMA((n_it,))],
    )(*srcs)


def _gather_weights(name, shards, fams):
    out_shapes, plan = [], []
    for n, (s, fam) in enumerate(zip(shards, fams)):
        n_l, r, cc = s.shape
        if fam == "col":
            out_shapes.append(((n_l, r, N_CHIPS * cc), s.dtype))
            sp = lambda ref, chip, c, r=r: ref.at[:, pl.ds(c * (r // 2), r // 2), :]
            dp = lambda ref, half, chip, r=r, cc=cc: ref.at[:, pl.ds(half * (r // 2), r // 2), pl.ds(chip * cc, cc)]
        else:
            out_shapes.append(((n_l, N_CHIPS * r, cc), s.dtype))
            sp = lambda ref, chip, c, cc=cc: ref.at[:, :, pl.ds(c * (cc // 2), cc // 2)]
            dp = lambda ref, half, chip, r=r, cc=cc: ref.at[:, pl.ds(chip * r, r), pl.ds(half * (cc // 2), cc // 2)]
        plan.append((n, n, sp, dp))
    return _exchange(name, shards, out_shapes, plan)


def _swap_halves(name, grads, fams):
    n = len(grads)
    out_shapes = []
    for g, fam in zip(grads, fams):
        r, cc = g.shape
        out_shapes.append((r // 2, cc) if fam == "col" else (r, cc // 2))

    def body(*refs):
        g_refs, x_refs = refs[:n], refs[n: 2 * n]
        send_sems, recv_sems = refs[2 * n:]
        x, y, c, _ = _place()
        copies = []
        for it, fam in enumerate(fams):
            r, cc = g_refs[it].shape
            if fam == "col":
                src = g_refs[it].at[pl.ds((1 - c) * (r // 2), r // 2), :]
            else:
                src = g_refs[it].at[:, pl.ds((1 - c) * (cc // 2), cc // 2)]
            cp = pltpu.make_async_remote_copy(src_ref=src, dst_ref=x_refs[it], send_sem=send_sems.at[it],
                                              recv_sem=recv_sems.at[it], device_id=(x, y, 1 - c), device_id_type=MESH)
            cp.start()
            copies.append(cp)
        for cp in copies:
            cp.wait()

    return pl.pallas_call(
        body, name=name, in_specs=[ANY] * n, out_specs=[ANY] * n,
        out_shape=[jax.ShapeDtypeStruct(s, F32) for s in out_shapes],
        scratch_shapes=[pltpu.SemaphoreType.DMA((n,)), pltpu.SemaphoreType.DMA((n,))],
    )(*grads)


def _half_sum(name, g, xr, fam, core):
    r, cc = xr.shape
    tr, tc = _tile(r, 512, 16), _tile(cc, 1024)
    nr, nc = r // tr, cc // tc

    def body(c_ref, g_ref, x_ref, s_ref):
        s_ref[...] = (g_ref[...] + x_ref[...]).astype(BF16)

    if fam == "col":
        gmap = lambda i, j, c_ref: (c_ref[0] * nr + i, j)
    else:
        gmap = lambda i, j, c_ref: (i, c_ref[0] * nc + j)
    return pl.pallas_call(
        body, name=name,
        grid_spec=pltpu.PrefetchScalarGridSpec(
            num_scalar_prefetch=1, grid=(nr, nc),
            in_specs=[pl.BlockSpec((tr, tc), gmap), pl.BlockSpec((tr, tc), lambda i, j, c_ref: (i, j))],
            out_specs=pl.BlockSpec((tr, tc), lambda i, j, c_ref: (i, j))),
        out_shape=jax.ShapeDtypeStruct((r, cc), BF16), compiler_params=_params(("parallel", "parallel")),
    )(core, g, xr)


def _scatter_sums(name, sums, fams, layers, groups):
    out_index, out_shapes, plan = {}, [], []
    n_layers = {}
    for grp, l in zip(groups, layers):
        n_layers[grp] = max(n_layers.get(grp, 0), l + 1)
    for n, (s, fam, l, grp) in enumerate(zip(sums, fams, layers, groups)):
        r, cc = s.shape
        piece = (r, cc // N_CHIPS) if fam == "col" else (r // N_CHIPS, cc)
        if grp not in out_index:
            out_index[grp] = len(out_shapes)
            out_shapes.append(((n_layers[grp], 2, N_CHIPS) + piece, BF16))
        if fam == "col":
            sp = lambda ref, chip, c, w=piece[1]: ref.at[:, pl.ds(chip * w, w)]
        else:
            sp = lambda ref, chip, c, h=piece[0]: ref.at[pl.ds(chip * h, h), :]
        dp = lambda ref, half, chip, l=l: ref.at[l, half, chip]
        plan.append((n, out_index[grp], sp, dp))
    outs = _exchange(name, sums, out_shapes, plan)
    return {grp: outs[i] for grp, i in out_index.items()}


def _allreduce_small(name, packed):
    rows = packed.shape[0]

    def body(p_ref, o_ref, slots, send_sems, recv_sems):
        x, y, c, _ = _place()
        me = 4 * x + 2 * y + c
        slots[me] = p_ref[...]
        copies = []
        for k in range(1, 8):
            peer = (x ^ (k >> 2), y ^ ((k >> 1) & 1), c ^ (k & 1))
            cp = pltpu.make_async_remote_copy(src_ref=p_ref, dst_ref=slots.at[me], send_sem=send_sems.at[k - 1],
                                              recv_sem=recv_sems.at[k - 1], device_id=peer, device_id_type=MESH)
            cp.start()
            copies.append(cp)
        for cp in copies:
            cp.wait()
        acc = slots[0]
        for k in range(1, 8):
            acc = acc + slots[k]
        o_ref[...] = acc

    return pl.pallas_call(
        body, name=name, out_shape=jax.ShapeDtypeStruct((rows, LANES), F32),
        in_specs=[pl.BlockSpec(memory_space=pltpu.VMEM)], out_specs=pl.BlockSpec(memory_space=pltpu.VMEM),
        scratch_shapes=[pltpu.VMEM((8, rows, LANES), F32), pltpu.SemaphoreType.DMA((7,)), pltpu.SemaphoreType.DMA((7,))],
    )(packed)


def _adamw_math(w, g, m, v):
    m = ADAM_B1 * m + (1.0 - ADAM_B1) * g
    v = ADAM_B2 * v + (1.0 - ADAM_B2) * (g * g)
    m_hat = m / (1.0 - ADAM_B1 ** ADAM_STEP)
    v_hat = v / (1.0 - ADAM_B2 ** ADAM_STEP)
    delta = -ADAM_LR * (m_hat / (jnp.sqrt(v_hat) + ADAM_EPS) + ADAM_WD * w)
    return delta, m, v


def _adamw_sharded(name, rbuf, w, m, v, fam):
    n_l, r, cc = w.shape
    if fam == "col":
        hr, cp = rbuf.shape[-2:]
        tr = _tile(hr, 128, 16)
        n_t = hr // tr
        rspec = pl.BlockSpec((None, None, N_CHIPS, tr, cp), lambda l, h, i: (l, h, 0, i, 0))
        wspec = pl.BlockSpec((None, tr, cc), lambda l, h, i: (l, h * n_t + i, 0))
    else:
        rp, hc = rbuf.shape[-2:]
        tc = _tile(hc, 256)
        n_t = hc // tc
        rspec = pl.BlockSpec((None, None, N_CHIPS, rp, tc), lambda l, h, i: (l, h, 0, 0, i))
        wspec = pl.BlockSpec((None, r, tc), lambda l, h, i: (l, 0, h * n_t + i))

    def body(r_ref, w_ref, m_ref, v_ref, g_out, d_out, m_out, v_out):
        rows, cols = w_ref.shape
        g = r_ref[0, :rows, :cols].astype(F32)
        for k in range(1, N_CHIPS):
            g = g + r_ref[k, :rows, :cols].astype(F32)
        delta, m_new, v_new = _adamw_math(w_ref[...], g, m_ref[...], v_ref[...])
        g_out[...] = g
        d_out[...] = delta
        m_out[...] = m_new
        v_out[...] = v_new

    return pl.pallas_call(
        body, name=name, grid=(n_l, 2, n_t), in_specs=[rspec, wspec, wspec, wspec], out_specs=[wspec] * 4,
        out_shape=[jax.ShapeDtypeStruct(w.shape, F32)] * 4,
        compiler_params=_params(("parallel", "parallel", "parallel")),
    )(rbuf, w, m, v)


def _adamw_small(name, w, g, m, v):
    def body(w_ref, g_ref, m_ref, v_ref, d_out, m_out, v_out):
        delta, m_new, v_new = _adamw_math(w_ref[...], g_ref[...], m_ref[...], v_ref[...])
        d_out[...] = delta
        m_out[...] = m_new
        v_out[...] = v_new

    return pl.pallas_call(body, name=name, out_shape=[jax.ShapeDtypeStruct(w.shape, F32)] * 3)(w, g, m, v)


def _pack(arrays):
    flat = jnp.concatenate([a.reshape(-1).astype(F32) for a in arrays])
    n = flat.shape[0]
    rows = _round_up(_round_up(n, LANES) // LANES, 8)
    return jnp.pad(flat, (0, rows * LANES - n)).reshape(rows, LANES)


def _unpack(packed, like):
    flat = packed.reshape(-1)
    out, pos = [], 0
    for a in like:
        out.append(flat[pos: pos + a.size].reshape(a.shape))
        pos += a.size
    return out


_SHARDED = ("ffn1_w_gate", "ffn1_w_up", "ffn1_w_down", "ffn2_w_gate", "ffn2_w_up", "ffn2_w_down", "pool_w",
            "w_k", "w_v", "w_q", "w_o")
_FAMILY = dict(ffn1_w_gate="col", ffn1_w_up="col", ffn2_w_gate="col", ffn2_w_up="col", ffn1_w_down="row",
               ffn2_w_down="row", pool_w="row", w_k="row", w_v="row", w_q="row", w_o="row")
_SHORT = dict(ffn1_w_gate="g1", ffn1_w_up="u1", ffn1_w_down="d1", ffn2_w_gate="g2", ffn2_w_up="u2", ffn2_w_down="d2",
              pool_w="pool", w_k="wk", w_v="wv", w_q="wq", w_o="wo")
_SMALL = ("ffn1_norm", "mix_norm", "ffn2_norm", "kv_norm", "k_gain", "q_gain", "rel_bias", "pool_scale")
_WEIGHTS = ("ffn1_norm", "ffn1_w_gate", "ffn1_w_up", "ffn1_w_down", "mix_norm", "ffn2_norm", "ffn2_w_gate",
            "ffn2_w_up", "ffn2_w_down", "pool_w", "pool_scale", "kv_norm", "w_k", "w_v", "k_gain", "w_q", "q_gain",
            "rel_bias", "w_o")


def _as3d(name, a):
    if name == "pool_w":
        return a.reshape(a.shape[0], a.shape[1] * a.shape[2], a.shape[3])
    if a.ndim == 2:
        return a[None]
    return a


def kernel(x, ffn1_norm, ffn1_w_gate, ffn1_w_up, ffn1_w_down, mix_norm, ffn2_norm, ffn2_w_gate, ffn2_w_up, ffn2_w_down, pool_w, pool_scale, kv_norm, w_k, w_v, k_gain, w_q, q_gain, rel_bias, w_o, loss_target, m_ffn1_norm, m_ffn1_w_gate, m_ffn1_w_up, m_ffn1_w_down, m_mix_norm, m_ffn2_norm, m_ffn2_w_gate, m_ffn2_w_up, m_ffn2_w_down, m_pool_w, m_pool_scale, m_kv_norm, m_w_k, m_w_v, m_k_gain, m_w_q, m_q_gain, m_rel_bias, m_w_o, v_ffn1_norm, v_ffn1_w_gate, v_ffn1_w_up, v_ffn1_w_down, v_mix_norm, v_ffn2_norm, v_ffn2_w_gate, v_ffn2_w_up, v_ffn2_w_down, v_pool_w, v_pool_scale, v_kv_norm, v_w_k, v_w_v, v_k_gain, v_w_q, v_q_gain, v_rel_bias, v_w_o):
    wts = dict(ffn1_norm=ffn1_norm, ffn1_w_gate=ffn1_w_gate, ffn1_w_up=ffn1_w_up, ffn1_w_down=ffn1_w_down,
               mix_norm=mix_norm, ffn2_norm=ffn2_norm, ffn2_w_gate=ffn2_w_gate, ffn2_w_up=ffn2_w_up,
               ffn2_w_down=ffn2_w_down, pool_w=pool_w, pool_scale=pool_scale, kv_norm=kv_norm, w_k=w_k, w_v=w_v,
               k_gain=k_gain, w_q=w_q, q_gain=q_gain, rel_bias=rel_bias, w_o=w_o)
    mom = dict(ffn1_norm=m_ffn1_norm, ffn1_w_gate=m_ffn1_w_gate, ffn1_w_up=m_ffn1_w_up, ffn1_w_down=m_ffn1_w_down,
               mix_norm=m_mix_norm, ffn2_norm=m_ffn2_norm, ffn2_w_gate=m_ffn2_w_gate, ffn2_w_up=m_ffn2_w_up,
               ffn2_w_down=m_ffn2_w_down, pool_w=m_pool_w, pool_scale=m_pool_scale, kv_norm=m_kv_norm, w_k=m_w_k,
               w_v=m_w_v, k_gain=m_k_gain, w_q=m_w_q, q_gain=m_q_gain, rel_bias=m_rel_bias, w_o=m_w_o)
    var = dict(ffn1_norm=v_ffn1_norm, ffn1_w_gate=v_ffn1_w_gate, ffn1_w_up=v_ffn1_w_up, ffn1_w_down=v_ffn1_w_down,
               mix_norm=v_mix_norm, ffn2_norm=v_ffn2_norm, ffn2_w_gate=v_ffn2_w_gate, ffn2_w_up=v_ffn2_w_up,
               ffn2_w_down=v_ffn2_w_down, pool_w=v_pool_w, pool_scale=v_pool_scale, kv_norm=v_kv_norm, w_k=v_w_k,
               w_v=v_w_v, k_gain=v_k_gain, w_q=v_w_q, q_gain=v_q_gain, rel_bias=v_rel_bias, w_o=v_w_o)
    t, d = x.shape[1], x.shape[2]
    chip = 2 * lax.axis_index("x") + lax.axis_index("y")
    core = lax.axis_index("c").astype(jnp.int32).reshape(1)
    n_groups = len(POOL_WINDOWS)
    p = d // n_groups

    shards, fams = [], []
    for name in _SHARDED:
        a = _as3d(name, wts[name])
        fam = _FAMILY[name]
        rows_to = a.shape[1] if fam == "col" else _round_up(a.shape[1], LANES)
        cols_to = _round_up(a.shape[2], LANES) if fam == "col" else a.shape[2]
        shards.append(_cast_pad(f"cast_{name}", a, rows_to, cols_to))
        fams.append(fam)
    shards.append(_cast_pad("cast_pool_scale", pool_scale[None], 16, pool_scale.shape[1], F32))
    fams.append("col")
    full = _gather_weights("gather_weights", shards, fams)
    w = {_SHORT[name]: f for name, f in zip(_SHARDED, full)}
    n_a = pool_w.shape[0]
    w["pool"] = (w["pool"].reshape(n_a, N_CHIPS, n_groups, p // N_CHIPS, p).transpose(0, 2, 1, 3, 4)
                 .reshape(n_a, n_groups, p, p))
    w["wk"], w["wv"] = w["wk"][0], w["wv"][0]
    small = dict(n1=ffn1_norm, nm=mix_norm, n2=ffn2_norm, kvn=kv_norm, kg=k_gain, qg=q_gain, rel=rel_bias,
                 pscale=full[-1][0, : pool_scale.shape[0]])

    loss, dx, gw, gs = _forward_backward(x[0], loss_target[0], w, small)

    grads, gfam, glayer, ggroup = [], [], [], []
    for name in _SHARDED:
        g = gw[_SHORT[name]]
        for l, gl in enumerate(g if isinstance(g, list) else [g]):
            if name == "pool_w":
                gl = (gl.reshape(n_groups, N_CHIPS, p // N_CHIPS, p).transpose(1, 0, 2, 3).reshape(n_groups * p, p))
            grads.append(gl)
            gfam.append(_FAMILY[name])
            glayer.append(l)
            ggroup.append(name)
    theirs = _swap_halves("swap_halves", grads, gfam)
    sums = [_half_sum(f"half_sum_{grp}_{l}", g, xr, fam, core)
            for g, xr, fam, l, grp in zip(grads, theirs, gfam, glayer, ggroup)]
    rbufs = _scatter_sums("scatter_sums", sums, gfam, glayer, ggroup)

    small_grads = [jnp.concatenate(gs["n1"]), jnp.concatenate(gs["nm"]), jnp.concatenate(gs["n2"]), gs["kvn"], gs["kg"],
                   jnp.concatenate(gs["qg"]), jnp.stack(gs["rel"]), jnp.concatenate(gs["pscale"]), loss[:, :1]]
    red = _unpack(_allreduce_small("allreduce_small", _pack(small_grads)), small_grads)
    g_small = dict(ffn1_norm=red[0], mix_norm=red[1], ffn2_norm=red[2], kv_norm=red[3].reshape(-1),
                   k_gain=red[4].reshape(-1), q_gain=red[5], rel_bias=red[6],
                   pool_scale=lax.dynamic_slice_in_dim(red[7], chip * pool_scale.shape[1], pool_scale.shape[1], 1))
    loss_out = red[8].reshape(())

    grad, delta, new_m, new_v = {}, {}, {}, {}
    for name in _SHARDED:
        outs = _adamw_sharded(f"adamw_{name}", rbufs[name], _as3d(name, wts[name]), _as3d(name, mom[name]),
                              _as3d(name, var[name]), _FAMILY[name])
        grad[name], delta[name], new_m[name], new_v[name] = [o.reshape(wts[name].shape) for o in outs]
    sw = [wts[n] for n in _SMALL]
    sg = [g_small[n].reshape(wts[n].shape) for n in _SMALL]
    d_s, m_s, v_s = _adamw_small("adamw_small", _pack(sw), _pack(sg), _pack([mom[n] for n in _SMALL]),
                                 _pack([var[n] for n in _SMALL]))
    for n, g_n, d_n, m_n, v_n in zip(_SMALL, sg, _unpack(d_s, sw), _unpack(m_s, sw), _unpack(v_s, sw)):
        grad[n], delta[n], new_m[n], new_v[n] = g_n, d_n, m_n, v_n
    return (loss_out, dx[None], *[grad[n] for n in _WEIGHTS], *[delta[n] for n in _WEIGHTS],
            *[new_m[n] for n in _WEIGHTS], *[new_v[n] for n in _WEIGHTS])
```

```python
import functools

import jax
import jax.numpy as jnp
from jax import lax
from jax.experimental import pallas as pl
from jax.experimental.pallas import tpu as pltpu

F32 = jnp.float32
BF16 = jnp.bfloat16

EPS = 1e-6
CHUNK = 64
LEFT = 512
LEFT_CHUNKS = LEFT // CHUNK
REL_MAX = 128
POOL_WINDOWS = (2, 4, 8, 16)
HALO = 16
NEG_INF = -1e30
TQ = 256
BAND_W = TQ + LEFT
N_CHIPS = 4
LANES = 128
VMEM_LIMIT = 56 * 1024 * 1024

ADAM_LR = 0.001
ADAM_B1 = 0.9
ADAM_B2 = 0.999
ADAM_EPS = 1e-08
ADAM_WD = 0.01
ADAM_STEP = 10

MESH = pl.DeviceIdType.MESH
ANY = pl.BlockSpec(memory_space=pl.ANY)


def _round_up(n, m):
    return (n + m - 1) // m * m


def _tile(n, pref, unit=LANES):
    if n <= pref:
        return n
    t = pref // unit * unit
    while t >= unit:
        if n % t == 0:
            return t
        t -= unit
    return n


def _params(sem):
    return pltpu.CompilerParams(dimension_semantics=sem, vmem_limit_bytes=VMEM_LIMIT)


def _bs(block, imap, lead=None):
    if lead is None:
        return pl.BlockSpec(tuple(block), imap)
    return pl.BlockSpec((None,) + tuple(block), lambda *g: (lead,) + tuple(imap(*g)))


_DIMS = {
    "nn": (((1,), (0,)), ((), ())),
    "nt": (((1,), (1,)), ((), ())),
    "tn": (((0,), (0,)), ((), ())),
}


def _mm(name, pairs, m, n, k, mode, outs, epilogue, extras=(), tm=512, tn=512, tk=512):
    tm, tn, tk = _tile(m, tm, LANES if mode == "tn" else 8), _tile(n, tn), _tile(k, tk)
    nk = k // tk
    n_pairs, n_extra, n_out = len(pairs), len(extras), len(outs)
    in_specs, operands = [], []
    for (a, la, oa), (b, lb, ob) in pairs:
        if mode == "tn":
            in_specs.append(_bs((tk, tm), lambda i, j, kk, o=oa: (kk + o, i), la))
        else:
            in_specs.append(_bs((tm, tk), lambda i, j, kk, o=oa: (i + o, kk), la))
        if mode == "nt":
            in_specs.append(_bs((tn, tk), lambda i, j, kk, o=ob: (j + o, kk), lb))
        else:
            in_specs.append(_bs((tk, tn), lambda i, j, kk, o=ob: (kk + o, j), lb))
        operands += [a, b]
    for e, le, kind in extras:
        if kind == "mn":
            in_specs.append(_bs((tm, tn), lambda i, j, kk: (i, j), le))
        else:
            in_specs.append(_bs((1, tn), lambda i, j, kk: (0, j), le))
        operands.append(e)
    out_specs = [pl.BlockSpec((tm, tn), lambda i, j, kk, o=off: (i + o, j)) for _, _, off in outs]
    out_shape = [jax.ShapeDtypeStruct((rows, n), dt) for rows, dt, _ in outs]
    dims = _DIMS[mode]

    def body(*refs):
        ab = refs[: 2 * n_pairs]
        ex = refs[2 * n_pairs: 2 * n_pairs + n_extra]
        out_refs = refs[2 * n_pairs + n_extra: 2 * n_pairs + n_extra + n_out]
        acc = refs[-1]
        kk = pl.program_id(2)

        @pl.when(kk == 0)
        def _():
            acc[...] = jnp.zeros_like(acc)

        for p in range(n_pairs):
            a = ab[2 * p][...].astype(BF16)
            b = ab[2 * p + 1][...].astype(BF16)
            acc[...] += lax.dot_general(a, b, dims, preferred_element_type=F32)

        @pl.when(kk == nk - 1)
        def _():
            res = epilogue(acc[...], *[e[...] for e in ex])
            for o_ref, o in zip(out_refs, res):
                o_ref[...] = o.astype(o_ref.dtype)

    return pl.pallas_call(
        body, name=name, grid=(m // tm, n // tn, nk), in_specs=in_specs, out_specs=out_specs, out_shape=out_shape,
        scratch_shapes=[pltpu.VMEM((tm, tn), F32)], compiler_params=_params(("parallel", "parallel", "arbitrary")),
    )(*operands)


def _op(a, lead=None, off=0):
    return (a, lead, off)


def _head_rms(blk, gain):
    r = lax.rsqrt(jnp.mean(blk * blk, axis=-1, keepdims=True) + EPS)
    return blk * r * gain, r


def _rms_fwd(name, x, gain_row):
    t, d = x.shape
    tm = _tile(t, 512)

    def body(x_ref, g_ref, h_ref, ht_ref):
        xv = x_ref[...]
        r = lax.rsqrt(jnp.mean(xv * xv, axis=-1, keepdims=True) + EPS)
        h = (xv * r * g_ref[...]).astype(BF16)
        h_ref[...] = h
        ht_ref[...] = h.T

    return pl.pallas_call(
        body, name=name, grid=(t // tm,),
        in_specs=[pl.BlockSpec((tm, d), lambda i: (i, 0)), pl.BlockSpec((1, d), lambda i: (0, 0))],
        out_specs=[pl.BlockSpec((tm, d), lambda i: (i, 0)), pl.BlockSpec((d, tm), lambda i: (0, i))],
        out_shape=[jax.ShapeDtypeStruct((t, d), BF16), jax.ShapeDtypeStruct((d, t), BF16)],
        compiler_params=_params(("parallel",)),
    )(x, gain_row)


def _rms_bwd_math(dh, xv, g):
    r = lax.rsqrt(jnp.mean(xv * xv, axis=-1, keepdims=True) + EPS)
    u = dh * g
    dx = r * u - xv * (r * r * r * jnp.mean(u * xv, axis=-1, keepdims=True))
    dg = jnp.sum(dh * xv * r, axis=0, keepdims=True)
    return dx, dg


def _rms_bwd(name, dh, x, gain_row, dres=None):
    t, d = x.shape
    tm = _tile(t, 256, 8)
    has_res = dres is not None

    def body(*refs):
        if has_res:
            dh_ref, x_ref, g_ref, dres_ref, dx_ref, dxb_ref, dg_ref = refs
        else:
            dh_ref, x_ref, g_ref, dx_ref, dxb_ref, dg_ref = refs
        dx, dg = _rms_bwd_math(dh_ref[...], x_ref[...], g_ref[...])
        if has_res:
            dx = dx + dres_ref[...]
        dx_ref[...] = dx
        dxb_ref[...] = dx.astype(BF16)

        @pl.when(pl.program_id(0) == 0)
        def _():
            dg_ref[...] = jnp.zeros_like(dg_ref)

        dg_ref[...] += dg

    row = pl.BlockSpec((tm, d), lambda i: (i, 0))
    vec = pl.BlockSpec((1, d), lambda i: (0, 0))
    return pl.pallas_call(
        body, name=name, grid=(t // tm,),
        in_specs=[row, row, vec] + ([row] if has_res else []),
        out_specs=[row, row, vec],
        out_shape=[jax.ShapeDtypeStruct((t, d), F32), jax.ShapeDtypeStruct((t, d), BF16),
                   jax.ShapeDtypeStruct((1, d), F32)],
        compiler_params=_params(("arbitrary",)),
    )(*([dh, x, gain_row] + ([dres] if has_res else [])))


def _ffn_up(name, h, wg, wu, layer):
    t, d = h.shape
    f = wg.shape[-1]
    tm, tn = _tile(t, 1024), _tile(f, 512)

    def body(h_ref, wg_ref, wu_ref, g_ref, u_ref, a_ref, at_ref):
        hv = h_ref[...]
        g = jnp.dot(hv, wg_ref[...], preferred_element_type=F32)
        u = jnp.dot(hv, wu_ref[...], preferred_element_type=F32)
        g_ref[...] = g.astype(BF16)
        u_ref[...] = u.astype(BF16)
        a = (g * jax.nn.sigmoid(g) * u).astype(BF16)
        a_ref[...] = a
        at_ref[...] = a.T

    wspec = _bs((d, tn), lambda i, j: (0, j), layer)
    ospec = pl.BlockSpec((tm, tn), lambda i, j: (i, j))
    return pl.pallas_call(
        body, name=name, grid=(t // tm, f // tn),
        in_specs=[pl.BlockSpec((tm, d), lambda i, j: (i, 0)), wspec, wspec],
        out_specs=[ospec] * 3 + [pl.BlockSpec((tn, tm), lambda i, j: (j, i))],
        out_shape=[jax.ShapeDtypeStruct((t, f), BF16)] * 3 + [jax.ShapeDtypeStruct((f, t), BF16)],
        compiler_params=_params(("parallel", "parallel")),
    )(h, wg, wu)


def _ffn_fwd(tag, x, gain_row, wg, wu, wd, layer):
    t, d = x.shape
    f = wg.shape[-1]
    h, ht = _rms_fwd(f"{tag}_norm", x, gain_row)
    g, u, a, at = _ffn_up(f"{tag}_up", h, wg, wu, layer)
    (x_new,) = _mm(f"{tag}_down", [(_op(a), _op(wd, layer))], t, d, f, "nn", [(t, F32, 0)],
                   lambda acc, xv: (xv + 0.5 * acc,), extras=[(x, None, "mn")], tm=1024, tn=1024, tk=1408)
    return x_new, (x, ht, g, u, at)


def _ffn_bwd(tag, dout, doutb, saved, gain_row, wg, wu, wd, layer):
    x, ht, g, u, at = saved
    t, d = x.shape
    f = wg.shape[-1]

    def act_bwd(acc, gv, uv):
        da = 0.5 * acc
        gv = gv.astype(F32)
        uv = uv.astype(F32)
        sig = jax.nn.sigmoid(gv)
        du = da * (gv * sig)
        dg = da * uv * (sig * (1.0 + gv * (1.0 - sig)))
        return dg, du

    dg, du = _mm(f"{tag}_bwd_act", [(_op(doutb), _op(wd, layer))], t, f, d, "nt", [(t, BF16, 0), (t, BF16, 0)],
                 act_bwd, extras=[(g, None, "mn"), (u, None, "mn")], tm=1024, tn=512, tk=2048)
    (dwd,) = _mm(f"{tag}_bwd_wd", [(_op(at), _op(doutb))], f, d, t, "nn", [(f, F32, 0)],
                 lambda acc: (0.5 * acc,), tm=1408, tn=1024, tk=512)
    (dwg,) = _mm(f"{tag}_bwd_wg", [(_op(ht), _op(dg))], d, f, t, "nn", [(d, F32, 0)], lambda acc: (acc,),
                 tm=1024, tn=1408, tk=512)
    (dwu,) = _mm(f"{tag}_bwd_wu", [(_op(ht), _op(du))], d, f, t, "nn", [(d, F32, 0)], lambda acc: (acc,),
                 tm=1024, tn=1408, tk=512)
    (dh,) = _mm(f"{tag}_bwd_h", [(_op(dg), _op(wg, layer)), (_op(du), _op(wu, layer))], t, d, f, "nt",
                [(t, F32, 0)], lambda acc: (acc,), tm=1024, tn=1024, tk=1408)
    dx, dxb, dgain = _rms_bwd(f"{tag}_bwd_norm", dh, x, gain_row, dout)
    return dx, dxb, dgain, dwg, dwu, dwd


def _inv_count(t0, rows, window):
    tpos = t0 + lax.broadcasted_iota(jnp.int32, (rows, 1), 0)
    return 1.0 / jnp.minimum(tpos + 1, window).astype(F32)


def _pool_fwd(name, x, gain_row, wp, scale_row, layer):
    t, d = x.shape
    n_g = len(POOL_WINDOWS)
    p = d // n_g
    tm = _tile(t, 256, 8)
    hb = tm // HALO

    def body(x_ref, xp_ref, g_ref, wp_ref, s_ref, out_ref, diff_ref, ypre_ref):
        i = pl.program_id(0)
        gain = g_ref[...]

        def norm(v):
            return v * lax.rsqrt(jnp.mean(v * v, axis=-1, keepdims=True) + EPS) * gain

        xv = x_ref[...]
        h = norm(xv)
        hp = jnp.where(i > 0, norm(xp_ref[...]), 0.0)
        ext = jnp.concatenate([hp, h], axis=0)
        for gi, w in enumerate(POOL_WINDOWS):
            cols = slice(gi * p, (gi + 1) * p)
            s = ext[:, cols]
            k = 1
            while k < w:
                s = s + pltpu.roll(s, k, 0)
                k *= 2
            pooled = s[HALO:, :] * _inv_count(i * tm, tm, w)
            diff = (pooled - h[:, cols]).astype(BF16)
            y = jnp.dot(diff, wp_ref[gi], preferred_element_type=F32)
            diff_ref[:, cols] = diff
            ypre_ref[:, cols] = y.astype(BF16)
            out_ref[:, cols] = xv[:, cols] + y * s_ref[:, cols]

    row = pl.BlockSpec((tm, d), lambda i: (i, 0))
    vec = pl.BlockSpec((1, d), lambda i: (0, 0))
    return pl.pallas_call(
        body, name=name, grid=(t // tm,),
        in_specs=[row, pl.BlockSpec((HALO, d), lambda i: (jnp.maximum(i * hb - 1, 0), 0)), vec,
                  _bs((n_g, p, p), lambda i: (0, 0, 0), layer), vec],
        out_specs=[row, row, row],
        out_shape=[jax.ShapeDtypeStruct((t, d), F32), jax.ShapeDtypeStruct((t, d), BF16),
                   jax.ShapeDtypeStruct((t, d), BF16)],
        compiler_params=_params(("parallel",)),
    )(x, x, gain_row, wp, scale_row)


def _pool_bwd_mat(name, dout, diff, ypre, wp, scale_row, layer):
    t, d = dout.shape
    n_g = len(POOL_WINDOWS)
    p = d // n_g
    tm = _tile(t, 256, 8)

    def body(do_ref, diff_ref, ypre_ref, wp_ref, s_ref, dd_ref, dwp_ref, ds_ref):
        @pl.when(pl.program_id(0) == 0)
        def _():
            dwp_ref[...] = jnp.zeros_like(dwp_ref)
            ds_ref[...] = jnp.zeros_like(ds_ref)

        dov = do_ref[...]
        ds_ref[...] += jnp.sum(dov * ypre_ref[...].astype(F32), axis=0, keepdims=True)
        dy = (dov * s_ref[...]).astype(BF16)
        for gi in range(n_g):
            cols = slice(gi * p, (gi + 1) * p)
            dwp_ref[gi] += lax.dot_general(diff_ref[:, cols], dy[:, cols], _DIMS["tn"], preferred_element_type=F32)
            dd_ref[:, cols] = lax.dot_general(dy[:, cols], wp_ref[gi], _DIMS["nt"], preferred_element_type=F32)

    row = pl.BlockSpec((tm, d), lambda i: (i, 0))
    vec = pl.BlockSpec((1, d), lambda i: (0, 0))
    return pl.pallas_call(
        body, name=name, grid=(t // tm,),
        in_specs=[row, row, row, _bs((n_g, p, p), lambda i: (0, 0, 0), layer), vec],
        out_specs=[row, pl.BlockSpec((n_g, p, p), lambda i: (0, 0, 0)), vec],
        out_shape=[jax.ShapeDtypeStruct((t, d), F32), jax.ShapeDtypeStruct((n_g, p, p), F32),
                   jax.ShapeDtypeStruct((1, d), F32)],
        compiler_params=_params(("arbitrary",)),
    )(dout, diff, ypre, wp, scale_row)


def _pool_bwd_win(name, ddiff, x, gain_row, dout):
    t, d = x.shape
    n_g = len(POOL_WINDOWS)
    p = d // n_g
    tm = _tile(t, 256, 8)
    hb = tm // HALO
    n_t = t // tm
    rows = tm + HALO

    def body(e_ref, en_ref, x_ref, g_ref, do_ref, dx_ref, dxb_ref, dg_ref):
        i = pl.program_id(0)
        e = e_ref[...]
        en = jnp.where(i < n_t - 1, en_ref[...], 0.0)
        ext = jnp.concatenate([e, en], axis=0)
        parts = []
        for gi, w in enumerate(POOL_WINDOWS):
            cols = slice(gi * p, (gi + 1) * p)
            s = ext[:, cols] * _inv_count(i * tm, rows, w)
            k = 1
            while k < w:
                s = s + pltpu.roll(s, rows - k, 0)
                k *= 2
            parts.append(s[:tm, :] - e[:, cols])
        dh = jnp.concatenate(parts, axis=1)
        dx, dg = _rms_bwd_math(dh, x_ref[...], g_ref[...])
        dx = dx + do_ref[...]
        dx_ref[...] = dx
        dxb_ref[...] = dx.astype(BF16)

        @pl.when(i == 0)
        def _():
            dg_ref[...] = jnp.zeros_like(dg_ref)

        dg_ref[...] += dg

    row = pl.BlockSpec((tm, d), lambda i: (i, 0))
    vec = pl.BlockSpec((1, d), lambda i: (0, 0))
    return pl.pallas_call(
        body, name=name, grid=(n_t,),
        in_specs=[row, pl.BlockSpec((HALO, d), lambda i: (jnp.minimum((i + 1) * hb, t // HALO - 1), 0)), row, vec, row],
        out_specs=[row, row, vec],
        out_shape=[jax.ShapeDtypeStruct((t, d), F32), jax.ShapeDtypeStruct((t, d), BF16),
                   jax.ShapeDtypeStruct((1, d), F32)],
        compiler_params=_params(("arbitrary",)),
    )(ddiff, ddiff, x, gain_row, dout)


def _rel_index(n_rel):
    dd = lax.broadcasted_iota(jnp.int32, (n_rel, BAND_W), 1)
    e = lax.broadcasted_iota(jnp.int32, (n_rel, BAND_W), 0)
    idx = jnp.where(dd < LEFT + 2 * CHUNK, jnp.clip(LEFT - dd, -(CHUNK - 1), REL_MAX) + (CHUNK - 1), n_rel - 1)
    return (idx == e).astype(F32)


def _bias_rows(name, table):
    n_h, n_rel = table.shape

    def body(t_ref, o_ref):
        o_ref[...] = jnp.dot(t_ref[...], _rel_index(n_rel), preferred_element_type=F32,
                             precision=lax.Precision.HIGHEST)

    rows = pl.pallas_call(body, name=name, out_shape=jax.ShapeDtypeStruct((n_h, BAND_W), F32))(table)
    return rows.reshape(n_h, 1, BAND_W)


def _bias_rows_bwd(name, dbias, n_rel):
    n_h = dbias.shape[0]

    def body(db_ref, o_ref):
        rr = lax.broadcasted_iota(jnp.int32, (TQ, TQ), 0)
        cc = lax.broadcasted_iota(jnp.int32, (TQ, TQ), 1)
        flipped = jnp.dot((rr + cc == TQ - 1).astype(F32), db_ref[...], preferred_element_type=F32,
                          precision=lax.Precision.HIGHEST)
        unskew = pltpu.roll(flipped, BAND_W - TQ + 1, 1, stride=1, stride_axis=0)
        drow = jnp.sum(unskew, axis=0, keepdims=True)
        o_ref[...] = lax.dot_general(drow, _rel_index(n_rel), _DIMS["nt"], preferred_element_type=F32,
                                     precision=lax.Precision.HIGHEST)

    return pl.pallas_call(
        body, name=name, grid=(n_h,),
        in_specs=[pl.BlockSpec((None, TQ, BAND_W), lambda h: (h, 0, 0))],
        out_specs=pl.BlockSpec((None, 1, n_rel), lambda h: (h, 0, 0)),
        out_shape=jax.ShapeDtypeStruct((n_h, 1, n_rel), F32), compiler_params=_params(("parallel",)),
    )(dbias)


def _band_bias(row):
    tile = pltpu.roll(jnp.broadcast_to(row, (TQ, BAND_W)), 0, 1, stride=1, stride_axis=0)
    qc = lax.broadcasted_iota(jnp.int32, (TQ, BAND_W), 0) // CHUNK
    kc = lax.broadcasted_iota(jnp.int32, (TQ, BAND_W), 1) // CHUNK
    return jnp.where((kc >= qc) & (kc <= qc + LEFT_CHUNKS), tile, NEG_INF)


def _band_probs(qv, kb, bias, t0):
    s = lax.dot_general(qv, kb, _DIMS["nt"], preferred_element_type=F32) * (qv.shape[-1] ** -0.5) + bias
    col = lax.broadcasted_iota(jnp.int32, (TQ, BAND_W), 1)
    s = jnp.where(col >= LEFT - t0, s, NEG_INF)
    e = jnp.exp(s - jnp.max(s, axis=-1, keepdims=True))
    return e / jnp.sum(e, axis=-1, keepdims=True)


def _valid_rows(blk, t0):
    r = lax.broadcasted_iota(jnp.int32, (BAND_W, 1), 0)
    return jnp.where(r >= LEFT - t0, blk, jnp.zeros_like(blk))


def _attn_fwd(name, qr, gain_row, kpad, vpad, brow):
    t, d = qr.shape
    hd = gain_row.shape[-1]
    n_h = d // hd

    def body(q_ref, g_ref, k_ref, v_ref, b_ref, o_ref, ot_ref, bias_ref):
        i = pl.program_id(1)

        @pl.when(i == 0)
        def _():
            bias_ref[...] = _band_bias(b_ref[...])

        t0 = pl.multiple_of(i * TQ, TQ)
        qv = _head_rms(q_ref[...].astype(F32), g_ref[...])[0].astype(BF16)
        kb = _valid_rows(k_ref[pl.ds(t0, BAND_W), :], t0)
        vb = _valid_rows(v_ref[pl.ds(t0, BAND_W), :], t0)
        pr = _band_probs(qv, kb, bias_ref[...], t0)
        o = jnp.dot(pr.astype(BF16), vb, preferred_element_type=F32).astype(BF16)
        o_ref[...] = o
        ot_ref[...] = o.T

    kv_spec = pl.BlockSpec((LEFT + t, hd), lambda h, i: (0, h))
    return pl.pallas_call(
        body, name=name, grid=(n_h, t // TQ),
        in_specs=[pl.BlockSpec((TQ, hd), lambda h, i: (i, h)), pl.BlockSpec((1, hd), lambda h, i: (0, 0)),
                  kv_spec, kv_spec, pl.BlockSpec((None, 1, BAND_W), lambda h, i: (h, 0, 0))],
        out_specs=[pl.BlockSpec((TQ, hd), lambda h, i: (i, h)), pl.BlockSpec((hd, TQ), lambda h, i: (h, i))],
        out_shape=[jax.ShapeDtypeStruct((t, d), BF16), jax.ShapeDtypeStruct((d, t), BF16)],
        scratch_shapes=[pltpu.VMEM((TQ, BAND_W), F32)],
        compiler_params=_params(("parallel", "arbitrary")),
    )(qr, gain_row, kpad, vpad, brow)


def _attn_bwd(name, qr, gain_row, kpad, vpad, brow, o, do, dk_in=None, dv_in=None):
    t, d = qr.shape
    hd = gain_row.shape[-1]
    n_h = d // hd
    scale = hd ** -0.5
    has_init = dk_in is not None

    def body(*refs):
        if has_init:
            (q_ref, g_ref, k_ref, v_ref, b_ref, o_ref, do_ref, dki_ref, dvi_ref,
             dq_ref, dk_ref, dv_ref, db_ref, dg_ref, bias_ref) = refs
        else:
            (q_ref, g_ref, k_ref, v_ref, b_ref, o_ref, do_ref,
             dq_ref, dk_ref, dv_ref, db_ref, dg_ref, bias_ref) = refs
        h = pl.program_id(0)
        i = pl.program_id(1)

        @pl.when(i == 0)
        def _():
            bias_ref[...] = _band_bias(b_ref[...])
            db_ref[...] = jnp.zeros_like(db_ref)
            if has_init:
                dk_ref[...] = dki_ref[...]
                dv_ref[...] = dvi_ref[...]
            else:
                dk_ref[...] = jnp.zeros_like(dk_ref)
                dv_ref[...] = jnp.zeros_like(dv_ref)

        @pl.when((i == 0) & (h == 0))
        def _():
            dg_ref[...] = jnp.zeros_like(dg_ref)

        t0 = pl.multiple_of(i * TQ, TQ)
        gain = g_ref[...]
        qraw = q_ref[...].astype(F32)
        qn, r = _head_rms(qraw, gain)
        qv = qn.astype(BF16)
        kb = _valid_rows(k_ref[pl.ds(t0, BAND_W), :], t0)
        vb = _valid_rows(v_ref[pl.ds(t0, BAND_W), :], t0)
        pr = _band_probs(qv, kb, bias_ref[...], t0)
        dov = do_ref[...]
        delta = jnp.sum(dov.astype(F32) * o_ref[...].astype(F32), axis=-1, keepdims=True)
        dp = lax.dot_general(dov, vb, _DIMS["nt"], preferred_element_type=F32)
        ds = pr * (dp - delta)
        db_ref[...] += ds
        dsb = ds.astype(BF16)
        dq = jnp.dot(dsb, kb, preferred_element_type=F32) * scale
        dk_ref[pl.ds(t0, BAND_W), :] += lax.dot_general(dsb, qv, _DIMS["tn"], preferred_element_type=F32) * scale
        dv_ref[pl.ds(t0, BAND_W), :] += lax.dot_general(pr.astype(BF16), dov, _DIMS["tn"], preferred_element_type=F32)
        u = dq * gain
        dq_ref[...] = (r * u - qraw * (r * r * r * jnp.mean(u * qraw, axis=-1, keepdims=True))).astype(BF16)
        dg_ref[...] += jnp.sum(dq * qraw * r, axis=0, keepdims=True)

    qspec = pl.BlockSpec((TQ, hd), lambda h, i: (i, h))
    kv_spec = pl.BlockSpec((LEFT + t, hd), lambda h, i: (0, h))
    gspec = pl.BlockSpec((1, hd), lambda h, i: (0, 0))
    ins = [qr, gain_row, kpad, vpad, brow, o, do] + ([dk_in, dv_in] if has_init else [])
    return pl.pallas_call(
        body, name=name, grid=(n_h, t // TQ),
        in_specs=[qspec, gspec, kv_spec, kv_spec, pl.BlockSpec((None, 1, BAND_W), lambda h, i: (h, 0, 0)), qspec, qspec]
        + ([kv_spec, kv_spec] if has_init else []),
        out_specs=[qspec, kv_spec, kv_spec, pl.BlockSpec((None, TQ, BAND_W), lambda h, i: (h, 0, 0)), gspec],
        out_shape=[jax.ShapeDtypeStruct((t, d), BF16), jax.ShapeDtypeStruct((LEFT + t, d), F32),
                   jax.ShapeDtypeStruct((LEFT + t, d), F32), jax.ShapeDtypeStruct((n_h, TQ, BAND_W), F32),
                   jax.ShapeDtypeStruct((1, hd), F32)],
        scratch_shapes=[pltpu.VMEM((TQ, BAND_W), F32)],
        compiler_params=_params(("arbitrary", "arbitrary")),
    )(*ins)


def _head_norm_bwd(name, dy, off_rows, yraw, gain_row):
    t, d = yraw.shape
    hd = gain_row.shape[-1]
    tm = _tile(t, 512, 8)
    off = off_rows // tm

    def body(dy_ref, y_ref, g_ref, dx_ref, dg_ref):
        @pl.when((pl.program_id(0) == 0) & (pl.program_id(1) == 0))
        def _():
            dg_ref[...] = jnp.zeros_like(dg_ref)

        gain = g_ref[...]
        yv = y_ref[...].astype(F32)
        dyv = dy_ref[...]
        r = lax.rsqrt(jnp.mean(yv * yv, axis=-1, keepdims=True) + EPS)
        u = dyv * gain
        dx_ref[...] = (r * u - yv * (r * r * r * jnp.mean(u * yv, axis=-1, keepdims=True))).astype(BF16)
        dg_ref[...] += jnp.sum(dyv * yv * r, axis=0, keepdims=True)

    blk = pl.BlockSpec((tm, hd), lambda i, h: (i, h))
    gspec = pl.BlockSpec((1, hd), lambda i, h: (0, 0))
    return pl.pallas_call(
        body, name=name, grid=(t // tm, d // hd),
        in_specs=[pl.BlockSpec((tm, hd), lambda i, h: (i + off, h)), blk, gspec],
        out_specs=[blk, gspec],
        out_shape=[jax.ShapeDtypeStruct((t, d), BF16), jax.ShapeDtypeStruct((1, hd), F32)],
        compiler_params=_params(("arbitrary", "arbitrary")),
    )(dy, yraw, gain_row)


def _head_norm_epilogue(hd):
    def epi(acc, gain):
        parts = [_head_rms(acc[:, c:c + hd], gain[:, c:c + hd])[0] for c in range(0, acc.shape[-1], hd)]
        return acc, jnp.concatenate(parts, axis=1)
    return epi


def _loss_head(name, y, target):
    t, d = y.shape
    tm = _tile(t, 512, 8)

    def body(y_ref, t_ref, dy_ref, dyb_ref, l_ref):
        @pl.when(pl.program_id(0) == 0)
        def _():
            l_ref[...] = jnp.zeros_like(l_ref)

        err = y_ref[...] - t_ref[...]
        dy = err * (1.0 / d)
        dy_ref[...] = dy
        dyb_ref[...] = dy.astype(BF16)
        per_tok = jnp.mean(err * err, axis=-1, keepdims=True)
        l_ref[...] += 0.5 * jnp.sum(per_tok, axis=0, keepdims=True)

    row = pl.BlockSpec((tm, d), lambda i: (i, 0))
    return pl.pallas_call(
        body, name=name, grid=(t // tm,), in_specs=[row, row],
        out_specs=[row, row, pl.BlockSpec((1, LANES), lambda i: (0, 0))],
        out_shape=[jax.ShapeDtypeStruct((t, d), F32), jax.ShapeDtypeStruct((t, d), BF16),
                   jax.ShapeDtypeStruct((1, LANES), F32)],
        compiler_params=_params(("arbitrary",)),
    )(y, target)


def _forward_backward(x, target, w, small):
    t, d = x.shape
    n_layers = small["n1"].shape[0]
    n_a = w["pool"].shape[0]
    n_b = w["wq"].shape[0]
    hd = small["kg"].shape[0]
    n_rel = small["rel"].shape[-1]
    row = lambda v: v.reshape(1, -1)
    kg_row = row(small["kg"])

    saved = []
    kv_saved = None
    kpad = vpad = None
    big = dict(tm=1024, tn=1024)
    for l in range(n_layers):
        x, s1 = _ffn_fwd(f"l{l}_ffn1", x, row(small["n1"][l]), w["g1"], w["u1"], w["d1"], l)
        if l < n_a:
            x_in = x
            x, diff, ypre = _pool_fwd(f"l{l}_pool", x, row(small["nm"][l]), w["pool"], row(small["pscale"][l]), l)
            sm = (x_in, diff, ypre)
        else:
            b = l - n_a
            x_in = x
            h, ht = _rms_fwd(f"l{l}_mix_norm", x, row(small["nm"][l]))
            (qr,) = _mm(f"l{l}_q", [(_op(h), _op(w["wq"], b))], t, d, d, "nn", [(t, BF16, 0)],
                        lambda acc: (acc,), tk=2048, **big)
            brow = _bias_rows(f"l{l}_bias", small["rel"][b])
            o, ot = _attn_fwd(f"l{l}_attn", qr, row(small["qg"][b]), kpad, vpad, brow)
            (x,) = _mm(f"l{l}_o", [(_op(o), _op(w["wo"], b))], t, d, d, "nn", [(t, F32, 0)],
                       lambda acc, xv: (xv + acc,), extras=[(x_in, None, "mn")], tk=2048, **big)
            sm = (x_in, ht, qr, brow, o, ot)
        x, s2 = _ffn_fwd(f"l{l}_ffn2", x, row(small["n2"][l]), w["g2"], w["u2"], w["d2"], l)
        saved.append((s1, sm, s2))
        if l == n_a - 1:
            hk, hkt = _rms_fwd("kv_norm", x, row(small["kvn"]))
            pad_blocks = LEFT // _tile(t, 512, 8)
            kr, kpad = _mm("kv_k", [(_op(hk), _op(w["wk"]))], t, d, d, "nn",
                           [(t, BF16, 0), (LEFT + t, BF16, pad_blocks)],
                           _head_norm_epilogue(hd), extras=[(jnp.tile(kg_row, (1, d // hd)), None, "row")],
                           tn=1024, tk=2048)
            (vpad,) = _mm("kv_v", [(_op(hk), _op(w["wv"]))], t, d, d, "nn", [(LEFT + t, BF16, pad_blocks)],
                          lambda acc: (acc,), tn=1024, tk=2048)
            kv_saved = (x, hkt, kr)

    dx, dxb, loss = _loss_head("loss_head", x, target)

    gw = {k: [None] * n_layers for k in ("g1", "u1", "d1", "g2", "u2", "d2")}
    gw.update(pool=[None] * n_a, wq=[None] * n_b, wo=[None] * n_b)
    gs = {k: [None] * n_layers for k in ("n1", "nm", "n2")}
    gs.update(qg=[None] * n_b, rel=[None] * n_b, pscale=[None] * n_a)
    dk = dv = None
    ident = lambda acc: (acc,)
    for l in reversed(range(n_layers)):
        s1, sm, s2 = saved[l]
        if l == n_a - 1:
            x_kv, hkt, kr = kv_saved
            dkr, gs["kg"] = _head_norm_bwd("kv_bwd_knorm", dk, LEFT, kr, kg_row)
            off_k = LEFT // _tile(t, 512)
            (gw["wk"],) = _mm("kv_bwd_wk", [(_op(hkt), _op(dkr))], d, d, t, "nn", [(d, F32, 0)], ident, tk=512, **big)
            (gw["wv"],) = _mm("kv_bwd_wv", [(_op(hkt), _op(dv, None, off_k))], d, d, t, "nn", [(d, F32, 0)], ident,
                              tk=512, **big)
            off_m = LEFT // _tile(t, 512, 8)
            (dhk,) = _mm("kv_bwd_h", [(_op(dkr), _op(w["wk"])), (_op(dv, None, off_m), _op(w["wv"]))], t, d, d, "nt",
                         [(t, F32, 0)], ident, tn=1024, tk=2048)
            dx, dxb, gs["kvn"] = _rms_bwd("kv_bwd_norm", dhk, x_kv, row(small["kvn"]), dx)
        dx, dxb, gs["n2"][l], gw["g2"][l], gw["u2"][l], gw["d2"][l] = _ffn_bwd(
            f"l{l}_ffn2", dx, dxb, s2, row(small["n2"][l]), w["g2"], w["u2"], w["d2"], l)
        if l < n_a:
            x_in, diff, ypre = sm
            ddiff, gw["pool"][l], gs["pscale"][l] = _pool_bwd_mat(
                f"l{l}_pool_bwd_mat", dx, diff, ypre, w["pool"], row(small["pscale"][l]), l)
            dx, dxb, gs["nm"][l] = _pool_bwd_win(f"l{l}_pool_bwd_win", ddiff, x_in, row(small["nm"][l]), dx)
        else:
            b = l - n_a
            x_in, ht, qr, brow, o, ot = sm
            (gw["wo"][b],) = _mm(f"l{l}_bwd_wo", [(_op(ot), _op(dxb))], d, d, t, "nn", [(d, F32, 0)], ident,
                                 tk=512, **big)
            (do,) = _mm(f"l{l}_bwd_o", [(_op(dxb), _op(w["wo"], b))], t, d, d, "nt", [(t, BF16, 0)], ident,
                        tk=2048, **big)
            dqr, dk, dv, dbias, gs["qg"][b] = _attn_bwd(f"l{l}_attn_bwd", qr, row(small["qg"][b]), kpad, vpad, brow,
                                                         o, do, dk, dv)
            gs["rel"][b] = _bias_rows_bwd(f"l{l}_bias_bwd", dbias, n_rel).reshape(-1, n_rel)
            (gw["wq"][b],) = _mm(f"l{l}_bwd_wq", [(_op(ht), _op(dqr))], d, d, t, "nn", [(d, F32, 0)], ident,
                                 tk=512, **big)
            (dh,) = _mm(f"l{l}_bwd_qh", [(_op(dqr), _op(w["wq"], b))], t, d, d, "nt", [(t, F32, 0)], ident,
                        tk=2048, **big)
            dx, dxb, gs["nm"][l] = _rms_bwd(f"l{l}_mix_bwd_norm", dh, x_in, row(small["nm"][l]), dx)
        dx, dxb, gs["n1"][l], gw["g1"][l], gw["u1"][l], gw["d1"][l] = _ffn_bwd(
            f"l{l}_ffn1", dx, dxb, s1, row(small["n1"][l]), w["g1"], w["u1"], w["d1"], l)
    return loss, dx, gw, gs


def _cast_pad(name, wsh, rows_to, cols_to, dtype=BF16):
    n_l, r, c = wsh.shape
    if cols_to != c:
        tr, tc_in, tc_out = _tile(r, 256, 16), c, cols_to
    else:
        tr, tc_in, tc_out = r, _tile(c, 256), _tile(c, 256)
    r_out = rows_to if tr == r else tr

    def body(w_ref, o_ref):
        o_ref[...] = jnp.zeros_like(o_ref)
        o_ref[: w_ref.shape[0], : w_ref.shape[1]] = w_ref[...].astype(dtype)

    return pl.pallas_call(
        body, name=name, grid=(n_l, r // tr, c // tc_in),
        in_specs=[pl.BlockSpec((None, tr, tc_in), lambda l, i, j: (l, i, j))],
        out_specs=pl.BlockSpec((None, r_out, tc_out), lambda l, i, j: (l, i, j)),
        out_shape=jax.ShapeDtypeStruct((n_l, rows_to, cols_to), dtype),
        compiler_params=_params(("parallel", "parallel", "parallel")),
    )(wsh)


def _place():
    x, y, c = lax.axis_index("x"), lax.axis_index("y"), lax.axis_index("c")
    return x, y, c, 2 * x + y


def _exchange(name, srcs, out_shapes, plan):
    n_src, n_out, n_it = len(srcs), len(out_shapes), len(plan)

    def body(*refs):
        src_refs = refs[:n_src]
        out_refs = refs[n_src: n_src + n_out]
        send_sems, recv_sems, loc_sems = refs[n_src + n_out:]
        x, y, c, me = _place()
        sib = (x, y, 1 - c)
        chips = [(1 - x, y), (x, 1 - y), (1 - x, 1 - y)]

        def remote(src, dst, it, k, dev):
            return pltpu.make_async_remote_copy(src_ref=src, dst_ref=dst, send_sem=send_sems.at[it, k],
                                                recv_sem=recv_sems.at[it, k], device_id=dev, device_id_type=MESH)

        started, local = [], []
        for it, (si, oi, sp, dp) in enumerate(plan):
            src, out = src_refs[si], out_refs[oi]
            mine = sp(src, me, c)
            lc = pltpu.make_async_copy(mine, dp(out, c, me), loc_sems.at[it])
            lc.start()
            local.append(lc)
            first = [remote(sp(src, 2 * px + py, c), dp(out, c, me), it, k, (px, py, c))
                     for k, (px, py) in enumerate(chips)]
            first.append(remote(mine, dp(out, c, me), it, 3, sib))
            for cp in first:
                cp.start()
            started += first
        for it, (si, oi, sp, dp) in enumerate(plan):
            src, out = src_refs[si], out_refs[oi]
            for k, (px, py) in enumerate(chips):
                landed = dp(out, c, 2 * px + py)
                remote(landed, landed, it, k, (px, py, c)).wait_recv()
                fwd = remote(landed, landed, it, 4 + k, sib)
                fwd.start()
                started.append(fwd)
        for it, (si, oi, sp, dp) in enumerate(plan):
            out = out_refs[oi]
            theirs = dp(out, 1 - c, me)
            remote(theirs, theirs, it, 3, sib).wait_recv()
            for k, (px, py) in enumerate(chips):
                theirs = dp(out, 1 - c, 2 * px + py)
                remote(theirs, theirs, it, 4 + k, sib).wait_recv()
        for cp in started:
            cp.wait_send()
        for lc in local:
            lc.wait()

    return pl.pallas_call(
        body, name=name, in_specs=[ANY] * n_src, out_specs=[ANY] * n_out,
        out_shape=[jax.ShapeDtypeStruct(s, dt) for s, dt in out_shapes],
        scratch_shapes=[pltpu.SemaphoreType.DMA((n_it, 7)), pltpu.SemaphoreType.DMA((n_it, 7)),
                        pltpu.SemaphoreType.DMA((n_it,))],
    )(*srcs)


def _gather_weights(name, shards, fams):
    out_shapes, plan = [], []
    for n, (s, fam) in enumerate(zip(shards, fams)):
        n_l, r, cc = s.shape
        if fam == "col":
            out_shapes.append(((n_l, r, N_CHIPS * cc), s.dtype))
            sp = lambda ref, chip, c, r=r: ref.at[:, pl.ds(c * (r // 2), r // 2), :]
            dp = lambda ref, half, chip, r=r, cc=cc: ref.at[:, pl.ds(half * (r // 2), r // 2), pl.ds(chip * cc, cc)]
        else:
            out_shapes.append(((n_l, N_CHIPS * r, cc), s.dtype))
            sp = lambda ref, chip, c, cc=cc: ref.at[:, :, pl.ds(c * (cc // 2), cc // 2)]
            dp = lambda ref, half, chip, r=r, cc=cc: ref.at[:, pl.ds(chip * r, r), pl.ds(half * (cc // 2), cc // 2)]
        plan.append((n, n, sp, dp))
    return _exchange(name, shards, out_shapes, plan)


def _swap_halves(name, grads, fams):
    n = len(grads)
    out_shapes = []
    for g, fam in zip(grads, fams):
        r, cc = g.shape
        out_shapes.append((r // 2, cc) if fam == "col" else (r, cc // 2))

    def body(*refs):
        g_refs, x_refs = refs[:n], refs[n: 2 * n]
        send_sems, recv_sems = refs[2 * n:]
        x, y, c, _ = _place()
        copies = []
        for it, fam in enumerate(fams):
            r, cc = g_refs[it].shape
            if fam == "col":
                src = g_refs[it].at[pl.ds((1 - c) * (r // 2), r // 2), :]
            else:
                src = g_refs[it].at[:, pl.ds((1 - c) * (cc // 2), cc // 2)]
            cp = pltpu.make_async_remote_copy(src_ref=src, dst_ref=x_refs[it], send_sem=send_sems.at[it],
                                              recv_sem=recv_sems.at[it], device_id=(x, y, 1 - c), device_id_type=MESH)
            cp.start()
            copies.append(cp)
        for cp in copies:
            cp.wait()

    return pl.pallas_call(
        body, name=name, in_specs=[ANY] * n, out_specs=[ANY] * n,
        out_shape=[jax.ShapeDtypeStruct(s, F32) for s in out_shapes],
        scratch_shapes=[pltpu.SemaphoreType.DMA((n,)), pltpu.SemaphoreType.DMA((n,))],
    )(*grads)


def _half_sum(name, g, xr, fam, core):
    r, cc = xr.shape
    tr, tc = _tile(r, 512, 16), _tile(cc, 1024)
    nr, nc = r // tr, cc // tc

    def body(c_ref, g_ref, x_ref, s_ref):
        s_ref[...] = (g_ref[...] + x_ref[...]).astype(BF16)

    if fam == "col":
        gmap = lambda i, j, c_ref: (c_ref[0] * nr + i, j)
    else:
        gmap = lambda i, j, c_ref: (i, c_ref[0] * nc + j)
    return pl.pallas_call(
        body, name=name,
        grid_spec=pltpu.PrefetchScalarGridSpec(
            num_scalar_prefetch=1, grid=(nr, nc),
            in_specs=[pl.BlockSpec((tr, tc), gmap), pl.BlockSpec((tr, tc), lambda i, j, c_ref: (i, j))],
            out_specs=pl.BlockSpec((tr, tc), lambda i, j, c_ref: (i, j))),
        out_shape=jax.ShapeDtypeStruct((r, cc), BF16), compiler_params=_params(("parallel", "parallel")),
    )(core, g, xr)


def _scatter_sums(name, sums, fams, layers, groups):
    out_index, out_shapes, plan = {}, [], []
    n_layers = {}
    for grp, l in zip(groups, layers):
        n_layers[grp] = max(n_layers.get(grp, 0), l + 1)
    for n, (s, fam, l, grp) in enumerate(zip(sums, fams, layers, groups)):
        r, cc = s.shape
        piece = (r, cc // N_CHIPS) if fam == "col" else (r // N_CHIPS, cc)
        if grp not in out_index:
            out_index[grp] = len(out_shapes)
            out_shapes.append(((n_layers[grp], 2, N_CHIPS) + piece, BF16))
        if fam == "col":
            sp = lambda ref, chip, c, w=piece[1]: ref.at[:, pl.ds(chip * w, w)]
        else:
            sp = lambda ref, chip, c, h=piece[0]: ref.at[pl.ds(chip * h, h), :]
        dp = lambda ref, half, chip, l=l: ref.at[l, half, chip]
        plan.append((n, out_index[grp], sp, dp))
    outs = _exchange(name, sums, out_shapes, plan)
    return {grp: outs[i] for grp, i in out_index.items()}


def _allreduce_small(name, packed):
    rows = packed.shape[0]

    def body(p_ref, o_ref, slots, send_sems, recv_sems):
        x, y, c, _ = _place()
        me = 4 * x + 2 * y + c
        slots[me] = p_ref[...]
        copies = []
        for k in range(1, 8):
            peer = (x ^ (k >> 2), y ^ ((k >> 1) & 1), c ^ (k & 1))
            cp = pltpu.make_async_remote_copy(src_ref=p_ref, dst_ref=slots.at[me], send_sem=send_sems.at[k - 1],
                                              recv_sem=recv_sems.at[k - 1], device_id=peer, device_id_type=MESH)
            cp.start()
            copies.append(cp)
        for cp in copies:
            cp.wait()
        acc = slots[0]
        for k in range(1, 8):
            acc = acc + slots[k]
        o_ref[...] = acc

    return pl.pallas_call(
        body, name=name, out_shape=jax.ShapeDtypeStruct((rows, LANES), F32),
        in_specs=[pl.BlockSpec(memory_space=pltpu.VMEM)], out_specs=pl.BlockSpec(memory_space=pltpu.VMEM),
        scratch_shapes=[pltpu.VMEM((8, rows, LANES), F32), pltpu.SemaphoreType.DMA((7,)), pltpu.SemaphoreType.DMA((7,))],
    )(packed)


def _adamw_math(w, g, m, v):
    m = ADAM_B1 * m + (1.0 - ADAM_B1) * g
    v = ADAM_B2 * v + (1.0 - ADAM_B2) * (g * g)
    m_hat = m / (1.0 - ADAM_B1 ** ADAM_STEP)
    v_hat = v / (1.0 - ADAM_B2 ** ADAM_STEP)
    delta = -ADAM_LR * (m_hat / (jnp.sqrt(v_hat) + ADAM_EPS) + ADAM_WD * w)
    return delta, m, v


def _adamw_sharded(name, rbuf, w, m, v, fam):
    n_l, r, cc = w.shape
    if fam == "col":
        hr, cp = rbuf.shape[-2:]
        tr = _tile(hr, 128, 16)
        n_t = hr // tr
        rspec = pl.BlockSpec((None, None, N_CHIPS, tr, cp), lambda l, h, i: (l, h, 0, i, 0))
        wspec = pl.BlockSpec((None, tr, cc), lambda l, h, i: (l, h * n_t + i, 0))
    else:
        rp, hc = rbuf.shape[-2:]
        tc = _tile(hc, 256)
        n_t = hc // tc
        rspec = pl.BlockSpec((None, None, N_CHIPS, rp, tc), lambda l, h, i: (l, h, 0, 0, i))
        wspec = pl.BlockSpec((None, r, tc), lambda l, h, i: (l, 0, h * n_t + i))

    def body(r_ref, w_ref, m_ref, v_ref, g_out, d_out, m_out, v_out):
        rows, cols = w_ref.shape
        g = r_ref[0, :rows, :cols].astype(F32)
        for k in range(1, N_CHIPS):
            g = g + r_ref[k, :rows, :cols].astype(F32)
        delta, m_new, v_new = _adamw_math(w_ref[...], g, m_ref[...], v_ref[...])
        g_out[...] = g
        d_out[...] = delta
        m_out[...] = m_new
        v_out[...] = v_new

    return pl.pallas_call(
        body, name=name, grid=(n_l, 2, n_t), in_specs=[rspec, wspec, wspec, wspec], out_specs=[wspec] * 4,
        out_shape=[jax.ShapeDtypeStruct(w.shape, F32)] * 4,
        compiler_params=_params(("parallel", "parallel", "parallel")),
    )(rbuf, w, m, v)


def _adamw_small(name, w, g, m, v):
    def body(w_ref, g_ref, m_ref, v_ref, d_out, m_out, v_out):
        delta, m_new, v_new = _adamw_math(w_ref[...], g_ref[...], m_ref[...], v_ref[...])
        d_out[...] = delta
        m_out[...] = m_new
        v_out[...] = v_new

    return pl.pallas_call(body, name=name, out_shape=[jax.ShapeDtypeStruct(w.shape, F32)] * 3)(w, g, m, v)


def _pack(arrays):
    flat = jnp.concatenate([a.reshape(-1).astype(F32) for a in arrays])
    n = flat.shape[0]
    rows = _round_up(_round_up(n, LANES) // LANES, 8)
    return jnp.pad(flat, (0, rows * LANES - n)).reshape(rows, LANES)


def _unpack(packed, like):
    flat = packed.reshape(-1)
    out, pos = [], 0
    for a in like:
        out.append(flat[pos: pos + a.size].reshape(a.shape))
        pos += a.size
    return out


_SHARDED = ("ffn1_w_gate", "ffn1_w_up", "ffn1_w_down", "ffn2_w_gate", "ffn2_w_up", "ffn2_w_down", "pool_w",
            "w_k", "w_v", "w_q", "w_o")
_FAMILY = dict(ffn1_w_gate="col", ffn1_w_up="col", ffn2_w_gate="col", ffn2_w_up="col", ffn1_w_down="row",
               ffn2_w_down="row", pool_w="row", w_k="row", w_v="row", w_q="row", w_o="row")
_SHORT = dict(ffn1_w_gate="g1", ffn1_w_up="u1", ffn1_w_down="d1", ffn2_w_gate="g2", ffn2_w_up="u2", ffn2_w_down="d2",
              pool_w="pool", w_k="wk", w_v="wv", w_q="wq", w_o="wo")
_SMALL = ("ffn1_norm", "mix_norm", "ffn2_norm", "kv_norm", "k_gain", "q_gain", "rel_bias", "pool_scale")
_WEIGHTS = ("ffn1_norm", "ffn1_w_gate", "ffn1_w_up", "ffn1_w_down", "mix_norm", "ffn2_norm", "ffn2_w_gate",
            "ffn2_w_up", "ffn2_w_down", "pool_w", "pool_scale", "kv_norm", "w_k", "w_v", "k_gain", "w_q", "q_gain",
            "rel_bias", "w_o")


def _as3d(name, a):
    if name == "pool_w":
        return a.reshape(a.shape[0], a.shape[1] * a.shape[2], a.shape[3])
    if a.ndim == 2:
        return a[None]
    return a


def kernel(x, ffn1_norm, ffn1_w_gate, ffn1_w_up, ffn1_w_down, mix_norm, ffn2_norm, ffn2_w_gate, ffn2_w_up, ffn2_w_down, pool_w, pool_scale, kv_norm, w_k, w_v, k_gain, w_q, q_gain, rel_bias, w_o, loss_target, m_ffn1_norm, m_ffn1_w_gate, m_ffn1_w_up, m_ffn1_w_down, m_mix_norm, m_ffn2_norm, m_ffn2_w_gate, m_ffn2_w_up, m_ffn2_w_down, m_pool_w, m_pool_scale, m_kv_norm, m_w_k, m_w_v, m_k_gain, m_w_q, m_q_gain, m_rel_bias, m_w_o, v_ffn1_norm, v_ffn1_w_gate, v_ffn1_w_up, v_ffn1_w_down, v_mix_norm, v_ffn2_norm, v_ffn2_w_gate, v_ffn2_w_up, v_ffn2_w_down, v_pool_w, v_pool_scale, v_kv_norm, v_w_k, v_w_v, v_k_gain, v_w_q, v_q_gain, v_rel_bias, v_w_o):
    wts = dict(ffn1_norm=ffn1_norm, ffn1_w_gate=ffn1_w_gate, ffn1_w_up=ffn1_w_up, ffn1_w_down=ffn1_w_down,
               mix_norm=mix_norm, ffn2_norm=ffn2_norm, ffn2_w_gate=ffn2_w_gate, ffn2_w_up=ffn2_w_up,
               ffn2_w_down=ffn2_w_down, pool_w=pool_w, pool_scale=pool_scale, kv_norm=kv_norm, w_k=w_k, w_v=w_v,
               k_gain=k_gain, w_q=w_q, q_gain=q_gain, rel_bias=rel_bias, w_o=w_o)
    mom = dict(ffn1_norm=m_ffn1_norm, ffn1_w_gate=m_ffn1_w_gate, ffn1_w_up=m_ffn1_w_up, ffn1_w_down=m_ffn1_w_down,
               mix_norm=m_mix_norm, ffn2_norm=m_ffn2_norm, ffn2_w_gate=m_ffn2_w_gate, ffn2_w_up=m_ffn2_w_up,
               ffn2_w_down=m_ffn2_w_down, pool_w=m_pool_w, pool_scale=m_pool_scale, kv_norm=m_kv_norm, w_k=m_w_k,
               w_v=m_w_v, k_gain=m_k_gain, w_q=m_w_q, q_gain=m_q_gain, rel_bias=m_rel_bias, w_o=m_w_o)
    var = dict(ffn1_norm=v_ffn1_norm, ffn1_w_gate=v_ffn1_w_gate, ffn1_w_up=v_ffn1_w_up, ffn1_w_down=v_ffn1_w_down,
               mix_norm=v_mix_norm, ffn2_norm=v_ffn2_norm, ffn2_w_gate=v_ffn2_w_gate, ffn2_w_up=v_ffn2_w_up,
               ffn2_w_down=v_ffn2_w_down, pool_w=v_pool_w, pool_scale=v_pool_scale, kv_norm=v_kv_norm, w_k=v_w_k,
               w_v=v_w_v, k_gain=v_k_gain, w_q=v_w_q, q_gain=v_q_gain, rel_bias=v_rel_bias, w_o=v_w_o)
    t, d = x.shape[1], x.shape[2]
    chip = 2 * lax.axis_index("x") + lax.axis_index("y")
    core = lax.axis_index("c").astype(jnp.int32).reshape(1)
    n_groups = len(POOL_WINDOWS)
    p = d // n_groups

    shards, fams = [], []
    for name in _SHARDED:
        a = _as3d(name, wts[name])
        fam = _FAMILY[name]
        rows_to = a.shape[1] if fam == "col" else _round_up(a.shape[1], LANES)
        cols_to = _round_up(a.shape[2], LANES) if fam == "col" else a.shape[2]
        shards.append(_cast_pad(f"cast_{name}", a, rows_to, cols_to))
        fams.append(fam)
    shards.append(_cast_pad("cast_pool_scale", pool_scale[None], 16, pool_scale.shape[1], F32))
    fams.append("col")
    full = _gather_weights("gather_weights", shards, fams)
    w = {_SHORT[name]: f for name, f in zip(_SHARDED, full)}
    n_a = pool_w.shape[0]
    w["pool"] = (w["pool"].reshape(n_a, N_CHIPS, n_groups, p // N_CHIPS, p).transpose(0, 2, 1, 3, 4)
                 .reshape(n_a, n_groups, p, p))
    w["wk"], w["wv"] = w["wk"][0], w["wv"][0]
    small = dict(n1=ffn1_norm, nm=mix_norm, n2=ffn2_norm, kvn=kv_norm, kg=k_gain, qg=q_gain, rel=rel_bias,
                 pscale=full[-1][0, : pool_scale.shape[0]])

    loss, dx, gw, gs = _forward_backward(x[0], loss_target[0], w, small)

    grads, gfam, glayer, ggroup = [], [], [], []
    for name in _SHARDED:
        g = gw[_SHORT[name]]
        for l, gl in enumerate(g if isinstance(g, list) else [g]):
            if name == "pool_w":
                gl = (gl.reshape(n_groups, N_CHIPS, p // N_CHIPS, p).transpose(1, 0, 2, 3).reshape(n_groups * p, p))
            grads.append(gl)
            gfam.append(_FAMILY[name])
            glayer.append(l)
            ggroup.append(name)
    theirs = _swap_halves("swap_halves", grads, gfam)
    sums = [_half_sum(f"half_sum_{grp}_{l}", g, xr, fam, core)
            for g, xr, fam, l, grp in zip(grads, theirs, gfam, glayer, ggroup)]
    rbufs = _scatter_sums("scatter_sums", sums, gfam, glayer, ggroup)

    small_grads = [jnp.concatenate(gs["n1"]), jnp.concatenate(gs["nm"]), jnp.concatenate(gs["n2"]), gs["kvn"], gs["kg"],
                   jnp.concatenate(gs["qg"]), jnp.stack(gs["rel"]), jnp.concatenate(gs["pscale"]), loss[:, :1]]
    red = _unpack(_allreduce_small("allreduce_small", _pack(small_grads)), small_grads)
    g_small = dict(ffn1_norm=red[0], mix_norm=red[1], ffn2_norm=red[2], kv_norm=red[3].reshape(-1),
                   k_gain=red[4].reshape(-1), q_gain=red[5], rel_bias=red[6],
                   pool_scale=lax.dynamic_slice_in_dim(red[7], chip * pool_scale.shape[1], pool_scale.shape[1], 1))
    loss_out = red[8].reshape(())

    grad, delta, new_m, new_v = {}, {}, {}, {}
    for name in _SHARDED:
        outs = _adamw_sharded(f"adamw_{name}", rbufs[name], _as3d(name, wts[name]), _as3d(name, mom[name]),
                              _as3d(name, var[name]), _FAMILY[name])
        grad[name], delta[name], new_m[name], new_v[name] = [o.reshape(wts[name].shape) for o in outs]
    sw = [wts[n] for n in _SMALL]
    sg = [g_small[n].reshape(wts[n].shape) for n in _SMALL]
    d_s, m_s, v_s = _adamw_small("adamw_small", _pack(sw), _pack(sg), _pack([mom[n] for n in _SMALL]),
                                 _pack([var[n] for n in _SMALL]))
    for n, g_n, d_n, m_n, v_n in zip(_SMALL, sg, _unpack(d_s, sw), _unpack(m_s, sw), _unpack(v_s, sw)):
        grad[n], delta[n], new_m[n], new_v[n] = g_n, d_n, m_n, v_n
    return (loss_out, dx[None], *[grad[n] for n in _WEIGHTS], *[delta[n] for n in _WEIGHTS],
            *[new_m[n] for n in _WEIGHTS], *[new_v[n] for n in _WEIGHTS])
```

```python
import functools

import jax
import jax.numpy as jnp
from jax import lax
from jax.experimental import pallas as pl
from jax.experimental.pallas import tpu as pltpu

F32 = jnp.float32
BF16 = jnp.bfloat16

EPS = 1e-6
CHUNK = 64
LEFT = 512
LEFT_CHUNKS = LEFT // CHUNK
REL_MAX = 128
POOL_WINDOWS = (2, 4, 8, 16)
HALO = 16
NEG_INF = -1e30
TQ = 256
BAND_W = TQ + LEFT
N_CHIPS = 4
LANES = 128
VMEM_LIMIT = 56 * 1024 * 1024

ADAM_LR = 0.001
ADAM_B1 = 0.9
ADAM_B2 = 0.999
ADAM_EPS = 1e-08
ADAM_WD = 0.01
ADAM_STEP = 10

MESH = pl.DeviceIdType.MESH
ANY = pl.BlockSpec(memory_space=pl.ANY)


def _round_up(n, m):
    return (n + m - 1) // m * m


def _tile(n, pref, unit=LANES):
    if n <= pref:
        return n
    t = pref // unit * unit
    while t >= unit:
        if n % t == 0:
            return t
        t -= unit
    return n


def _params(sem):
    return pltpu.CompilerParams(dimension_semantics=sem, vmem_limit_bytes=VMEM_LIMIT)


def _bs(block, imap, lead=None):
    if lead is None:
        return pl.BlockSpec(tuple(block), imap)
    return pl.BlockSpec((None,) + tuple(block), lambda *g: (lead,) + tuple(imap(*g)))


_DIMS = {
    "nn": (((1,), (0,)), ((), ())),
    "nt": (((1,), (1,)), ((), ())),
    "tn": (((0,), (0,)), ((), ())),
}


def _job_io(jobs):
    operands = [s for jb in jobs for s in jb.srcs]
    out_shape = [jax.ShapeDtypeStruct(s, dt) for jb in jobs for s, dt in jb.out_shapes]
    scratch = [s for jb in jobs for s in jb.sems]
    return operands, out_shape, scratch


def _job_phases(jobs, src_refs, out_refs, sem_refs):
    phases, a, b, c = [], 0, 0, 0
    for jb in jobs:
        ns, no, nm = len(jb.srcs), len(jb.out_shapes), len(jb.sems)
        phases.append(jb.phases(src_refs[a:a + ns], out_refs[b:b + no], sem_refs[c:c + nm]))
        a, b, c = a + ns, b + no, c + nm
    return phases


def _run_phases(phases, step, total):
    for n, at in enumerate((0, total // 2, total - 1)):
        @pl.when(step == at)
        def _():
            for ph in phases:
                ph[n]()


def _mm(name, pairs, m, n, k, mode, outs, epilogue, extras=(), tm=512, tn=512, tk=512, jobs=()):
    tm, tn, tk = _tile(m, tm, LANES if mode == "tn" else 8), _tile(n, tn), _tile(k, tk)
    ni, nj, nk = m // tm, n // tn, k // tk
    n_pairs, n_extra, n_out = len(pairs), len(extras), len(outs)
    in_specs, operands = [], []
    for (a, la, oa), (b, lb, ob) in pairs:
        if mode == "tn":
            in_specs.append(_bs((tk, tm), lambda i, j, kk, o=oa: (kk + o, i), la))
        else:
            in_specs.append(_bs((tm, tk), lambda i, j, kk, o=oa: (i + o, kk), la))
        if mode == "nt":
            in_specs.append(_bs((tn, tk), lambda i, j, kk, o=ob: (j + o, kk), lb))
        else:
            in_specs.append(_bs((tk, tn), lambda i, j, kk, o=ob: (kk + o, j), lb))
        operands += [a, b]
    for e, le, kind in extras:
        if kind == "mn":
            in_specs.append(_bs((tm, tn), lambda i, j, kk: (i, j), le))
        else:
            in_specs.append(_bs((1, tn), lambda i, j, kk: (0, j), le))
        operands.append(e)
    out_specs, out_shape, flipped = [], [], []
    for o in outs:
        rows, dt, off = o[:3]
        flipped.append(len(o) > 3 and o[3])
        if flipped[-1]:
            out_specs.append(pl.BlockSpec((tn, tm), lambda i, j, kk: (j, i)))
            out_shape.append(jax.ShapeDtypeStruct((n, rows), dt))
        else:
            out_specs.append(pl.BlockSpec((tm, tn), lambda i, j, kk, o=off: (i + o, j)))
            out_shape.append(jax.ShapeDtypeStruct((rows, n), dt))
    dims = _DIMS[mode]
    job_ops, job_out, job_sems = _job_io(jobs)
    n_in = len(operands)

    def body(*refs):
        ab = refs[: 2 * n_pairs]
        ex = refs[2 * n_pairs: n_in]
        job_src = refs[n_in: n_in + len(job_ops)]
        out_refs = refs[n_in + len(job_ops): n_in + len(job_ops) + n_out]
        job_dst = refs[n_in + len(job_ops) + n_out: n_in + len(job_ops) + n_out + len(job_out)]
        acc = refs[n_in + len(job_ops) + n_out + len(job_out)]
        kk = pl.program_id(2)
        if jobs:
            step = (pl.program_id(0) * nj + pl.program_id(1)) * nk + kk
            _run_phases(_job_phases(jobs, job_src, job_dst, refs[len(refs) - len(job_sems):]), step, ni * nj * nk)

        @pl.when(kk == 0)
        def _():
            acc[...] = jnp.zeros_like(acc)

        for p in range(n_pairs):
            a = ab[2 * p][...].astype(BF16)
            b = ab[2 * p + 1][...].astype(BF16)
            acc[...] += lax.dot_general(a, b, dims, preferred_element_type=F32)

        @pl.when(kk == nk - 1)
        def _():
            res = epilogue(acc[...], *[e[...] for e in ex])
            for o_ref, o, flip in zip(out_refs, res, flipped):
                o_ref[...] = (o.T if flip else o).astype(o_ref.dtype)

    sem = ("arbitrary",) * 3 if jobs else ("parallel", "parallel", "arbitrary")
    res = pl.pallas_call(
        body, name=name, grid=(ni, nj, nk), in_specs=in_specs + [ANY] * len(job_ops),
        out_specs=out_specs + [ANY] * len(job_out), out_shape=out_shape + job_out,
        scratch_shapes=[pltpu.VMEM((tm, tn), F32)] + job_sems, compiler_params=_params(sem),
    )(*operands, *job_ops)
    return (res[:n_out], res[n_out:]) if jobs else res


def _op(a, lead=None, off=0):
    return (a, lead, off)


def _head_rms(blk, gain):
    r = lax.rsqrt(jnp.mean(blk * blk, axis=-1, keepdims=True) + EPS)
    return blk * r * gain, r


def _rms_fwd(name, x, gain_row):
    t, d = x.shape
    tm = _tile(t, 512)

    def body(x_ref, g_ref, h_ref, ht_ref):
        xv = x_ref[...]
        r = lax.rsqrt(jnp.mean(xv * xv, axis=-1, keepdims=True) + EPS)
        h = (xv * r * g_ref[...]).astype(BF16)
        h_ref[...] = h
        ht_ref[...] = h.T

    return pl.pallas_call(
        body, name=name, grid=(t // tm,),
        in_specs=[pl.BlockSpec((tm, d), lambda i: (i, 0)), pl.BlockSpec((1, d), lambda i: (0, 0))],
        out_specs=[pl.BlockSpec((tm, d), lambda i: (i, 0)), pl.BlockSpec((d, tm), lambda i: (0, i))],
        out_shape=[jax.ShapeDtypeStruct((t, d), BF16), jax.ShapeDtypeStruct((d, t), BF16)],
        compiler_params=_params(("parallel",)),
    )(x, gain_row)


def _rms_bwd_math(dh, xv, g):
    r = lax.rsqrt(jnp.mean(xv * xv, axis=-1, keepdims=True) + EPS)
    u = dh * g
    dx = r * u - xv * (r * r * r * jnp.mean(u * xv, axis=-1, keepdims=True))
    dg = jnp.sum(dh * xv * r, axis=0, keepdims=True)
    return dx, dg


def _rms_bwd(name, dh, x, gain_row, dres=None):
    t, d = x.shape
    tm = _tile(t, 256, 8)
    has_res = dres is not None

    def body(*refs):
        if has_res:
            dh_ref, x_ref, g_ref, dres_ref, dx_ref, dxb_ref, dg_ref = refs
        else:
            dh_ref, x_ref, g_ref, dx_ref, dxb_ref, dg_ref = refs
        dx, dg = _rms_bwd_math(dh_ref[...], x_ref[...], g_ref[...])
        if has_res:
            dx = dx + dres_ref[...]
        dx_ref[...] = dx
        dxb_ref[...] = dx.astype(BF16)

        @pl.when(pl.program_id(0) == 0)
        def _():
            dg_ref[...] = jnp.zeros_like(dg_ref)

        dg_ref[...] += dg

    row = pl.BlockSpec((tm, d), lambda i: (i, 0))
    vec = pl.BlockSpec((1, d), lambda i: (0, 0))
    return pl.pallas_call(
        body, name=name, grid=(t // tm,),
        in_specs=[row, row, vec] + ([row] if has_res else []),
        out_specs=[row, row, vec],
        out_shape=[jax.ShapeDtypeStruct((t, d), F32), jax.ShapeDtypeStruct((t, d), BF16),
                   jax.ShapeDtypeStruct((1, d), F32)],
        compiler_params=_params(("arbitrary",)),
    )(*([dh, x, gain_row] + ([dres] if has_res else [])))


def _ffn_up(name, h, wgt, wut, jobs=()):
    t, d = h.shape
    f = wgt.shape[0]
    tm, tn = _tile(t, 1024), _tile(f, 512)
    ni, nj = t // tm, f // tn
    job_ops, job_out, job_sems = _job_io(jobs)

    def body(*refs):
        h_ref, wg_ref, wu_ref = refs[:3]
        job_src = refs[3: 3 + len(job_ops)]
        g_ref, u_ref, a_ref, at_ref = refs[3 + len(job_ops): 7 + len(job_ops)]
        job_dst = refs[7 + len(job_ops): 7 + len(job_ops) + len(job_out)]
        if jobs:
            step = pl.program_id(0) * nj + pl.program_id(1)
            _run_phases(_job_phases(jobs, job_src, job_dst, refs[len(refs) - len(job_sems):]), step, ni * nj)
        hv = h_ref[...]
        g = lax.dot_general(hv, wg_ref[...], _DIMS["nt"], preferred_element_type=F32)
        u = lax.dot_general(hv, wu_ref[...], _DIMS["nt"], preferred_element_type=F32)
        g_ref[...] = g.astype(BF16)
        u_ref[...] = u.astype(BF16)
        a = (g * jax.nn.sigmoid(g) * u).astype(BF16)
        a_ref[...] = a
        at_ref[...] = a.T

    wspec = pl.BlockSpec((tn, d), lambda i, j: (j, 0))
    ospec = pl.BlockSpec((tm, tn), lambda i, j: (i, j))
    res = pl.pallas_call(
        body, name=name, grid=(ni, nj),
        in_specs=[pl.BlockSpec((tm, d), lambda i, j: (i, 0)), wspec, wspec] + [ANY] * len(job_ops),
        out_specs=[ospec] * 3 + [pl.BlockSpec((tn, tm), lambda i, j: (j, i))] + [ANY] * len(job_out),
        out_shape=[jax.ShapeDtypeStruct((t, f), BF16)] * 3 + [jax.ShapeDtypeStruct((f, t), BF16)] + job_out,
        scratch_shapes=job_sems,
        compiler_params=_params(("arbitrary", "arbitrary") if jobs else ("parallel", "parallel")),
    )(h, wgt, wut, *job_ops)
    return res[:4], res[4:]


def _ffn_bwd(tag, dout, doutb, saved, gain_row, wgt, wut, wd, jobs, swap_extra):
    x, ht, g, u, at = saved
    t, d = x.shape
    f = wd.shape[0]
    jobs = dict(jobs)
    got = {}

    def mm(key, *args, **kw):
        jb = jobs.get(key, ())
        res = _mm(f"{tag}_bwd_{key}", *args, jobs=jb, **kw)
        if jb:
            res, got[key] = res
        return res

    def act_bwd(acc, gv, uv):
        da = 0.5 * acc
        gv = gv.astype(F32)
        uv = uv.astype(F32)
        sig = jax.nn.sigmoid(gv)
        du = da * (gv * sig)
        dg = da * uv * (sig * (1.0 + gv * (1.0 - sig)))
        return dg, du

    ident = lambda acc: (acc,)
    dg, du = mm("act", [(_op(doutb), _op(wd))], t, f, d, "nt", [(t, BF16, 0), (t, BF16, 0)],
                act_bwd, extras=[(g, None, "mn"), (u, None, "mn")], tm=1024, tn=512, tk=2048)
    (dwd,) = mm("wd", [(_op(at), _op(doutb))], f, d, t, "nn", [(f, F32, 0)], lambda acc: (0.5 * acc,),
                tm=1408, tn=1024, tk=512)
    (dwgt,) = mm("wg", [(_op(ht), _op(dg))], d, f, t, "nn", [(d, F32, 0, True)], ident, tm=1024, tn=1408, tk=512)
    (dwut,) = mm("wu", [(_op(ht), _op(du))], d, f, t, "nn", [(d, F32, 0, True)], ident, tm=1024, tn=1408, tk=512)
    jobs["h"] = [_SwapJob(list(swap_extra) + [dwgt, dwut, dwd])]
    (dh,) = mm("h", [(_op(dg), _op(wgt)), (_op(du), _op(wut))], t, d, f, "nn", [(t, F32, 0)], ident,
               tm=1024, tn=1024, tk=1408)
    dx, dxb, dgain = _rms_bwd(f"{tag}_bwd_norm", dh, x, gain_row, dout)
    return dx, dxb, dgain, [dwgt, dwut, dwd], got


def _inv_count(t0, rows, window):
    tpos = t0 + lax.broadcasted_iota(jnp.int32, (rows, 1), 0)
    return 1.0 / jnp.minimum(tpos + 1, window).astype(F32)


def _pool_fwd(name, x, gain_row, wp, scale_row, layer):
    t, d = x.shape
    n_g = len(POOL_WINDOWS)
    p = d // n_g
    tm = _tile(t, 256, 8)
    hb = tm // HALO

    def body(x_ref, xp_ref, g_ref, wp_ref, s_ref, out_ref, diff_ref, ypre_ref):
        i = pl.program_id(0)
        gain = g_ref[...]

        def norm(v):
            return v * lax.rsqrt(jnp.mean(v * v, axis=-1, keepdims=True) + EPS) * gain

        xv = x_ref[...]
        h = norm(xv)
        hp = jnp.where(i > 0, norm(xp_ref[...]), 0.0)
        ext = jnp.concatenate([hp, h], axis=0)
        for gi, w in enumerate(POOL_WINDOWS):
            cols = slice(gi * p, (gi + 1) * p)
            s = ext[:, cols]
            k = 1
            while k < w:
                s = s + pltpu.roll(s, k, 0)
                k *= 2
            pooled = s[HALO:, :] * _inv_count(i * tm, tm, w)
            diff = (pooled - h[:, cols]).astype(BF16)
            y = jnp.dot(diff, wp_ref[gi], preferred_element_type=F32)
            diff_ref[:, cols] = diff
            ypre_ref[:, cols] = y.astype(BF16)
            out_ref[:, cols] = xv[:, cols] + y * s_ref[:, cols]

    row = pl.BlockSpec((tm, d), lambda i: (i, 0))
    vec = pl.BlockSpec((1, d), lambda i: (0, 0))
    return pl.pallas_call(
        body, name=name, grid=(t // tm,),
        in_specs=[row, pl.BlockSpec((HALO, d), lambda i: (jnp.maximum(i * hb - 1, 0), 0)), vec,
                  _bs((n_g, p, p), lambda i: (0, 0, 0), layer), vec],
        out_specs=[row, row, row],
        out_shape=[jax.ShapeDtypeStruct((t, d), F32), jax.ShapeDtypeStruct((t, d), BF16),
                   jax.ShapeDtypeStruct((t, d), BF16)],
        compiler_params=_params(("parallel",)),
    )(x, x, gain_row, wp, scale_row)


def _pool_bwd_mat(name, dout, diff, ypre, wp, scale_row, layer):
    t, d = dout.shape
    n_g = len(POOL_WINDOWS)
    p = d // n_g
    tm = _tile(t, 256, 8)

    def body(do_ref, diff_ref, ypre_ref, wp_ref, s_ref, dd_ref, dwp_ref, ds_ref):
        @pl.when(pl.program_id(0) == 0)
        def _():
            dwp_ref[...] = jnp.zeros_like(dwp_ref)
            ds_ref[...] = jnp.zeros_like(ds_ref)

        dov = do_ref[...]
        ds_ref[...] += jnp.sum(dov * ypre_ref[...].astype(F32), axis=0, keepdims=True)
        dy = (dov * s_ref[...]).astype(BF16)
        for gi in range(n_g):
            cols = slice(gi * p, (gi + 1) * p)
            dwp_ref[gi] += lax.dot_general(diff_ref[:, cols], dy[:, cols], _DIMS["tn"], preferred_element_type=F32)
            dd_ref[:, cols] = lax.dot_general(dy[:, cols], wp_ref[gi], _DIMS["nt"], preferred_element_type=F32)

    row = pl.BlockSpec((tm, d), lambda i: (i, 0))
    vec = pl.BlockSpec((1, d), lambda i: (0, 0))
    return pl.pallas_call(
        body, name=name, grid=(t // tm,),
        in_specs=[row, row, row, _bs((n_g, p, p), lambda i: (0, 0, 0), layer), vec],
        out_specs=[row, pl.BlockSpec((n_g, p, p), lambda i: (0, 0, 0)), vec],
        out_shape=[jax.ShapeDtypeStruct((t, d), F32), jax.ShapeDtypeStruct((n_g, p, p), F32),
                   jax.ShapeDtypeStruct((1, d), F32)],
        compiler_params=_params(("arbitrary",)),
    )(dout, diff, ypre, wp, scale_row)


def _pool_bwd_win(name, ddiff, x, gain_row, dout):
    t, d = x.shape
    n_g = len(POOL_WINDOWS)
    p = d // n_g
    tm = _tile(t, 256, 8)
    hb = tm // HALO
    n_t = t // tm
    rows = tm + HALO

    def body(e_ref, en_ref, x_ref, g_ref, do_ref, dx_ref, dxb_ref, dg_ref):
        i = pl.program_id(0)
        e = e_ref[...]
        en = jnp.where(i < n_t - 1, en_ref[...], 0.0)
        ext = jnp.concatenate([e, en], axis=0)
        parts = []
        for gi, w in enumerate(POOL_WINDOWS):
            cols = slice(gi * p, (gi + 1) * p)
            s = ext[:, cols] * _inv_count(i * tm, rows, w)
            k = 1
            while k < w:
                s = s + pltpu.roll(s, rows - k, 0)
                k *= 2
            parts.append(s[:tm, :] - e[:, cols])
        dh = jnp.concatenate(parts, axis=1)
        dx, dg = _rms_bwd_math(dh, x_ref[...], g_ref[...])
        dx = dx + do_ref[...]
        dx_ref[...] = dx
        dxb_ref[...] = dx.astype(BF16)

        @pl.when(i == 0)
        def _():
            dg_ref[...] = jnp.zeros_like(dg_ref)

        dg_ref[...] += dg

    row = pl.BlockSpec((tm, d), lambda i: (i, 0))
    vec = pl.BlockSpec((1, d), lambda i: (0, 0))
    return pl.pallas_call(
        body, name=name, grid=(n_t,),
        in_specs=[row, pl.BlockSpec((HALO, d), lambda i: (jnp.minimum((i + 1) * hb, t // HALO - 1), 0)), row, vec, row],
        out_specs=[row, row, vec],
        out_shape=[jax.ShapeDtypeStruct((t, d), F32), jax.ShapeDtypeStruct((t, d), BF16),
                   jax.ShapeDtypeStruct((1, d), F32)],
        compiler_params=_params(("arbitrary",)),
    )(ddiff, ddiff, x, gain_row, dout)


def _rel_index(n_rel):
    dd = lax.broadcasted_iota(jnp.int32, (n_rel, BAND_W), 1)
    e = lax.broadcasted_iota(jnp.int32, (n_rel, BAND_W), 0)
    idx = jnp.where(dd < LEFT + 2 * CHUNK, jnp.clip(LEFT - dd, -(CHUNK - 1), REL_MAX) + (CHUNK - 1), n_rel - 1)
    return (idx == e).astype(F32)


def _bias_rows(name, table):
    n_h, n_rel = table.shape

    def body(t_ref, o_ref):
        o_ref[...] = jnp.dot(t_ref[...], _rel_index(n_rel), preferred_element_type=F32,
                             precision=lax.Precision.HIGHEST)

    rows = pl.pallas_call(body, name=name, out_shape=jax.ShapeDtypeStruct((n_h, BAND_W), F32))(table)
    return rows.reshape(n_h, 1, BAND_W)


def _bias_rows_bwd(name, dbias, n_rel):
    n_h = dbias.shape[0]

    def body(db_ref, o_ref):
        rr = lax.broadcasted_iota(jnp.int32, (TQ, TQ), 0)
        cc = lax.broadcasted_iota(jnp.int32, (TQ, TQ), 1)
        flipped = jnp.dot((rr + cc == TQ - 1).astype(F32), db_ref[...], preferred_element_type=F32,
                          precision=lax.Precision.HIGHEST)
        unskew = pltpu.roll(flipped, BAND_W - TQ + 1, 1, stride=1, stride_axis=0)
        drow = jnp.sum(unskew, axis=0, keepdims=True)
        o_ref[...] = lax.dot_general(drow, _rel_index(n_rel), _DIMS["nt"], preferred_element_type=F32,
                                     precision=lax.Precision.HIGHEST)

    return pl.pallas_call(
        body, name=name, grid=(n_h,),
        in_specs=[pl.BlockSpec((None, TQ, BAND_W), lambda h: (h, 0, 0))],
        out_specs=pl.BlockSpec((None, 1, n_rel), lambda h: (h, 0, 0)),
        out_shape=jax.ShapeDtypeStruct((n_h, 1, n_rel), F32), compiler_params=_params(("parallel",)),
    )(dbias)


def _band_bias(row):
    tile = pltpu.roll(jnp.broadcast_to(row, (TQ, BAND_W)), 0, 1, stride=1, stride_axis=0)
    qc = lax.broadcasted_iota(jnp.int32, (TQ, BAND_W), 0) // CHUNK
    kc = lax.broadcasted_iota(jnp.int32, (TQ, BAND_W), 1) // CHUNK
    return jnp.where((kc >= qc) & (kc <= qc + LEFT_CHUNKS), tile, NEG_INF)


def _band_probs(qv, kb, bias, t0):
    s = lax.dot_general(qv, kb, _DIMS["nt"], preferred_element_type=F32) * (qv.shape[-1] ** -0.5) + bias
    col = lax.broadcasted_iota(jnp.int32, (TQ, BAND_W), 1)
    s = jnp.where(col >= LEFT - t0, s, NEG_INF)
    e = jnp.exp(s - jnp.max(s, axis=-1, keepdims=True))
    return e / jnp.sum(e, axis=-1, keepdims=True)


def _valid_rows(blk, t0):
    r = lax.broadcasted_iota(jnp.int32, (BAND_W, 1), 0)
    return jnp.where(r >= LEFT - t0, blk, jnp.zeros_like(blk))


def _attn_fwd(name, qr, gain_row, kpad, vpad, brow):
    t, d = qr.shape
    hd = gain_row.shape[-1]
    n_h = d // hd

    def body(q_ref, g_ref, k_ref, v_ref, b_ref, o_ref, ot_ref, bias_ref):
        i = pl.program_id(1)

        @pl.when(i == 0)
        def _():
            bias_ref[...] = _band_bias(b_ref[...])

        t0 = pl.multiple_of(i * TQ, TQ)
        qv = _head_rms(q_ref[...].astype(F32), g_ref[...])[0].astype(BF16)
        kb = _valid_rows(k_ref[pl.ds(t0, BAND_W), :], t0)
        vb = _valid_rows(v_ref[pl.ds(t0, BAND_W), :], t0)
        pr = _band_probs(qv, kb, bias_ref[...], t0)
        o = jnp.dot(pr.astype(BF16), vb, preferred_element_type=F32).astype(BF16)
        o_ref[...] = o
        ot_ref[...] = o.T

    kv_spec = pl.BlockSpec((LEFT + t, hd), lambda h, i: (0, h))
    return pl.pallas_call(
        body, name=name, grid=(n_h, t // TQ),
        in_specs=[pl.BlockSpec((TQ, hd), lambda h, i: (i, h)), pl.BlockSpec((1, hd), lambda h, i: (0, 0)),
                  kv_spec, kv_spec, pl.BlockSpec((None, 1, BAND_W), lambda h, i: (h, 0, 0))],
        out_specs=[pl.BlockSpec((TQ, hd), lambda h, i: (i, h)), pl.BlockSpec((hd, TQ), lambda h, i: (h, i))],
        out_shape=[jax.ShapeDtypeStruct((t, d), BF16), jax.ShapeDtypeStruct((d, t), BF16)],
        scratch_shapes=[pltpu.VMEM((TQ, BAND_W), F32)],
        compiler_params=_params(("parallel", "arbitrary")),
    )(qr, gain_row, kpad, vpad, brow)


def _attn_bwd(name, qr, gain_row, kpad, vpad, brow, o, do, dk_in=None, dv_in=None):
    t, d = qr.shape
    hd = gain_row.shape[-1]
    n_h = d // hd
    scale = hd ** -0.5
    has_init = dk_in is not None

    def body(*refs):
        if has_init:
            (q_ref, g_ref, k_ref, v_ref, b_ref, o_ref, do_ref, dki_ref, dvi_ref,
             dq_ref, dk_ref, dv_ref, db_ref, dg_ref, bias_ref) = refs
        else:
            (q_ref, g_ref, k_ref, v_ref, b_ref, o_ref, do_ref,
             dq_ref, dk_ref, dv_ref, db_ref, dg_ref, bias_ref) = refs
        h = pl.program_id(0)
        i = pl.program_id(1)

        @pl.when(i == 0)
        def _():
            bias_ref[...] = _band_bias(b_ref[...])
            db_ref[...] = jnp.zeros_like(db_ref)
            if has_init:
                dk_ref[...] = dki_ref[...]
                dv_ref[...] = dvi_ref[...]
            else:
                dk_ref[...] = jnp.zeros_like(dk_ref)
                dv_ref[...] = jnp.zeros_like(dv_ref)

        @pl.when((i == 0) & (h == 0))
        def _():
            dg_ref[...] = jnp.zeros_like(dg_ref)

        t0 = pl.multiple_of(i * TQ, TQ)
        gain = g_ref[...]
        qraw = q_ref[...].astype(F32)
        qn, r = _head_rms(qraw, gain)
        qv = qn.astype(BF16)
        kb = _valid_rows(k_ref[pl.ds(t0, BAND_W), :], t0)
        vb = _valid_rows(v_ref[pl.ds(t0, BAND_W), :], t0)
        pr = _band_probs(qv, kb, bias_ref[...], t0)
        dov = do_ref[...]
        delta = jnp.sum(dov.astype(F32) * o_ref[...].astype(F32), axis=-1, keepdims=True)
        dp = lax.dot_general(dov, vb, _DIMS["nt"], preferred_element_type=F32)
        ds = pr * (dp - delta)
        db_ref[...] += ds
        dsb = ds.astype(BF16)
        dq = jnp.dot(dsb, kb, preferred_element_type=F32) * scale
        dk_ref[pl.ds(t0, BAND_W), :] += lax.dot_general(dsb, qv, _DIMS["tn"], preferred_element_type=F32) * scale
        dv_ref[pl.ds(t0, BAND_W), :] += lax.dot_general(pr.astype(BF16), dov, _DIMS["tn"], preferred_element_type=F32)
        u = dq * gain
        dq_ref[...] = (r * u - qraw * (r * r * r * jnp.mean(u * qraw, axis=-1, keepdims=True))).astype(BF16)
        dg_ref[...] += jnp.sum(dq * qraw * r, axis=0, keepdims=True)

    qspec = pl.BlockSpec((TQ, hd), lambda h, i: (i, h))
    kv_spec = pl.BlockSpec((LEFT + t, hd), lambda h, i: (0, h))
    gspec = pl.BlockSpec((1, hd), lambda h, i: (0, 0))
    ins = [qr, gain_row, kpad, vpad, brow, o, do] + ([dk_in, dv_in] if has_init else [])
    return pl.pallas_call(
        body, name=name, grid=(n_h, t // TQ),
        in_specs=[qspec, gspec, kv_spec, kv_spec, pl.BlockSpec((None, 1, BAND_W), lambda h, i: (h, 0, 0)), qspec, qspec]
        + ([kv_spec, kv_spec] if has_init else []),
        out_specs=[qspec, kv_spec, kv_spec, pl.BlockSpec((None, TQ, BAND_W), lambda h, i: (h, 0, 0)), gspec],
        out_shape=[jax.ShapeDtypeStruct((t, d), BF16), jax.ShapeDtypeStruct((LEFT + t, d), F32),
                   jax.ShapeDtypeStruct((LEFT + t, d), F32), jax.ShapeDtypeStruct((n_h, TQ, BAND_W), F32),
                   jax.ShapeDtypeStruct((1, hd), F32)],
        scratch_shapes=[pltpu.VMEM((TQ, BAND_W), F32)],
        compiler_params=_params(("arbitrary", "arbitrary")),
    )(*ins)


def _head_norm_bwd(name, dy, off_rows, yraw, gain_row):
    t, d = yraw.shape
    hd = gain_row.shape[-1]
    tm = _tile(t, 512, 8)
    off = off_rows // tm

    def body(dy_ref, y_ref, g_ref, dx_ref, dg_ref):
        @pl.when((pl.program_id(0) == 0) & (pl.program_id(1) == 0))
        def _():
            dg_ref[...] = jnp.zeros_like(dg_ref)

        gain = g_ref[...]
        yv = y_ref[...].astype(F32)
        dyv = dy_ref[...]
        r = lax.rsqrt(jnp.mean(yv * yv, axis=-1, keepdims=True) + EPS)
        u = dyv * gain
        dx_ref[...] = (r * u - yv * (r * r * r * jnp.mean(u * yv, axis=-1, keepdims=True))).astype(BF16)
        dg_ref[...] += jnp.sum(dyv * yv * r, axis=0, keepdims=True)

    blk = pl.BlockSpec((tm, hd), lambda i, h: (i, h))
    gspec = pl.BlockSpec((1, hd), lambda i, h: (0, 0))
    return pl.pallas_call(
        body, name=name, grid=(t // tm, d // hd),
        in_specs=[pl.BlockSpec((tm, hd), lambda i, h: (i + off, h)), blk, gspec],
        out_specs=[blk, gspec],
        out_shape=[jax.ShapeDtypeStruct((t, d), BF16), jax.ShapeDtypeStruct((1, hd), F32)],
        compiler_params=_params(("arbitrary", "arbitrary")),
    )(dy, yraw, gain_row)


def _head_norm_epilogue(hd):
    def epi(acc, gain):
        parts = [_head_rms(acc[:, c:c + hd], gain[:, c:c + hd])[0] for c in range(0, acc.shape[-1], hd)]
        return acc, jnp.concatenate(parts, axis=1)
    return epi


def _loss_head(name, y, target):
    t, d = y.shape
    tm = _tile(t, 512, 8)

    def body(y_ref, t_ref, dy_ref, dyb_ref, l_ref):
        @pl.when(pl.program_id(0) == 0)
        def _():
            l_ref[...] = jnp.zeros_like(l_ref)

        err = y_ref[...] - t_ref[...]
        dy = err * (1.0 / d)
        dy_ref[...] = dy
        dyb_ref[...] = dy.astype(BF16)
        per_tok = jnp.mean(err * err, axis=-1, keepdims=True)
        l_ref[...] += 0.5 * jnp.sum(per_tok, axis=0, keepdims=True)

    row = pl.BlockSpec((tm, d), lambda i: (i, 0))
    return pl.pallas_call(
        body, name=name, grid=(t // tm,), in_specs=[row, row],
        out_specs=[row, row, pl.BlockSpec((1, LANES), lambda i: (0, 0))],
        out_shape=[jax.ShapeDtypeStruct((t, d), F32), jax.ShapeDtypeStruct((t, d), BF16),
                   jax.ShapeDtypeStruct((1, LANES), F32)],
        compiler_params=_params(("arbitrary",)),
    )(y, target)


def _forward_backward(x, target, sh, small, core):
    t, d = x.shape
    n_layers = small["n1"].shape[0]
    n_a = sh["pool"].shape[0]
    n_b = sh["wq"].shape[0]
    hd = small["kg"].shape[0]
    n_rel = small["rel"].shape[-1]
    n_g = len(POOL_WINDOWS)
    p = d // n_g
    row = lambda v: v.reshape(1, -1)
    kg_row = row(small["kg"])
    big = dict(tm=1024, tn=1024)
    ident = lambda acc: (acc,)

    stages = [(l, s) for l in range(n_layers) for s in (1, 2)]
    ffn_keys = lambda k, names: [(f"{nm}{stages[k][1]}", stages[k][0]) for nm in names]
    up_keys = {k: ffn_keys(k + 1, "gu") for k in range(len(stages) - 1)}
    down_keys = {k: ffn_keys(k + 1, "d") for k in range(len(stages) - 1)}
    up_keys[0] = ffn_keys(0, "d") + up_keys[0]
    up_keys[2 * n_a - 2] = up_keys[2 * n_a - 2] + [("wk", 0), ("wv", 0)]
    for b in range(n_b):
        up_keys[2 * (n_a + b) - 1] = up_keys[2 * (n_a + b) - 1] + [("wq", b), ("wo", b)]
    first_keys = ffn_keys(0, "gu") + [("pool", l) for l in range(n_a)] + [("pscale", 0)]
    full = {}

    def gather(keys):
        return [_gather_job([(sh[nm], l) for nm, l in keys])] if keys else []

    full.update(zip(first_keys, _run_jobs("gather_first", gather(first_keys))))
    pscale = full[("pscale", 0)].reshape(N_CHIPS, 8, p)[:, :n_a].transpose(1, 0, 2).reshape(n_a, d)
    pool_full = lambda l: (full[("pool", l)].reshape(N_CHIPS, n_g, p // N_CHIPS, p).transpose(1, 0, 2, 3)
                           .reshape(n_g, p, p))

    saved, mixed = [], []
    kv_saved = None
    kpad = vpad = None
    for k, (l, s) in enumerate(stages):
        if s == 2:
            x_in = x
            if l < n_a:
                wp = pool_full(l)
                x, diff, ypre = _pool_fwd(f"l{l}_pool", x, row(small["nm"][l]), wp, row(pscale[l]), None)
                mixed.append((x_in, diff, ypre, wp))
            else:
                b = l - n_a
                h, ht = _rms_fwd(f"l{l}_mix_norm", x, row(small["nm"][l]))
                (qr,) = _mm(f"l{l}_q", [(_op(h), _op(full[("wq", b)]))], t, d, d, "nn", [(t, BF16, 0)], ident,
                            tk=2048, **big)
                brow = _bias_rows(f"l{l}_bias", small["rel"][b])
                o, ot = _attn_fwd(f"l{l}_attn", qr, row(small["qg"][b]), kpad, vpad, brow)
                (x,) = _mm(f"l{l}_o", [(_op(o), _op(full[("wo", b)]))], t, d, d, "nn", [(t, F32, 0)],
                           lambda acc, xv: (xv + acc,), extras=[(x_in, None, "mn")], tk=2048, **big)
                mixed.append((x_in, ht, qr, brow, o, ot))
        tag = f"l{l}_ffn{s}"
        h, ht = _rms_fwd(f"{tag}_norm", x, row(small[f"n{s}"][l]))
        (g, u, a, at), got = _ffn_up(f"{tag}_up", h, full[(f"g{s}", l)], full[(f"u{s}", l)], gather(up_keys.get(k, [])))
        full.update(zip(up_keys.get(k, []), got))
        jobs = gather(down_keys.get(k, []))
        res = _mm(f"{tag}_down", [(_op(a), _op(full[(f"d{s}", l)]))], t, d, a.shape[1], "nn", [(t, F32, 0)],
                  lambda acc, xv: (xv + 0.5 * acc,), extras=[(x, None, "mn")], tm=1024, tn=1024, tk=1408, jobs=jobs)
        (x_new,), got = res if jobs else (res, [])
        full.update(zip(down_keys.get(k, []), got))
        saved.append((x, ht, g, u, at))
        x = x_new
        if k == 2 * n_a - 1:
            hk, hkt = _rms_fwd("kv_norm", x, row(small["kvn"]))
            pad_blocks = LEFT // _tile(t, 512, 8)
            kr, kpad = _mm("kv_k", [(_op(hk), _op(full[("wk", 0)]))], t, d, d, "nn",
                           [(t, BF16, 0), (LEFT + t, BF16, pad_blocks)],
                           _head_norm_epilogue(hd), extras=[(jnp.tile(kg_row, (1, d // hd)), None, "row")],
                           tn=1024, tk=2048)
            (vpad,) = _mm("kv_v", [(_op(hk), _op(full[("wv", 0)]))], t, d, d, "nn", [(LEFT + t, BF16, pad_blocks)],
                          ident, tn=1024, tk=2048)
            kv_saved = (x, hkt, kr)

    dx, dxb, loss = _loss_head("loss_head", x, target)

    gs = {nm: [None] * n_layers for nm in ("n1", "nm", "n2")}
    gs.update(qg=[None] * n_b, rel=[None] * n_b, pscale=[None] * n_a)
    grads, sums, rbufs = [], [], {}
    dk = dv = None
    for k in reversed(range(len(stages))):
        l, s = stages[k]
        if k == 2 * n_a - 1:
            x_kv, hkt, kr = kv_saved
            dkr, gs["kg"] = _head_norm_bwd("kv_bwd_knorm", dk, LEFT, kr, kg_row)
            off_k = LEFT // _tile(t, 512)
            (gwk,) = _mm("kv_bwd_wk", [(_op(hkt), _op(dkr))], d, d, t, "nn", [(d, F32, 0)], ident, tk=512, **big)
            (gwv,) = _mm("kv_bwd_wv", [(_op(hkt), _op(dv, None, off_k))], d, d, t, "nn", [(d, F32, 0)], ident,
                         tk=512, **big)
            grads += [(("wk", 0), gwk), (("wv", 0), gwv)]
            off_m = LEFT // _tile(t, 512, 8)
            (dhk,) = _mm("kv_bwd_h", [(_op(dkr), _op(full[("wk", 0)])), (_op(dv, None, off_m), _op(full[("wv", 0)]))],
                         t, d, d, "nt", [(t, F32, 0)], ident, tn=1024, tk=2048)
            dx, dxb, gs["kvn"] = _rms_bwd("kv_bwd_norm", dhk, x_kv, row(small["kvn"]), dx)
        tag = f"l{l}_ffn{s}"
        groups = [sums[n::3] for n in range(3)]
        jobs = {key: [_scatter_job([sm for _, sm in grp])] for key, grp in zip(("act", "wd", "wg"), groups) if grp}
        dx, dxb, gs[f"n{s}"][l], ffn_grads, got = _ffn_bwd(
            tag, dx, dxb, saved[k], row(small[f"n{s}"][l]), full[(f"g{s}", l)], full[(f"u{s}", l)],
            full[(f"d{s}", l)], jobs, [g for _, g in grads])
        for key, grp in zip(("act", "wd", "wg"), groups):
            rbufs.update(zip([kk for kk, _ in grp], got.get(key, [])))
        grads += list(zip(ffn_keys(k, "gud"), ffn_grads))
        sums = [(key, _half_sum(f"half_sum_{key[0]}_{key[1]}", g, xr, core))
                for (key, g), xr in zip(grads, got["h"])]
        grads = []
        if s == 2:
            if l < n_a:
                x_in, diff, ypre, wp = mixed[l]
                ddiff, gpool, gs["pscale"][l] = _pool_bwd_mat(f"l{l}_pool_bwd_mat", dx, diff, ypre, wp,
                                                              row(pscale[l]), None)
                grads.append((("pool", l), gpool.reshape(n_g, N_CHIPS, p // N_CHIPS, p).transpose(1, 0, 2, 3)
                              .reshape(n_g * p, p)))
                dx, dxb, gs["nm"][l] = _pool_bwd_win(f"l{l}_pool_bwd_win", ddiff, x_in, row(small["nm"][l]), dx)
            else:
                b = l - n_a
                x_in, ht, qr, brow, o, ot = mixed[l]
                (gwo,) = _mm(f"l{l}_bwd_wo", [(_op(ot), _op(dxb))], d, d, t, "nn", [(d, F32, 0)], ident, tk=512, **big)
                (do,) = _mm(f"l{l}_bwd_o", [(_op(dxb), _op(full[("wo", b)]))], t, d, d, "nt", [(t, BF16, 0)], ident,
                            tk=2048, **big)
                dqr, dk, dv, dbias, gs["qg"][b] = _attn_bwd(f"l{l}_attn_bwd", qr, row(small["qg"][b]), kpad, vpad,
                                                             brow, o, do, dk, dv)
                gs["rel"][b] = _bias_rows_bwd(f"l{l}_bias_bwd", dbias, n_rel).reshape(-1, n_rel)
                (gwq,) = _mm(f"l{l}_bwd_wq", [(_op(ht), _op(dqr))], d, d, t, "nn", [(d, F32, 0)], ident, tk=512, **big)
                grads += [(("wo", b), gwo), (("wq", b), gwq)]
                (dh,) = _mm(f"l{l}_bwd_qh", [(_op(dqr), _op(full[("wq", b)]))], t, d, d, "nt", [(t, F32, 0)], ident,
                            tk=2048, **big)
                dx, dxb, gs["nm"][l] = _rms_bwd(f"l{l}_mix_bwd_norm", dh, x_in, row(small["nm"][l]), dx)
    assert not grads
    rbufs.update(zip([key for key, _ in sums], _run_jobs("scatter_last", [_scatter_job([sm for _, sm in sums])])))
    return loss, dx, rbufs, gs


def _cast_pad(name, wsh, rows_to, cols_to, dtype=BF16):
    n_l, r, c = wsh.shape
    if cols_to != c:
        tr, tc_in, tc_out = _tile(r, 256, 16), c, cols_to
    else:
        tr, tc_in, tc_out = r, _tile(c, 256), _tile(c, 256)
    r_out = rows_to if tr == r else tr

    def body(w_ref, o_ref):
        o_ref[...] = jnp.zeros_like(o_ref)
        o_ref[: w_ref.shape[0], : w_ref.shape[1]] = w_ref[...].astype(dtype)

    return pl.pallas_call(
        body, name=name, grid=(n_l, r // tr, c // tc_in),
        in_specs=[pl.BlockSpec((None, tr, tc_in), lambda l, i, j: (l, i, j))],
        out_specs=pl.BlockSpec((None, r_out, tc_out), lambda l, i, j: (l, i, j)),
        out_shape=jax.ShapeDtypeStruct((n_l, rows_to, cols_to), dtype),
        compiler_params=_params(("parallel", "parallel", "parallel")),
    )(wsh)


def _place():
    x, y, c = lax.axis_index("x"), lax.axis_index("y"), lax.axis_index("c")
    return x, y, c, 2 * x + y


def _piece(ref, chip, half, rows, cols):
    return ref.at[pl.ds(chip * rows, rows), pl.ds(half * (cols // 2), cols // 2)]


class _ExchangeJob:
    def __init__(self, srcs, out_shapes, plan):
        self.srcs, self.out_shapes, self.plan = list(srcs), list(out_shapes), list(plan)
        n_it = len(plan)
        self.sems = [pltpu.SemaphoreType.DMA((n_it, 7)), pltpu.SemaphoreType.DMA((n_it, 7)),
                     pltpu.SemaphoreType.DMA((n_it,))]

    def phases(self, src_refs, out_refs, sem_refs):
        send_sems, recv_sems, loc_sems = sem_refs
        plan = self.plan

        def ctx():
            x, y, c, me = _place()
            return c, me, (x, y, 1 - c), [(1 - x, y), (x, 1 - y), (1 - x, 1 - y)]

        def remote(src, dst, it, k, dev):
            return pltpu.make_async_remote_copy(src_ref=src, dst_ref=dst, send_sem=send_sems.at[it, k],
                                                recv_sem=recv_sems.at[it, k], device_id=dev, device_id_type=MESH)

        def first(it):
            c, me, sib, chips = ctx()
            si, oi, sp, dp = plan[it]
            src, out = src_refs[si], out_refs[oi]
            mine = sp(src, me, c)
            local = pltpu.make_async_copy(mine, dp(out, c, me), loc_sems.at[it])
            sends = [remote(sp(src, 2 * px + py, c), dp(out, c, me), it, k, (px, py, c))
                     for k, (px, py) in enumerate(chips)]
            sends.append(remote(mine, dp(out, c, me), it, 3, sib))
            return local, sends

        def passed_on(it, k):
            c, me, sib, chips = ctx()
            px, py = chips[k]
            landed = dp_of(it)(out_refs[plan[it][1]], c, 2 * px + py)
            return remote(landed, landed, it, k, (px, py, c)), remote(landed, landed, it, 4 + k, sib)

        def dp_of(it):
            return plan[it][3]

        def start():
            for it in range(len(plan)):
                local, sends = first(it)
                local.start()
                for cp in sends:
                    cp.start()

        def mid():
            for it in range(len(plan)):
                for k in range(3):
                    arrival, onward = passed_on(it, k)
                    arrival.wait_recv()
                    onward.start()

        def finish():
            c, me, sib, chips = ctx()
            for it in range(len(plan)):
                out = out_refs[plan[it][1]]
                theirs = dp_of(it)(out, 1 - c, me)
                remote(theirs, theirs, it, 3, sib).wait_recv()
                for k, (px, py) in enumerate(chips):
                    theirs = dp_of(it)(out, 1 - c, 2 * px + py)
                    remote(theirs, theirs, it, 4 + k, sib).wait_recv()
            for it in range(len(plan)):
                local, sends = first(it)
                for cp in sends:
                    cp.wait_send()
                for k in range(3):
                    passed_on(it, k)[1].wait_send()
                local.wait()

        return start, mid, finish


def _gather_job(items):
    srcs, out_shapes, plan = [], [], []
    for n, (sh, layer) in enumerate(items):
        _, r, cc = sh.shape
        srcs.append(sh)
        out_shapes.append(((N_CHIPS * r, cc), sh.dtype))
        sp = lambda ref, chip, c, layer=layer, cc=cc: ref.at[layer, :, pl.ds(c * (cc // 2), cc // 2)]
        dp = lambda ref, half, chip, r=r, cc=cc: _piece(ref, chip, half, r, cc)
        plan.append((n, n, sp, dp))
    return _ExchangeJob(srcs, out_shapes, plan)


def _scatter_job(sums):
    out_shapes, plan = [], []
    for n, s in enumerate(sums):
        r4, hc = s.shape
        r = r4 // N_CHIPS
        out_shapes.append(((2, N_CHIPS, r, hc), BF16))
        sp = lambda ref, chip, c, r=r: ref.at[pl.ds(chip * r, r), :]
        dp = lambda ref, half, chip: ref.at[half, chip]
        plan.append((n, n, sp, dp))
    return _ExchangeJob(sums, out_shapes, plan)


class _SwapJob:
    def __init__(self, grads):
        self.srcs = list(grads)
        self.out_shapes = [((g.shape[0], g.shape[1] // 2), F32) for g in grads]
        self.sems = [pltpu.SemaphoreType.DMA((len(grads),)), pltpu.SemaphoreType.DMA((len(grads),))]

    def phases(self, src_refs, out_refs, sem_refs):
        send_sems, recv_sems = sem_refs

        def copies():
            x, y, c, _ = _place()
            out = []
            for it, (g_ref, x_ref) in enumerate(zip(src_refs, out_refs)):
                hc = g_ref.shape[1] // 2
                out.append(pltpu.make_async_remote_copy(
                    src_ref=g_ref.at[:, pl.ds((1 - c) * hc, hc)], dst_ref=x_ref, send_sem=send_sems.at[it],
                    recv_sem=recv_sems.at[it], device_id=(x, y, 1 - c), device_id_type=MESH))
            return out

        def start():
            for cp in copies():
                cp.start()

        def finish():
            for cp in copies():
                cp.wait()

        return start, lambda: None, finish


def _run_jobs(name, jobs):
    job_ops, job_out, job_sems = _job_io(jobs)
    n_src, n_out = len(job_ops), len(job_out)

    def body(*refs):
        for ph in zip(*_job_phases(jobs, refs[:n_src], refs[n_src: n_src + n_out], refs[n_src + n_out:])):
            for fn in ph:
                fn()

    return pl.pallas_call(body, name=name, in_specs=[ANY] * n_src, out_specs=[ANY] * n_out, out_shape=job_out,
                          scratch_shapes=job_sems)(*job_ops)


def _half_sum(name, g, xr, core):
    r, cc = xr.shape
    tr, tc = _tile(r, 512, 16), _tile(cc, 1024)
    nc = cc // tc

    def body(c_ref, g_ref, x_ref, s_ref):
        s_ref[...] = (g_ref[...] + x_ref[...]).astype(BF16)

    return pl.pallas_call(
        body, name=name,
        grid_spec=pltpu.PrefetchScalarGridSpec(
            num_scalar_prefetch=1, grid=(r // tr, nc),
            in_specs=[pl.BlockSpec((tr, tc), lambda i, j, c_ref: (i, c_ref[0] * nc + j)),
                      pl.BlockSpec((tr, tc), lambda i, j, c_ref: (i, j))],
            out_specs=pl.BlockSpec((tr, tc), lambda i, j, c_ref: (i, j))),
        out_shape=jax.ShapeDtypeStruct((r, cc), BF16), compiler_params=_params(("parallel", "parallel")),
    )(core, g, xr)


def _allreduce_small(name, packed):
    rows = packed.shape[0]

    def body(p_ref, o_ref, slots, send_sems, recv_sems):
        x, y, c, _ = _place()
        me = 4 * x + 2 * y + c
        slots[me] = p_ref[...]
        copies = []
        for k in range(1, 8):
            peer = (x ^ (k >> 2), y ^ ((k >> 1) & 1), c ^ (k & 1))
            cp = pltpu.make_async_remote_copy(src_ref=p_ref, dst_ref=slots.at[me], send_sem=send_sems.at[k - 1],
                                              recv_sem=recv_sems.at[k - 1], device_id=peer, device_id_type=MESH)
            cp.start()
            copies.append(cp)
        for cp in copies:
            cp.wait()
        acc = slots[0]
        for k in range(1, 8):
            acc = acc + slots[k]
        o_ref[...] = acc

    return pl.pallas_call(
        body, name=name, out_shape=jax.ShapeDtypeStruct((rows, LANES), F32),
        in_specs=[pl.BlockSpec(memory_space=pltpu.VMEM)], out_specs=pl.BlockSpec(memory_space=pltpu.VMEM),
        scratch_shapes=[pltpu.VMEM((8, rows, LANES), F32), pltpu.SemaphoreType.DMA((7,)), pltpu.SemaphoreType.DMA((7,))],
    )(packed)


def _adamw_math(w, g, m, v):
    m = ADAM_B1 * m + (1.0 - ADAM_B1) * g
    v = ADAM_B2 * v + (1.0 - ADAM_B2) * (g * g)
    m_hat = m / (1.0 - ADAM_B1 ** ADAM_STEP)
    v_hat = v / (1.0 - ADAM_B2 ** ADAM_STEP)
    delta = -ADAM_LR * (m_hat / (jnp.sqrt(v_hat) + ADAM_EPS) + ADAM_WD * w)
    return delta, m, v


def _adamw_sharded(name, rbufs, w, m, v):
    n_l, r, cc = w.shape
    rp, hc = rbufs[0].shape[-2:]
    tc = _tile(hc, 256)
    n_t = hc // tc
    wspec = pl.BlockSpec((None, r, tc), lambda l, h, i: (l, 0, h * n_t + i))

    def rspec(q):
        def imap(l, h, i):
            before, after = l < q, l > q
            return (jnp.where(before, 0, jnp.where(after, 1, h)), 0, 0,
                    jnp.where(before, 0, jnp.where(after, n_t - 1, i)))
        return pl.BlockSpec((None, N_CHIPS, rp, tc), imap)

    def body(*refs):
        r_refs = refs[:n_l]
        w_ref, m_ref, v_ref, g_out, d_out, m_out, v_out = refs[n_l:]
        rows = w_ref.shape[0]
        layer = pl.program_id(0)
        for q in range(n_l):
            @pl.when(layer == q)
            def _():
                g = r_refs[q][0, :rows, :].astype(F32)
                for k in range(1, N_CHIPS):
                    g = g + r_refs[q][k, :rows, :].astype(F32)
                g_out[...] = g

        g = g_out[...]
        delta, m_new, v_new = _adamw_math(w_ref[...], g, m_ref[...], v_ref[...])
        d_out[...] = delta
        m_out[...] = m_new
        v_out[...] = v_new

    return pl.pallas_call(
        body, name=name, grid=(n_l, 2, n_t), in_specs=[rspec(q) for q in range(n_l)] + [wspec] * 3,
        out_specs=[wspec] * 4, out_shape=[jax.ShapeDtypeStruct(w.shape, F32)] * 4,
        compiler_params=_params(("arbitrary", "arbitrary", "arbitrary")),
    )(*rbufs, w, m, v)


def _adamw_small(name, w, g, m, v):
    def body(w_ref, g_ref, m_ref, v_ref, d_out, m_out, v_out):
        delta, m_new, v_new = _adamw_math(w_ref[...], g_ref[...], m_ref[...], v_ref[...])
        d_out[...] = delta
        m_out[...] = m_new
        v_out[...] = v_new

    return pl.pallas_call(body, name=name, out_shape=[jax.ShapeDtypeStruct(w.shape, F32)] * 3)(w, g, m, v)


def _pack(arrays):
    flat = jnp.concatenate([a.reshape(-1).astype(F32) for a in arrays])
    n = flat.shape[0]
    rows = _round_up(_round_up(n, LANES) // LANES, 8)
    return jnp.pad(flat, (0, rows * LANES - n)).reshape(rows, LANES)


def _unpack(packed, like):
    flat = packed.reshape(-1)
    out, pos = [], 0
    for a in like:
        out.append(flat[pos: pos + a.size].reshape(a.shape))
        pos += a.size
    return out


_SHARDED = ("ffn1_w_gate", "ffn1_w_up", "ffn1_w_down", "ffn2_w_gate", "ffn2_w_up", "ffn2_w_down", "pool_w",
            "w_k", "w_v", "w_q", "w_o")
_SHORT = dict(ffn1_w_gate="g1", ffn1_w_up="u1", ffn1_w_down="d1", ffn2_w_gate="g2", ffn2_w_up="u2", ffn2_w_down="d2",
              pool_w="pool", w_k="wk", w_v="wv", w_q="wq", w_o="wo")
_TRANSPOSED = ("ffn1_w_gate", "ffn1_w_up", "ffn2_w_gate", "ffn2_w_up")
_SMALL = ("ffn1_norm", "mix_norm", "ffn2_norm", "kv_norm", "k_gain", "q_gain", "rel_bias", "pool_scale")
_WEIGHTS = ("ffn1_norm", "ffn1_w_gate", "ffn1_w_up", "ffn1_w_down", "mix_norm", "ffn2_norm", "ffn2_w_gate",
            "ffn2_w_up", "ffn2_w_down", "pool_w", "pool_scale", "kv_norm", "w_k", "w_v", "k_gain", "w_q", "q_gain",
            "rel_bias", "w_o")


def _as3d(name, a):
    if name in _TRANSPOSED:
        return jnp.swapaxes(a, 1, 2)
    if name == "pool_w":
        return a.reshape(a.shape[0], a.shape[1] * a.shape[2], a.shape[3])
    if a.ndim == 2:
        return a[None]
    return a


def _from3d(name, a, shape):
    return jnp.swapaxes(a, 1, 2) if name in _TRANSPOSED else a.reshape(shape)


def kernel(x, ffn1_norm, ffn1_w_gate, ffn1_w_up, ffn1_w_down, mix_norm, ffn2_norm, ffn2_w_gate, ffn2_w_up, ffn2_w_down, pool_w, pool_scale, kv_norm, w_k, w_v, k_gain, w_q, q_gain, rel_bias, w_o, loss_target, m_ffn1_norm, m_ffn1_w_gate, m_ffn1_w_up, m_ffn1_w_down, m_mix_norm, m_ffn2_norm, m_ffn2_w_gate, m_ffn2_w_up, m_ffn2_w_down, m_pool_w, m_pool_scale, m_kv_norm, m_w_k, m_w_v, m_k_gain, m_w_q, m_q_gain, m_rel_bias, m_w_o, v_ffn1_norm, v_ffn1_w_gate, v_ffn1_w_up, v_ffn1_w_down, v_mix_norm, v_ffn2_norm, v_ffn2_w_gate, v_ffn2_w_up, v_ffn2_w_down, v_pool_w, v_pool_scale, v_kv_norm, v_w_k, v_w_v, v_k_gain, v_w_q, v_q_gain, v_rel_bias, v_w_o):
    wts = dict(ffn1_norm=ffn1_norm, ffn1_w_gate=ffn1_w_gate, ffn1_w_up=ffn1_w_up, ffn1_w_down=ffn1_w_down,
               mix_norm=mix_norm, ffn2_norm=ffn2_norm, ffn2_w_gate=ffn2_w_gate, ffn2_w_up=ffn2_w_up,
               ffn2_w_down=ffn2_w_down, pool_w=pool_w, pool_scale=pool_scale, kv_norm=kv_norm, w_k=w_k, w_v=w_v,
               k_gain=k_gain, w_q=w_q, q_gain=q_gain, rel_bias=rel_bias, w_o=w_o)
    mom = dict(ffn1_norm=m_ffn1_norm, ffn1_w_gate=m_ffn1_w_gate, ffn1_w_up=m_ffn1_w_up, ffn1_w_down=m_ffn1_w_down,
               mix_norm=m_mix_norm, ffn2_norm=m_ffn2_norm, ffn2_w_gate=m_ffn2_w_gate, ffn2_w_up=m_ffn2_w_up,
               ffn2_w_down=m_ffn2_w_down, pool_w=m_pool_w, pool_scale=m_pool_scale, kv_norm=m_kv_norm, w_k=m_w_k,
               w_v=m_w_v, k_gain=m_k_gain, w_q=m_w_q, q_gain=m_q_gain, rel_bias=m_rel_bias, w_o=m_w_o)
    var = dict(ffn1_norm=v_ffn1_norm, ffn1_w_gate=v_ffn1_w_gate, ffn1_w_up=v_ffn1_w_up, ffn1_w_down=v_ffn1_w_down,
               mix_norm=v_mix_norm, ffn2_norm=v_ffn2_norm, ffn2_w_gate=v_ffn2_w_gate, ffn2_w_up=v_ffn2_w_up,
               ffn2_w_down=v_ffn2_w_down, pool_w=v_pool_w, pool_scale=v_pool_scale, kv_norm=v_kv_norm, w_k=v_w_k,
               w_v=v_w_v, k_gain=v_k_gain, w_q=v_w_q, q_gain=v_q_gain, rel_bias=v_rel_bias, w_o=v_w_o)
    chip = 2 * lax.axis_index("x") + lax.axis_index("y")
    core = lax.axis_index("c").astype(jnp.int32).reshape(1)

    sh = {}
    for name in _SHARDED:
        a = _as3d(name, wts[name])
        sh[_SHORT[name]] = _cast_pad(f"cast_{name}", a, _round_up(a.shape[1], LANES), a.shape[2])
    sh["pscale"] = _cast_pad("cast_pool_scale", pool_scale[None], 8, pool_scale.shape[1], F32)
    small = dict(n1=ffn1_norm, nm=mix_norm, n2=ffn2_norm, kvn=kv_norm, kg=k_gain, qg=q_gain, rel=rel_bias)

    loss, dx, rbufs, gs = _forward_backward(x[0], loss_target[0], sh, small, core)

    small_grads = [jnp.concatenate(gs["n1"]), jnp.concatenate(gs["nm"]), jnp.concatenate(gs["n2"]), gs["kvn"], gs["kg"],
                   jnp.concatenate(gs["qg"]), jnp.stack(gs["rel"]), jnp.concatenate(gs["pscale"]), loss[:, :1]]
    red = _unpack(_allreduce_small("allreduce_small", _pack(small_grads)), small_grads)
    g_small = dict(ffn1_norm=red[0], mix_norm=red[1], ffn2_norm=red[2], kv_norm=red[3].reshape(-1),
                   k_gain=red[4].reshape(-1), q_gain=red[5], rel_bias=red[6],
                   pool_scale=lax.dynamic_slice_in_dim(red[7], chip * pool_scale.shape[1], pool_scale.shape[1], 1))
    loss_out = red[8].reshape(())

    grad, delta, new_m, new_v = {}, {}, {}, {}
    for name in _SHARDED:
        w3 = _as3d(name, wts[name])
        outs = _adamw_sharded(f"adamw_{name}", [rbufs[(_SHORT[name], l)] for l in range(w3.shape[0])], w3,
                              _as3d(name, mom[name]), _as3d(name, var[name]))
        grad[name], delta[name], new_m[name], new_v[name] = [_from3d(name, o, wts[name].shape) for o in outs]
    sw = [wts[n] for n in _SMALL]
    sg = [g_small[n].reshape(wts[n].shape) for n in _SMALL]
    d_s, m_s, v_s = _adamw_small("adamw_small", _pack(sw), _pack(sg), _pack([mom[n] for n in _SMALL]),
                                 _pack([var[n] for n in _SMALL]))
    for n, g_n, d_n, m_n, v_n in zip(_SMALL, sg, _unpack(d_s, sw), _unpack(m_s, sw), _unpack(v_s, sw)):
        grad[n], delta[n], new_m[n], new_v[n] = g_n, d_n, m_n, v_n
    return (loss_out, dx[None], *[grad[n] for n in _WEIGHTS], *[delta[n] for n in _WEIGHTS],
            *[new_m[n] for n in _WEIGHTS], *[new_v[n] for n in _WEIGHTS])
```

```python
import functools

import jax
import jax.numpy as jnp
from jax import lax
from jax.experimental import pallas as pl
from jax.experimental.pallas import tpu as pltpu

F32 = jnp.float32
BF16 = jnp.bfloat16

EPS = 1e-6
CHUNK = 64
LEFT = 512
LEFT_CHUNKS = LEFT // CHUNK
REL_MAX = 128
POOL_WINDOWS = (2, 4, 8, 16)
HALO = 16
NEG_INF = -1e30
TQ = 256
BAND_W = TQ + LEFT
N_CHIPS = 4
LANES = 128
VMEM_LIMIT = 56 * 1024 * 1024

ADAM_LR = 0.001
ADAM_B1 = 0.9
ADAM_B2 = 0.999
ADAM_EPS = 1e-08
ADAM_WD = 0.01
ADAM_STEP = 10

MESH = pl.DeviceIdType.MESH
ANY = pl.BlockSpec(memory_space=pl.ANY)


def _round_up(n, m):
    return (n + m - 1) // m * m


def _tile(n, pref, unit=LANES):
    if n <= pref:
        return n
    t = pref // unit * unit
    while t >= unit:
        if n % t == 0:
            return t
        t -= unit
    return n


def _params(sem):
    return pltpu.CompilerParams(dimension_semantics=sem, vmem_limit_bytes=VMEM_LIMIT)


def _bs(block, imap, lead=None):
    if lead is None:
        return pl.BlockSpec(tuple(block), imap)
    return pl.BlockSpec((None,) + tuple(block), lambda *g: (lead,) + tuple(imap(*g)))


_DIMS = {
    "nn": (((1,), (0,)), ((), ())),
    "nt": (((1,), (1,)), ((), ())),
    "tn": (((0,), (0,)), ((), ())),
}


def _job_io(jobs):
    operands = [s for jb in jobs for s in jb.srcs]
    out_shape = [jax.ShapeDtypeStruct(s, dt) for jb in jobs for s, dt in jb.out_shapes]
    scratch = [s for jb in jobs for s in jb.sems]
    return operands, out_shape, scratch


def _job_phases(jobs, src_refs, out_refs, sem_refs):
    phases, a, b, c = [], 0, 0, 0
    for jb in jobs:
        ns, no, nm = len(jb.srcs), len(jb.out_shapes), len(jb.sems)
        phases.append(jb.phases(src_refs[a:a + ns], out_refs[b:b + no], sem_refs[c:c + nm]))
        a, b, c = a + ns, b + no, c + nm
    return phases


def _run_phases(phases, step, total):
    for n, at in enumerate((0, max(total - 1 - max(1, total // 8), 0), total - 1)):
        @pl.when(step == at)
        def _():
            for ph in phases:
                ph[n]()


def _mm(name, pairs, m, n, k, mode, outs, epilogue, extras=(), tm=512, tn=512, tk=512, jobs=()):
    tm, tn, tk = _tile(m, tm, LANES if mode == "tn" else 8), _tile(n, tn), _tile(k, tk)
    ni, nj, nk = m // tm, n // tn, k // tk
    n_pairs, n_extra, n_out = len(pairs), len(extras), len(outs)
    in_specs, operands = [], []
    for (a, la, oa), (b, lb, ob) in pairs:
        if mode == "tn":
            in_specs.append(_bs((tk, tm), lambda i, j, kk, o=oa: (kk + o, i), la))
        else:
            in_specs.append(_bs((tm, tk), lambda i, j, kk, o=oa: (i + o, kk), la))
        if mode == "nt":
            in_specs.append(_bs((tn, tk), lambda i, j, kk, o=ob: (j + o, kk), lb))
        else:
            in_specs.append(_bs((tk, tn), lambda i, j, kk, o=ob: (kk + o, j), lb))
        operands += [a, b]
    for e, le, kind in extras:
        if kind == "mn":
            in_specs.append(_bs((tm, tn), lambda i, j, kk: (i, j), le))
        else:
            in_specs.append(_bs((1, tn), lambda i, j, kk: (0, j), le))
        operands.append(e)
    out_specs, out_shape, flipped = [], [], []
    for o in outs:
        rows, dt, off = o[:3]
        flipped.append(len(o) > 3 and o[3])
        if flipped[-1]:
            out_specs.append(pl.BlockSpec((tn, tm), lambda i, j, kk: (j, i)))
            out_shape.append(jax.ShapeDtypeStruct((n, rows), dt))
        else:
            out_specs.append(pl.BlockSpec((tm, tn), lambda i, j, kk, o=off: (i + o, j)))
            out_shape.append(jax.ShapeDtypeStruct((rows, n), dt))
    dims = _DIMS[mode]
    job_ops, job_out, job_sems = _job_io(jobs)
    n_in = len(operands)

    def body(*refs):
        ab = refs[: 2 * n_pairs]
        ex = refs[2 * n_pairs: n_in]
        job_src = refs[n_in: n_in + len(job_ops)]
        out_refs = refs[n_in + len(job_ops): n_in + len(job_ops) + n_out]
        job_dst = refs[n_in + len(job_ops) + n_out: n_in + len(job_ops) + n_out + len(job_out)]
        kk = pl.program_id(2)
        if jobs:
            step = (pl.program_id(0) * nj + pl.program_id(1)) * nk + kk
            _run_phases(_job_phases(jobs, job_src, job_dst, refs[len(refs) - len(job_sems):]), step, ni * nj * nk)

        def partial():
            tot = None
            for p in range(n_pairs):
                a = ab[2 * p][...].astype(BF16)
                b = ab[2 * p + 1][...].astype(BF16)
                prod = lax.dot_general(a, b, dims, preferred_element_type=F32)
                tot = prod if tot is None else tot + prod
            return tot

        def finish(total):
            res = epilogue(total, *[e[...] for e in ex])
            for o_ref, o, flip in zip(out_refs, res, flipped):
                o_ref[...] = (o.T if flip else o).astype(o_ref.dtype)

        if nk == 1:
            finish(partial())
            return
        acc = refs[n_in + len(job_ops) + n_out + len(job_out)]

        @pl.when(kk == 0)
        def _():
            acc[...] = jnp.zeros_like(acc)

        acc[...] += partial()

        @pl.when(kk == nk - 1)
        def _():
            finish(acc[...])

    sem = ("arbitrary",) * 3 if jobs else ("parallel", "parallel", "arbitrary")
    res = pl.pallas_call(
        body, name=name, grid=(ni, nj, nk), in_specs=in_specs + [ANY] * len(job_ops),
        out_specs=out_specs + [ANY] * len(job_out), out_shape=out_shape + job_out,
        scratch_shapes=([pltpu.VMEM((tm, tn), F32)] if nk > 1 else []) + job_sems, compiler_params=_params(sem),
    )(*operands, *job_ops)
    return (res[:n_out], res[n_out:]) if jobs else res


def _op(a, lead=None, off=0):
    return (a, lead, off)


def _head_rms(blk, gain):
    r = lax.rsqrt(jnp.mean(blk * blk, axis=-1, keepdims=True) + EPS)
    return blk * r * gain, r


def _rms_fwd(name, x, gain_row):
    t, d = x.shape
    tm = _tile(t, 512)

    def body(x_ref, g_ref, h_ref, ht_ref):
        xv = x_ref[...]
        r = lax.rsqrt(jnp.mean(xv * xv, axis=-1, keepdims=True) + EPS)
        h = (xv * r * g_ref[...]).astype(BF16)
        h_ref[...] = h
        ht_ref[...] = h.T

    return pl.pallas_call(
        body, name=name, grid=(t // tm,),
        in_specs=[pl.BlockSpec((tm, d), lambda i: (i, 0)), pl.BlockSpec((1, d), lambda i: (0, 0))],
        out_specs=[pl.BlockSpec((tm, d), lambda i: (i, 0)), pl.BlockSpec((d, tm), lambda i: (0, i))],
        out_shape=[jax.ShapeDtypeStruct((t, d), BF16), jax.ShapeDtypeStruct((d, t), BF16)],
        compiler_params=_params(("parallel",)),
    )(x, gain_row)


def _rms_bwd_math(dh, xv, g):
    r = lax.rsqrt(jnp.mean(xv * xv, axis=-1, keepdims=True) + EPS)
    u = dh * g
    dx = r * u - xv * (r * r * r * jnp.mean(u * xv, axis=-1, keepdims=True))
    dg = jnp.sum(dh * xv * r, axis=0, keepdims=True)
    return dx, dg


def _rms_bwd(name, dh, x, gain_row, dres=None):
    t, d = x.shape
    tm = _tile(t, 256, 8)
    has_res = dres is not None

    def body(*refs):
        if has_res:
            dh_ref, x_ref, g_ref, dres_ref, dx_ref, dxb_ref, dg_ref = refs
        else:
            dh_ref, x_ref, g_ref, dx_ref, dxb_ref, dg_ref = refs
        dx, dg = _rms_bwd_math(dh_ref[...], x_ref[...], g_ref[...])
        if has_res:
            dx = dx + dres_ref[...]
        dx_ref[...] = dx
        dxb_ref[...] = dx.astype(BF16)

        @pl.when(pl.program_id(0) == 0)
        def _():
            dg_ref[...] = jnp.zeros_like(dg_ref)

        dg_ref[...] += dg

    row = pl.BlockSpec((tm, d), lambda i: (i, 0))
    vec = pl.BlockSpec((1, d), lambda i: (0, 0))
    return pl.pallas_call(
        body, name=name, grid=(t // tm,),
        in_specs=[row, row, vec] + ([row] if has_res else []),
        out_specs=[row, row, vec],
        out_shape=[jax.ShapeDtypeStruct((t, d), F32), jax.ShapeDtypeStruct((t, d), BF16),
                   jax.ShapeDtypeStruct((1, d), F32)],
        compiler_params=_params(("arbitrary",)),
    )(*([dh, x, gain_row] + ([dres] if has_res else [])))


def _ffn_up(name, h, wgt, wut, jobs=()):
    t, d = h.shape
    f = wgt.shape[0]
    tm, tn = _tile(t, 1024), _tile(f, 512)
    ni, nj = t // tm, f // tn
    job_ops, job_out, job_sems = _job_io(jobs)

    def body(*refs):
        h_ref, wg_ref, wu_ref = refs[:3]
        job_src = refs[3: 3 + len(job_ops)]
        g_ref, u_ref, a_ref, at_ref = refs[3 + len(job_ops): 7 + len(job_ops)]
        job_dst = refs[7 + len(job_ops): 7 + len(job_ops) + len(job_out)]
        if jobs:
            step = pl.program_id(0) * nj + pl.program_id(1)
            _run_phases(_job_phases(jobs, job_src, job_dst, refs[len(refs) - len(job_sems):]), step, ni * nj)
        hv = h_ref[...]
        g = lax.dot_general(hv, wg_ref[...], _DIMS["nt"], preferred_element_type=F32)
        u = lax.dot_general(hv, wu_ref[...], _DIMS["nt"], preferred_element_type=F32)
        sig = jax.nn.sigmoid(g)
        silu = g * sig
        g_ref[...] = silu.astype(BF16)
        u_ref[...] = (u * (sig * (1.0 + g * (1.0 - sig)))).astype(BF16)
        a = (silu * u).astype(BF16)
        a_ref[...] = a
        at_ref[...] = a.T

    wspec = pl.BlockSpec((tn, d), lambda i, j: (j, 0))
    ospec = pl.BlockSpec((tm, tn), lambda i, j: (i, j))
    res = pl.pallas_call(
        body, name=name, grid=(ni, nj),
        in_specs=[pl.BlockSpec((tm, d), lambda i, j: (i, 0)), wspec, wspec] + [ANY] * len(job_ops),
        out_specs=[ospec] * 3 + [pl.BlockSpec((tn, tm), lambda i, j: (j, i))] + [ANY] * len(job_out),
        out_shape=[jax.ShapeDtypeStruct((t, f), BF16)] * 3 + [jax.ShapeDtypeStruct((f, t), BF16)] + job_out,
        scratch_shapes=job_sems,
        compiler_params=_params(("arbitrary", "arbitrary") if jobs else ("parallel", "parallel")),
    )(h, wgt, wut, *job_ops)
    return res[:4], res[4:]


def _ffn_bwd(tag, dout, doutb, saved, gain_row, wgt, wut, wd, jobs, swap_extra):
    x, ht, g, u, at = saved
    t, d = x.shape
    f = wd.shape[0]
    jobs = dict(jobs)
    got = {}

    def mm(key, *args, **kw):
        jb = jobs.get(key, ())
        res = _mm(f"{tag}_bwd_{key}", *args, jobs=jb, **kw)
        if jb:
            res, got[key] = res
        return res

    def act_bwd(acc, sv, vv):
        da = 0.5 * acc
        return da * vv.astype(F32), da * sv.astype(F32)

    ident = lambda acc: (acc,)
    dg, du = mm("act", [(_op(doutb), _op(wd))], t, f, d, "nt", [(t, BF16, 0), (t, BF16, 0)],
                act_bwd, extras=[(g, None, "mn"), (u, None, "mn")], tm=1024, tn=512, tk=2048)
    (dwd,) = mm("wd", [(_op(at), _op(doutb))], f, d, t, "nn", [(f, F32, 0)], lambda acc: (0.5 * acc,),
                tm=1408, tn=1024, tk=2048)
    (dwgt,) = mm("wg", [(_op(ht), _op(dg))], d, f, t, "nn", [(d, F32, 0, True)], ident, tm=1024, tn=1408, tk=2048)
    (dwut,) = mm("wu", [(_op(ht), _op(du))], d, f, t, "nn", [(d, F32, 0, True)], ident, tm=1024, tn=1408, tk=2048)
    jobs["h"] = [_SwapJob(list(swap_extra) + [dwgt, dwut, dwd])]
    (dh,) = mm("h", [(_op(dg), _op(wgt)), (_op(du), _op(wut))], t, d, f, "nn", [(t, F32, 0)], ident,
               tm=1024, tn=1024, tk=1408)
    dx, dxb, dgain = _rms_bwd(f"{tag}_bwd_norm", dh, x, gain_row, dout)
    return dx, dxb, dgain, [dwgt, dwut, dwd], got


def _inv_count(t0, rows, window):
    tpos = t0 + lax.broadcasted_iota(jnp.int32, (rows, 1), 0)
    return 1.0 / jnp.minimum(tpos + 1, window).astype(F32)


def _pool_fwd(name, x, gain_row, wp, scale_row, layer):
    t, d = x.shape
    n_g = len(POOL_WINDOWS)
    p = d // n_g
    tm = _tile(t, 256, 8)
    hb = tm // HALO

    def body(x_ref, xp_ref, g_ref, wp_ref, s_ref, out_ref, diff_ref, ypre_ref):
        i = pl.program_id(0)
        gain = g_ref[...]

        def norm(v):
            return v * lax.rsqrt(jnp.mean(v * v, axis=-1, keepdims=True) + EPS) * gain

        xv = x_ref[...]
        h = norm(xv)
        hp = jnp.where(i > 0, norm(xp_ref[...]), 0.0)
        ext = jnp.concatenate([hp, h], axis=0)
        for gi, w in enumerate(POOL_WINDOWS):
            cols = slice(gi * p, (gi + 1) * p)
            s = ext[:, cols]
            k = 1
            while k < w:
                s = s + pltpu.roll(s, k, 0)
                k *= 2
            pooled = s[HALO:, :] * _inv_count(i * tm, tm, w)
            diff = (pooled - h[:, cols]).astype(BF16)
            y = jnp.dot(diff, wp_ref[gi], preferred_element_type=F32)
            diff_ref[:, cols] = diff
            ypre_ref[:, cols] = y.astype(BF16)
            out_ref[:, cols] = xv[:, cols] + y * s_ref[:, cols]

    row = pl.BlockSpec((tm, d), lambda i: (i, 0))
    vec = pl.BlockSpec((1, d), lambda i: (0, 0))
    return pl.pallas_call(
        body, name=name, grid=(t // tm,),
        in_specs=[row, pl.BlockSpec((HALO, d), lambda i: (jnp.maximum(i * hb - 1, 0), 0)), vec,
                  _bs((n_g, p, p), lambda i: (0, 0, 0), layer), vec],
        out_specs=[row, row, row],
        out_shape=[jax.ShapeDtypeStruct((t, d), F32), jax.ShapeDtypeStruct((t, d), BF16),
                   jax.ShapeDtypeStruct((t, d), BF16)],
        compiler_params=_params(("parallel",)),
    )(x, x, gain_row, wp, scale_row)


def _pool_bwd_mat(name, dout, diff, ypre, wp, scale_row, layer):
    t, d = dout.shape
    n_g = len(POOL_WINDOWS)
    p = d // n_g
    tm = _tile(t, 256, 8)

    def body(do_ref, diff_ref, ypre_ref, wp_ref, s_ref, dd_ref, dwp_ref, ds_ref):
        @pl.when(pl.program_id(0) == 0)
        def _():
            dwp_ref[...] = jnp.zeros_like(dwp_ref)
            ds_ref[...] = jnp.zeros_like(ds_ref)

        dov = do_ref[...]
        ds_ref[...] += jnp.sum(dov * ypre_ref[...].astype(F32), axis=0, keepdims=True)
        dy = (dov * s_ref[...]).astype(BF16)
        for gi in range(n_g):
            cols = slice(gi * p, (gi + 1) * p)
            dwp_ref[gi] += lax.dot_general(diff_ref[:, cols], dy[:, cols], _DIMS["tn"], preferred_element_type=F32)
            dd_ref[:, cols] = lax.dot_general(dy[:, cols], wp_ref[gi], _DIMS["nt"], preferred_element_type=F32)

    row = pl.BlockSpec((tm, d), lambda i: (i, 0))
    vec = pl.BlockSpec((1, d), lambda i: (0, 0))
    return pl.pallas_call(
        body, name=name, grid=(t // tm,),
        in_specs=[row, row, row, _bs((n_g, p, p), lambda i: (0, 0, 0), layer), vec],
        out_specs=[row, pl.BlockSpec((n_g, p, p), lambda i: (0, 0, 0)), vec],
        out_shape=[jax.ShapeDtypeStruct((t, d), F32), jax.ShapeDtypeStruct((n_g, p, p), F32),
                   jax.ShapeDtypeStruct((1, d), F32)],
        compiler_params=_params(("arbitrary",)),
    )(dout, diff, ypre, wp, scale_row)


def _pool_bwd_win(name, ddiff, x, gain_row, dout):
    t, d = x.shape
    n_g = len(POOL_WINDOWS)
    p = d // n_g
    tm = _tile(t, 256, 8)
    hb = tm // HALO
    n_t = t // tm
    rows = tm + HALO

    def body(e_ref, en_ref, x_ref, g_ref, do_ref, dx_ref, dxb_ref, dg_ref):
        i = pl.program_id(0)
        e = e_ref[...]
        en = jnp.where(i < n_t - 1, en_ref[...], 0.0)
        ext = jnp.concatenate([e, en], axis=0)
        parts = []
        for gi, w in enumerate(POOL_WINDOWS):
            cols = slice(gi * p, (gi + 1) * p)
            s = ext[:, cols] * _inv_count(i * tm, rows, w)
            k = 1
            while k < w:
                s = s + pltpu.roll(s, rows - k, 0)
                k *= 2
            parts.append(s[:tm, :] - e[:, cols])
        dh = jnp.concatenate(parts, axis=1)
        dx, dg = _rms_bwd_math(dh, x_ref[...], g_ref[...])
        dx = dx + do_ref[...]
        dx_ref[...] = dx
        dxb_ref[...] = dx.astype(BF16)

        @pl.when(i == 0)
        def _():
            dg_ref[...] = jnp.zeros_like(dg_ref)

        dg_ref[...] += dg

    row = pl.BlockSpec((tm, d), lambda i: (i, 0))
    vec = pl.BlockSpec((1, d), lambda i: (0, 0))
    return pl.pallas_call(
        body, name=name, grid=(n_t,),
        in_specs=[row, pl.BlockSpec((HALO, d), lambda i: (jnp.minimum((i + 1) * hb, t // HALO - 1), 0)), row, vec, row],
        out_specs=[row, row, vec],
        out_shape=[jax.ShapeDtypeStruct((t, d), F32), jax.ShapeDtypeStruct((t, d), BF16),
                   jax.ShapeDtypeStruct((1, d), F32)],
        compiler_params=_params(("arbitrary",)),
    )(ddiff, ddiff, x, gain_row, dout)


def _rel_index(n_rel):
    dd = lax.broadcasted_iota(jnp.int32, (n_rel, BAND_W), 1)
    e = lax.broadcasted_iota(jnp.int32, (n_rel, BAND_W), 0)
    idx = jnp.where(dd < LEFT + 2 * CHUNK, jnp.clip(LEFT - dd, -(CHUNK - 1), REL_MAX) + (CHUNK - 1), n_rel - 1)
    return (idx == e).astype(F32)


def _bias_rows(name, table):
    n_h, n_rel = table.shape

    def body(t_ref, o_ref):
        o_ref[...] = jnp.dot(t_ref[...], _rel_index(n_rel), preferred_element_type=F32,
                             precision=lax.Precision.HIGHEST)

    rows = pl.pallas_call(body, name=name, out_shape=jax.ShapeDtypeStruct((n_h, BAND_W), F32))(table)
    return rows.reshape(n_h, 1, BAND_W)


def _bias_rows_bwd(name, dbias, n_rel):
    n_h = dbias.shape[0]

    def body(db_ref, o_ref):
        rr = lax.broadcasted_iota(jnp.int32, (TQ, TQ), 0)
        cc = lax.broadcasted_iota(jnp.int32, (TQ, TQ), 1)
        flipped = jnp.dot((rr + cc == TQ - 1).astype(F32), db_ref[...], preferred_element_type=F32,
                          precision=lax.Precision.HIGHEST)
        unskew = pltpu.roll(flipped, BAND_W - TQ + 1, 1, stride=1, stride_axis=0)
        drow = jnp.sum(unskew, axis=0, keepdims=True)
        o_ref[...] = lax.dot_general(drow, _rel_index(n_rel), _DIMS["nt"], preferred_element_type=F32,
                                     precision=lax.Precision.HIGHEST)

    return pl.pallas_call(
        body, name=name, grid=(n_h,),
        in_specs=[pl.BlockSpec((None, TQ, BAND_W), lambda h: (h, 0, 0))],
        out_specs=pl.BlockSpec((None, 1, n_rel), lambda h: (h, 0, 0)),
        out_shape=jax.ShapeDtypeStruct((n_h, 1, n_rel), F32), compiler_params=_params(("parallel",)),
    )(dbias)


def _band_bias(row):
    tile = pltpu.roll(jnp.broadcast_to(row, (TQ, BAND_W)), 0, 1, stride=1, stride_axis=0)
    qc = lax.broadcasted_iota(jnp.int32, (TQ, BAND_W), 0) // CHUNK
    kc = lax.broadcasted_iota(jnp.int32, (TQ, BAND_W), 1) // CHUNK
    return jnp.where((kc >= qc) & (kc <= qc + LEFT_CHUNKS), tile, NEG_INF)


def _band_probs(qv, kb, bias, t0):
    s = lax.dot_general(qv, kb, _DIMS["nt"], preferred_element_type=F32) * (qv.shape[-1] ** -0.5) + bias
    col = lax.broadcasted_iota(jnp.int32, (TQ, BAND_W), 1)
    s = jnp.where(col >= LEFT - t0, s, NEG_INF)
    e = jnp.exp(s - jnp.max(s, axis=-1, keepdims=True))
    return e / jnp.sum(e, axis=-1, keepdims=True)


def _valid_rows(blk, t0):
    r = lax.broadcasted_iota(jnp.int32, (BAND_W, 1), 0)
    return jnp.where(r >= LEFT - t0, blk, jnp.zeros_like(blk))


def _attn_fwd(name, qr, gain_row, kpad, vpad, brow):
    t, d = qr.shape
    hd = gain_row.shape[-1]
    n_h = d // hd

    def body(q_ref, g_ref, k_ref, v_ref, b_ref, o_ref, ot_ref, bias_ref):
        i = pl.program_id(1)

        @pl.when(i == 0)
        def _():
            bias_ref[...] = _band_bias(b_ref[...])

        t0 = pl.multiple_of(i * TQ, TQ)
        qv = _head_rms(q_ref[...].astype(F32), g_ref[...])[0].astype(BF16)
        kb = _valid_rows(k_ref[pl.ds(t0, BAND_W), :], t0)
        vb = _valid_rows(v_ref[pl.ds(t0, BAND_W), :], t0)
        pr = _band_probs(qv, kb, bias_ref[...], t0)
        o = jnp.dot(pr.astype(BF16), vb, preferred_element_type=F32).astype(BF16)
        o_ref[...] = o
        ot_ref[...] = o.T

    kv_spec = pl.BlockSpec((LEFT + t, hd), lambda h, i: (0, h))
    return pl.pallas_call(
        body, name=name, grid=(n_h, t // TQ),
        in_specs=[pl.BlockSpec((TQ, hd), lambda h, i: (i, h)), pl.BlockSpec((1, hd), lambda h, i: (0, 0)),
                  kv_spec, kv_spec, pl.BlockSpec((None, 1, BAND_W), lambda h, i: (h, 0, 0))],
        out_specs=[pl.BlockSpec((TQ, hd), lambda h, i: (i, h)), pl.BlockSpec((hd, TQ), lambda h, i: (h, i))],
        out_shape=[jax.ShapeDtypeStruct((t, d), BF16), jax.ShapeDtypeStruct((d, t), BF16)],
        scratch_shapes=[pltpu.VMEM((TQ, BAND_W), F32)],
        compiler_params=_params(("parallel", "arbitrary")),
    )(qr, gain_row, kpad, vpad, brow)


def _attn_bwd(name, qr, gain_row, kpad, vpad, brow, o, do, dk_in=None, dv_in=None):
    t, d = qr.shape
    hd = gain_row.shape[-1]
    n_h = d // hd
    scale = hd ** -0.5
    has_init = dk_in is not None

    def body(*refs):
        if has_init:
            (q_ref, g_ref, k_ref, v_ref, b_ref, o_ref, do_ref, dki_ref, dvi_ref,
             dq_ref, dk_ref, dv_ref, db_ref, dg_ref, bias_ref) = refs
        else:
            (q_ref, g_ref, k_ref, v_ref, b_ref, o_ref, do_ref,
             dq_ref, dk_ref, dv_ref, db_ref, dg_ref, bias_ref) = refs
        h = pl.program_id(0)
        i = pl.program_id(1)

        @pl.when(i == 0)
        def _():
            bias_ref[...] = _band_bias(b_ref[...])
            db_ref[...] = jnp.zeros_like(db_ref)
            if has_init:
                dk_ref[...] = dki_ref[...]
                dv_ref[...] = dvi_ref[...]
            else:
                dk_ref[...] = jnp.zeros_like(dk_ref)
                dv_ref[...] = jnp.zeros_like(dv_ref)

        @pl.when((i == 0) & (h == 0))
        def _():
            dg_ref[...] = jnp.zeros_like(dg_ref)

        t0 = pl.multiple_of(i * TQ, TQ)
        gain = g_ref[...]
        qraw = q_ref[...].astype(F32)
        qn, r = _head_rms(qraw, gain)
        qv = qn.astype(BF16)
        kb = _valid_rows(k_ref[pl.ds(t0, BAND_W), :], t0)
        vb = _valid_rows(v_ref[pl.ds(t0, BAND_W), :], t0)
        pr = _band_probs(qv, kb, bias_ref[...], t0)
        dov = do_ref[...]
        delta = jnp.sum(dov.astype(F32) * o_ref[...].astype(F32), axis=-1, keepdims=True)
        dp = lax.dot_general(dov, vb, _DIMS["nt"], preferred_element_type=F32)
        ds = pr * (dp - delta)
        db_ref[...] += ds
        dsb = ds.astype(BF16)
        dq = jnp.dot(dsb, kb, preferred_element_type=F32) * scale
        dk_ref[pl.ds(t0, BAND_W), :] += lax.dot_general(dsb, qv, _DIMS["tn"], preferred_element_type=F32) * scale
        dv_ref[pl.ds(t0, BAND_W), :] += lax.dot_general(pr.astype(BF16), dov, _DIMS["tn"], preferred_element_type=F32)
        u = dq * gain
        dq_ref[...] = (r * u - qraw * (r * r * r * jnp.mean(u * qraw, axis=-1, keepdims=True))).astype(BF16)
        dg_ref[...] += jnp.sum(dq * qraw * r, axis=0, keepdims=True)

    qspec = pl.BlockSpec((TQ, hd), lambda h, i: (i, h))
    kv_spec = pl.BlockSpec((LEFT + t, hd), lambda h, i: (0, h))
    gspec = pl.BlockSpec((1, hd), lambda h, i: (0, 0))
    ins = [qr, gain_row, kpad, vpad, brow, o, do] + ([dk_in, dv_in] if has_init else [])
    return pl.pallas_call(
        body, name=name, grid=(n_h, t // TQ),
        in_specs=[qspec, gspec, kv_spec, kv_spec, pl.BlockSpec((None, 1, BAND_W), lambda h, i: (h, 0, 0)), qspec, qspec]
        + ([kv_spec, kv_spec] if has_init else []),
        out_specs=[qspec, kv_spec, kv_spec, pl.BlockSpec((None, TQ, BAND_W), lambda h, i: (h, 0, 0)), gspec],
        out_shape=[jax.ShapeDtypeStruct((t, d), BF16), jax.ShapeDtypeStruct((LEFT + t, d), F32),
                   jax.ShapeDtypeStruct((LEFT + t, d), F32), jax.ShapeDtypeStruct((n_h, TQ, BAND_W), F32),
                   jax.ShapeDtypeStruct((1, hd), F32)],
        scratch_shapes=[pltpu.VMEM((TQ, BAND_W), F32)],
        compiler_params=_params(("arbitrary", "arbitrary")),
    )(*ins)


def _head_norm_bwd(name, dy, off_rows, yraw, gain_row):
    t, d = yraw.shape
    hd = gain_row.shape[-1]
    tm = _tile(t, 512, 8)
    off = off_rows // tm

    def body(dy_ref, y_ref, g_ref, dx_ref, dg_ref):
        @pl.when((pl.program_id(0) == 0) & (pl.program_id(1) == 0))
        def _():
            dg_ref[...] = jnp.zeros_like(dg_ref)

        gain = g_ref[...]
        yv = y_ref[...].astype(F32)
        dyv = dy_ref[...]
        r = lax.rsqrt(jnp.mean(yv * yv, axis=-1, keepdims=True) + EPS)
        u = dyv * gain
        dx_ref[...] = (r * u - yv * (r * r * r * jnp.mean(u * yv, axis=-1, keepdims=True))).astype(BF16)
        dg_ref[...] += jnp.sum(dyv * yv * r, axis=0, keepdims=True)

    blk = pl.BlockSpec((tm, hd), lambda i, h: (i, h))
    gspec = pl.BlockSpec((1, hd), lambda i, h: (0, 0))
    return pl.pallas_call(
        body, name=name, grid=(t // tm, d // hd),
        in_specs=[pl.BlockSpec((tm, hd), lambda i, h: (i + off, h)), blk, gspec],
        out_specs=[blk, gspec],
        out_shape=[jax.ShapeDtypeStruct((t, d), BF16), jax.ShapeDtypeStruct((1, hd), F32)],
        compiler_params=_params(("arbitrary", "arbitrary")),
    )(dy, yraw, gain_row)


def _head_norm_epilogue(hd):
    def epi(acc, gain):
        parts = [_head_rms(acc[:, c:c + hd], gain[:, c:c + hd])[0] for c in range(0, acc.shape[-1], hd)]
        return acc, jnp.concatenate(parts, axis=1)
    return epi


def _loss_head(name, y, target):
    t, d = y.shape
    tm = _tile(t, 512, 8)

    def body(y_ref, t_ref, dy_ref, dyb_ref, l_ref):
        @pl.when(pl.program_id(0) == 0)
        def _():
            l_ref[...] = jnp.zeros_like(l_ref)

        err = y_ref[...] - t_ref[...]
        dy = err * (1.0 / d)
        dy_ref[...] = dy
        dyb_ref[...] = dy.astype(BF16)
        per_tok = jnp.mean(err * err, axis=-1, keepdims=True)
        l_ref[...] += 0.5 * jnp.sum(per_tok, axis=0, keepdims=True)

    row = pl.BlockSpec((tm, d), lambda i: (i, 0))
    return pl.pallas_call(
        body, name=name, grid=(t // tm,), in_specs=[row, row],
        out_specs=[row, row, pl.BlockSpec((1, LANES), lambda i: (0, 0))],
        out_shape=[jax.ShapeDtypeStruct((t, d), F32), jax.ShapeDtypeStruct((t, d), BF16),
                   jax.ShapeDtypeStruct((1, LANES), F32)],
        compiler_params=_params(("arbitrary",)),
    )(y, target)


def _forward_backward(x, target, sh, small, core):
    t, d = x.shape
    n_layers = small["n1"].shape[0]
    n_a = sh["pool"].shape[0]
    n_b = sh["wq"].shape[0]
    hd = small["kg"].shape[0]
    n_rel = small["rel"].shape[-1]
    n_g = len(POOL_WINDOWS)
    p = d // n_g
    row = lambda v: v.reshape(1, -1)
    kg_row = row(small["kg"])
    big = dict(tm=1024, tn=1024)
    ident = lambda acc: (acc,)

    stages = [(l, s) for l in range(n_layers) for s in (1, 2)]
    ffn_keys = lambda k, names: [(f"{nm}{stages[k][1]}", stages[k][0]) for nm in names]
    up_keys = {k: ffn_keys(k + 1, "gu") for k in range(len(stages) - 1)}
    down_keys = {k: ffn_keys(k + 1, "d") for k in range(len(stages) - 1)}
    up_keys[0] = ffn_keys(0, "d") + up_keys[0]
    up_keys[2 * n_a - 2] = up_keys[2 * n_a - 2] + [("wk", 0), ("wv", 0)]
    for b in range(n_b):
        up_keys[2 * (n_a + b) - 1] = up_keys[2 * (n_a + b) - 1] + [("wq", b), ("wo", b)]
    first_keys = ffn_keys(0, "gu") + [("pool", l) for l in range(n_a)] + [("pscale", 0)]
    full = {}

    def gather(keys):
        return [_gather_job([(sh[nm], l) for nm, l in keys])] if keys else []

    full.update(zip(first_keys, _run_jobs("gather_first", gather(first_keys))))
    pscale = full[("pscale", 0)].reshape(N_CHIPS, 8, p)[:, :n_a].transpose(1, 0, 2).reshape(n_a, d)
    pool_full = lambda l: (full[("pool", l)].reshape(N_CHIPS, n_g, p // N_CHIPS, p).transpose(1, 0, 2, 3)
                           .reshape(n_g, p, p))

    saved, mixed = [], []
    kv_saved = None
    kpad = vpad = None
    for k, (l, s) in enumerate(stages):
        if s == 2:
            x_in = x
            if l < n_a:
                wp = pool_full(l)
                x, diff, ypre = _pool_fwd(f"l{l}_pool", x, row(small["nm"][l]), wp, row(pscale[l]), None)
                mixed.append((x_in, diff, ypre, wp))
            else:
                b = l - n_a
                h, ht = _rms_fwd(f"l{l}_mix_norm", x, row(small["nm"][l]))
                (qr,) = _mm(f"l{l}_q", [(_op(h), _op(full[("wq", b)]))], t, d, d, "nn", [(t, BF16, 0)], ident,
                            tk=2048, **big)
                brow = _bias_rows(f"l{l}_bias", small["rel"][b])
                o, ot = _attn_fwd(f"l{l}_attn", qr, row(small["qg"][b]), kpad, vpad, brow)
                (x,) = _mm(f"l{l}_o", [(_op(o), _op(full[("wo", b)]))], t, d, d, "nn", [(t, F32, 0)],
                           lambda acc, xv: (xv + acc,), extras=[(x_in, None, "mn")], tk=2048, **big)
                mixed.append((x_in, ht, qr, brow, o, ot))
        tag = f"l{l}_ffn{s}"
        h, ht = _rms_fwd(f"{tag}_norm", x, row(small[f"n{s}"][l]))
        (g, u, a, at), got = _ffn_up(f"{tag}_up", h, full[(f"g{s}", l)], full[(f"u{s}", l)], gather(up_keys.get(k, [])))
        full.update(zip(up_keys.get(k, []), got))
        jobs = gather(down_keys.get(k, []))
        res = _mm(f"{tag}_down", [(_op(a), _op(full[(f"d{s}", l)]))], t, d, a.shape[1], "nn", [(t, F32, 0)],
                  lambda acc, xv: (xv + 0.5 * acc,), extras=[(x, None, "mn")], tm=1024, tn=1024, tk=2816, jobs=jobs)
        (x_new,), got = res if jobs else (res, [])
        full.update(zip(down_keys.get(k, []), got))
        saved.append((x, ht, g, u, at))
        x = x_new
        if k == 2 * n_a - 1:
            hk, hkt = _rms_fwd("kv_norm", x, row(small["kvn"]))
            pad_blocks = LEFT // _tile(t, 512, 8)
            kr, kpad = _mm("kv_k", [(_op(hk), _op(full[("wk", 0)]))], t, d, d, "nn",
                           [(t, BF16, 0), (LEFT + t, BF16, pad_blocks)],
                           _head_norm_epilogue(hd), extras=[(jnp.tile(kg_row, (1, d // hd)), None, "row")],
                           tn=1024, tk=2048)
            (vpad,) = _mm("kv_v", [(_op(hk), _op(full[("wv", 0)]))], t, d, d, "nn", [(LEFT + t, BF16, pad_blocks)],
                          ident, tn=1024, tk=2048)
            kv_saved = (x, hkt, kr)

    dx, dxb, loss = _loss_head("loss_head", x, target)

    gs = {nm: [None] * n_layers for nm in ("n1", "nm", "n2")}
    gs.update(qg=[None] * n_b, rel=[None] * n_b, pscale=[None] * n_a)
    grads, sums, rbufs = [], [], {}
    dk = dv = None
    for k in reversed(range(len(stages))):
        l, s = stages[k]
        if k == 2 * n_a - 1:
            x_kv, hkt, kr = kv_saved
            dkr, gs["kg"] = _head_norm_bwd("kv_bwd_knorm", dk, LEFT, kr, kg_row)
            off_k = LEFT // _tile(t, 512)
            (gwk,) = _mm("kv_bwd_wk", [(_op(hkt), _op(dkr))], d, d, t, "nn", [(d, F32, 0)], ident, tk=2048, **big)
            (gwv,) = _mm("kv_bwd_wv", [(_op(hkt), _op(dv, None, off_k))], d, d, t, "nn", [(d, F32, 0)], ident,
                         tk=512, **big)
            grads += [(("wk", 0), gwk), (("wv", 0), gwv)]
            off_m = LEFT // _tile(t, 512, 8)
            (dhk,) = _mm("kv_bwd_h", [(_op(dkr), _op(full[("wk", 0)])), (_op(dv, None, off_m), _op(full[("wv", 0)]))],
                         t, d, d, "nt", [(t, F32, 0)], ident, tn=1024, tk=2048)
            dx, dxb, gs["kvn"] = _rms_bwd("kv_bwd_norm", dhk, x_kv, row(small["kvn"]), dx)
        tag = f"l{l}_ffn{s}"
        groups = [sums[n::3] for n in range(3)]
        jobs = {key: [_scatter_job([sm for _, sm in grp])] for key, grp in zip(("act", "wd", "wg"), groups) if grp}
        dx, dxb, gs[f"n{s}"][l], ffn_grads, got = _ffn_bwd(
            tag, dx, dxb, saved[k], row(small[f"n{s}"][l]), full[(f"g{s}", l)], full[(f"u{s}", l)],
            full[(f"d{s}", l)], jobs, [g for _, g in grads])
        for key, grp in zip(("act", "wd", "wg"), groups):
            rbufs.update(zip([kk for kk, _ in grp], got.get(key, [])))
        grads += list(zip(ffn_keys(k, "gud"), ffn_grads))
        sums = [(key, _half_sum(f"half_sum_{key[0]}_{key[1]}", g, xr, core))
                for (key, g), xr in zip(grads, got["h"])]
        grads = []
        if s == 2:
            if l < n_a:
                x_in, diff, ypre, wp = mixed[l]
                ddiff, gpool, gs["pscale"][l] = _pool_bwd_mat(f"l{l}_pool_bwd_mat", dx, diff, ypre, wp,
                                                              row(pscale[l]), None)
                grads.append((("pool", l), gpool.reshape(n_g, N_CHIPS, p // N_CHIPS, p).transpose(1, 0, 2, 3)
                              .reshape(n_g * p, p)))
                dx, dxb, gs["nm"][l] = _pool_bwd_win(f"l{l}_pool_bwd_win", ddiff, x_in, row(small["nm"][l]), dx)
            else:
                b = l - n_a
                x_in, ht, qr, brow, o, ot = mixed[l]
                (gwo,) = _mm(f"l{l}_bwd_wo", [(_op(ot), _op(dxb))], d, d, t, "nn", [(d, F32, 0)], ident, tk=2048, **big)
                (do,) = _mm(f"l{l}_bwd_o", [(_op(dxb), _op(full[("wo", b)]))], t, d, d, "nt", [(t, BF16, 0)], ident,
                            tk=2048, **big)
                dqr, dk, dv, dbias, gs["qg"][b] = _attn_bwd(f"l{l}_attn_bwd", qr, row(small["qg"][b]), kpad, vpad,
                                                             brow, o, do, dk, dv)
                gs["rel"][b] = _bias_rows_bwd(f"l{l}_bias_bwd", dbias, n_rel).reshape(-1, n_rel)
                (gwq,) = _mm(f"l{l}_bwd_wq", [(_op(ht), _op(dqr))], d, d, t, "nn", [(d, F32, 0)], ident, tk=2048, **big)
                grads += [(("wo", b), gwo), (("wq", b), gwq)]
                (dh,) = _mm(f"l{l}_bwd_qh", [(_op(dqr), _op(full[("wq", b)]))], t, d, d, "nt", [(t, F32, 0)], ident,
                            tk=2048, **big)
                dx, dxb, gs["nm"][l] = _rms_bwd(f"l{l}_mix_bwd_norm", dh, x_in, row(small["nm"][l]), dx)
    assert not grads
    rbufs.update(zip([key for key, _ in sums], _run_jobs("scatter_last", [_scatter_job([sm for _, sm in sums])])))
    return loss, dx, rbufs, gs


def _cast_pad(name, wsh, rows_to, cols_to, dtype=BF16):
    n_l, r, c = wsh.shape
    if cols_to != c:
        tr, tc_in, tc_out = _tile(r, 256, 16), c, cols_to
    else:
        tr, tc_in, tc_out = r, _tile(c, 256), _tile(c, 256)
    r_out = rows_to if tr == r else tr

    def body(w_ref, o_ref):
        o_ref[...] = jnp.zeros_like(o_ref)
        o_ref[: w_ref.shape[0], : w_ref.shape[1]] = w_ref[...].astype(dtype)

    return pl.pallas_call(
        body, name=name, grid=(n_l, r // tr, c // tc_in),
        in_specs=[pl.BlockSpec((None, tr, tc_in), lambda l, i, j: (l, i, j))],
        out_specs=pl.BlockSpec((None, r_out, tc_out), lambda l, i, j: (l, i, j)),
        out_shape=jax.ShapeDtypeStruct((n_l, rows_to, cols_to), dtype),
        compiler_params=_params(("parallel", "parallel", "parallel")),
    )(wsh)


def _place():
    x, y, c = lax.axis_index("x"), lax.axis_index("y"), lax.axis_index("c")
    return x, y, c, 2 * x + y


def _piece(ref, chip, half, rows, cols):
    return ref.at[pl.ds(chip * rows, rows), pl.ds(half * (cols // 2), cols // 2)]


class _ExchangeJob:
    def __init__(self, srcs, out_shapes, plan):
        self.srcs, self.out_shapes, self.plan = list(srcs), list(out_shapes), list(plan)
        n_it = len(plan)
        self.sems = [pltpu.SemaphoreType.DMA((n_it, 7)), pltpu.SemaphoreType.DMA((n_it, 7)),
                     pltpu.SemaphoreType.DMA((n_it,))]

    def phases(self, src_refs, out_refs, sem_refs):
        send_sems, recv_sems, loc_sems = sem_refs
        plan = self.plan

        def ctx():
            x, y, c, me = _place()
            return c, me, (x, y, 1 - c), [(1 - x, y), (x, 1 - y), (1 - x, 1 - y)]

        def remote(src, dst, it, k, dev):
            return pltpu.make_async_remote_copy(src_ref=src, dst_ref=dst, send_sem=send_sems.at[it, k],
                                                recv_sem=recv_sems.at[it, k], device_id=dev, device_id_type=MESH)

        def first(it):
            c, me, sib, chips = ctx()
            si, oi, sp, dp = plan[it]
            src, out = src_refs[si], out_refs[oi]
            mine = sp(src, me, c)
            local = pltpu.make_async_copy(mine, dp(out, c, me), loc_sems.at[it])
            sends = [remote(sp(src, 2 * px + py, c), dp(out, c, me), it, k, (px, py, c))
                     for k, (px, py) in enumerate(chips)]
            sends.append(remote(mine, dp(out, c, me), it, 3, sib))
            return local, sends

        def passed_on(it, k):
            c, me, sib, chips = ctx()
            px, py = chips[k]
            landed = dp_of(it)(out_refs[plan[it][1]], c, 2 * px + py)
            return remote(landed, landed, it, k, (px, py, c)), remote(landed, landed, it, 4 + k, sib)

        def dp_of(it):
            return plan[it][3]

        def start():
            for it in range(len(plan)):
                local, sends = first(it)
                local.start()
                for cp in sends:
                    cp.start()

        def mid():
            for it in range(len(plan)):
                for k in range(3):
                    arrival, onward = passed_on(it, k)
                    arrival.wait_recv()
                    onward.start()

        def finish():
            c, me, sib, chips = ctx()
            for it in range(len(plan)):
                out = out_refs[plan[it][1]]
                theirs = dp_of(it)(out, 1 - c, me)
                remote(theirs, theirs, it, 3, sib).wait_recv()
                for k, (px, py) in enumerate(chips):
                    theirs = dp_of(it)(out, 1 - c, 2 * px + py)
                    remote(theirs, theirs, it, 4 + k, sib).wait_recv()
            for it in range(len(plan)):
                local, sends = first(it)
                for cp in sends:
                    cp.wait_send()
                for k in range(3):
                    passed_on(it, k)[1].wait_send()
                local.wait()

        return start, mid, finish


def _gather_job(items):
    srcs, out_shapes, plan = [], [], []
    for n, (sh, layer) in enumerate(items):
        _, r, cc = sh.shape
        srcs.append(sh)
        out_shapes.append(((N_CHIPS * r, cc), sh.dtype))
        sp = lambda ref, chip, c, layer=layer, cc=cc: ref.at[layer, :, pl.ds(c * (cc // 2), cc // 2)]
        dp = lambda ref, half, chip, r=r, cc=cc: _piece(ref, chip, half, r, cc)
        plan.append((n, n, sp, dp))
    return _ExchangeJob(srcs, out_shapes, plan)


def _scatter_job(sums):
    out_shapes, plan = [], []
    for n, s in enumerate(sums):
        r4, hc = s.shape
        r = r4 // N_CHIPS
        out_shapes.append(((2, N_CHIPS, r, hc), BF16))
        sp = lambda ref, chip, c, r=r: ref.at[pl.ds(chip * r, r), :]
        dp = lambda ref, half, chip: ref.at[half, chip]
        plan.append((n, n, sp, dp))
    return _ExchangeJob(sums, out_shapes, plan)


class _SwapJob:
    def __init__(self, grads):
        self.srcs = list(grads)
        self.out_shapes = [((g.shape[0], g.shape[1] // 2), F32) for g in grads]
        self.sems = [pltpu.SemaphoreType.DMA((len(grads),)), pltpu.SemaphoreType.DMA((len(grads),))]

    def phases(self, src_refs, out_refs, sem_refs):
        send_sems, recv_sems = sem_refs

        def copies():
            x, y, c, _ = _place()
            out = []
            for it, (g_ref, x_ref) in enumerate(zip(src_refs, out_refs)):
                hc = g_ref.shape[1] // 2
                out.append(pltpu.make_async_remote_copy(
                    src_ref=g_ref.at[:, pl.ds((1 - c) * hc, hc)], dst_ref=x_ref, send_sem=send_sems.at[it],
                    recv_sem=recv_sems.at[it], device_id=(x, y, 1 - c), device_id_type=MESH))
            return out

        def start():
            for cp in copies():
                cp.start()

        def finish():
            for cp in copies():
                cp.wait()

        return start, lambda: None, finish


def _run_jobs(name, jobs):
    job_ops, job_out, job_sems = _job_io(jobs)
    n_src, n_out = len(job_ops), len(job_out)

    def body(*refs):
        for ph in zip(*_job_phases(jobs, refs[:n_src], refs[n_src: n_src + n_out], refs[n_src + n_out:])):
            for fn in ph:
                fn()

    return pl.pallas_call(body, name=name, in_specs=[ANY] * n_src, out_specs=[ANY] * n_out, out_shape=job_out,
                          scratch_shapes=job_sems)(*job_ops)


def _half_sum(name, g, xr, core):
    r, cc = xr.shape
    tr, tc = _tile(r, 512, 16), _tile(cc, 1024)
    nc = cc // tc

    def body(c_ref, g_ref, x_ref, s_ref):
        s_ref[...] = (g_ref[...] + x_ref[...]).astype(BF16)

    return pl.pallas_call(
        body, name=name,
        grid_spec=pltpu.PrefetchScalarGridSpec(
            num_scalar_prefetch=1, grid=(r // tr, nc),
            in_specs=[pl.BlockSpec((tr, tc), lambda i, j, c_ref: (i, c_ref[0] * nc + j)),
                      pl.BlockSpec((tr, tc), lambda i, j, c_ref: (i, j))],
            out_specs=pl.BlockSpec((tr, tc), lambda i, j, c_ref: (i, j))),
        out_shape=jax.ShapeDtypeStruct((r, cc), BF16), compiler_params=_params(("parallel", "parallel")),
    )(core, g, xr)


def _allreduce_small(name, packed):
    rows = packed.shape[0]

    def body(p_ref, o_ref, slots, send_sems, recv_sems):
        x, y, c, _ = _place()
        me = 4 * x + 2 * y + c
        slots[me] = p_ref[...]
        copies = []
        for k in range(1, 8):
            peer = (x ^ (k >> 2), y ^ ((k >> 1) & 1), c ^ (k & 1))
            cp = pltpu.make_async_remote_copy(src_ref=p_ref, dst_ref=slots.at[me], send_sem=send_sems.at[k - 1],
                                              recv_sem=recv_sems.at[k - 1], device_id=peer, device_id_type=MESH)
            cp.start()
            copies.append(cp)
        for cp in copies:
            cp.wait()
        acc = slots[0]
        for k in range(1, 8):
            acc = acc + slots[k]
        o_ref[...] = acc

    return pl.pallas_call(
        body, name=name, out_shape=jax.ShapeDtypeStruct((rows, LANES), F32),
        in_specs=[pl.BlockSpec(memory_space=pltpu.VMEM)], out_specs=pl.BlockSpec(memory_space=pltpu.VMEM),
        scratch_shapes=[pltpu.VMEM((8, rows, LANES), F32), pltpu.SemaphoreType.DMA((7,)), pltpu.SemaphoreType.DMA((7,))],
    )(packed)


def _adamw_math(w, g, m, v):
    m = ADAM_B1 * m + (1.0 - ADAM_B1) * g
    v = ADAM_B2 * v + (1.0 - ADAM_B2) * (g * g)
    m_hat = m / (1.0 - ADAM_B1 ** ADAM_STEP)
    v_hat = v / (1.0 - ADAM_B2 ** ADAM_STEP)
    delta = -ADAM_LR * (m_hat / (jnp.sqrt(v_hat) + ADAM_EPS) + ADAM_WD * w)
    return delta, m, v


def _adamw_sharded(name, rbufs, w, m, v):
    n_l, r, cc = w.shape
    rp, hc = rbufs[0].shape[-2:]
    tc = _tile(hc, 256)
    n_t = hc // tc
    wspec = pl.BlockSpec((None, r, tc), lambda l, h, i: (l, 0, h * n_t + i))

    def rspec(q):
        def imap(l, h, i):
            before, after = l < q, l > q
            return (jnp.where(before, 0, jnp.where(after, 1, h)), 0, 0,
                    jnp.where(before, 0, jnp.where(after, n_t - 1, i)))
        return pl.BlockSpec((None, N_CHIPS, rp, tc), imap)

    def body(*refs):
        r_refs = refs[:n_l]
        w_ref, m_ref, v_ref, g_out, d_out, m_out, v_out = refs[n_l:]
        rows = w_ref.shape[0]
        layer = pl.program_id(0)
        for q in range(n_l):
            @pl.when(layer == q)
            def _():
                g = r_refs[q][0, :rows, :].astype(F32)
                for k in range(1, N_CHIPS):
                    g = g + r_refs[q][k, :rows, :].astype(F32)
                g_out[...] = g

        g = g_out[...]
        delta, m_new, v_new = _adamw_math(w_ref[...], g, m_ref[...], v_ref[...])
        d_out[...] = delta
        m_out[...] = m_new
        v_out[...] = v_new

    return pl.pallas_call(
        body, name=name, grid=(n_l, 2, n_t), in_specs=[rspec(q) for q in range(n_l)] + [wspec] * 3,
        out_specs=[wspec] * 4, out_shape=[jax.ShapeDtypeStruct(w.shape, F32)] * 4,
        compiler_params=_params(("arbitrary", "arbitrary", "arbitrary")),
    )(*rbufs, w, m, v)


def _adamw_small(name, w, g, m, v):
    def body(w_ref, g_ref, m_ref, v_ref, d_out, m_out, v_out):
        delta, m_new, v_new = _adamw_math(w_ref[...], g_ref[...], m_ref[...], v_ref[...])
        d_out[...] = delta
        m_out[...] = m_new
        v_out[...] = v_new

    return pl.pallas_call(body, name=name, out_shape=[jax.ShapeDtypeStruct(w.shape, F32)] * 3)(w, g, m, v)


def _pack(arrays):
    flat = jnp.concatenate([a.reshape(-1).astype(F32) for a in arrays])
    n = flat.shape[0]
    rows = _round_up(_round_up(n, LANES) // LANES, 8)
    return jnp.pad(flat, (0, rows * LANES - n)).reshape(rows, LANES)


def _unpack(packed, like):
    flat = packed.reshape(-1)
    out, pos = [], 0
    for a in like:
        out.append(flat[pos: pos + a.size].reshape(a.shape))
        pos += a.size
    return out


_SHARDED = ("ffn1_w_gate", "ffn1_w_up", "ffn1_w_down", "ffn2_w_gate", "ffn2_w_up", "ffn2_w_down", "pool_w",
            "w_k", "w_v", "w_q", "w_o")
_SHORT = dict(ffn1_w_gate="g1", ffn1_w_up="u1", ffn1_w_down="d1", ffn2_w_gate="g2", ffn2_w_up="u2", ffn2_w_down="d2",
              pool_w="pool", w_k="wk", w_v="wv", w_q="wq", w_o="wo")
_TRANSPOSED = ("ffn1_w_gate", "ffn1_w_up", "ffn2_w_gate", "ffn2_w_up")
_SMALL = ("ffn1_norm", "mix_norm", "ffn2_norm", "kv_norm", "k_gain", "q_gain", "rel_bias", "pool_scale")
_WEIGHTS = ("ffn1_norm", "ffn1_w_gate", "ffn1_w_up", "ffn1_w_down", "mix_norm", "ffn2_norm", "ffn2_w_gate",
            "ffn2_w_up", "ffn2_w_down", "pool_w", "pool_scale", "kv_norm", "w_k", "w_v", "k_gain", "w_q", "q_gain",
            "rel_bias", "w_o")


def _as3d(name, a):
    if name in _TRANSPOSED:
        return jnp.swapaxes(a, 1, 2)
    if name == "pool_w":
        return a.reshape(a.shape[0], a.shape[1] * a.shape[2], a.shape[3])
    if a.ndim == 2:
        return a[None]
    return a


def _from3d(name, a, shape):
    return jnp.swapaxes(a, 1, 2) if name in _TRANSPOSED else a.reshape(shape)


def kernel(x, ffn1_norm, ffn1_w_gate, ffn1_w_up, ffn1_w_down, mix_norm, ffn2_norm, ffn2_w_gate, ffn2_w_up, ffn2_w_down, pool_w, pool_scale, kv_norm, w_k, w_v, k_gain, w_q, q_gain, rel_bias, w_o, loss_target, m_ffn1_norm, m_ffn1_w_gate, m_ffn1_w_up, m_ffn1_w_down, m_mix_norm, m_ffn2_norm, m_ffn2_w_gate, m_ffn2_w_up, m_ffn2_w_down, m_pool_w, m_pool_scale, m_kv_norm, m_w_k, m_w_v, m_k_gain, m_w_q, m_q_gain, m_rel_bias, m_w_o, v_ffn1_norm, v_ffn1_w_gate, v_ffn1_w_up, v_ffn1_w_down, v_mix_norm, v_ffn2_norm, v_ffn2_w_gate, v_ffn2_w_up, v_ffn2_w_down, v_pool_w, v_pool_scale, v_kv_norm, v_w_k, v_w_v, v_k_gain, v_w_q, v_q_gain, v_rel_bias, v_w_o):
    wts = dict(ffn1_norm=ffn1_norm, ffn1_w_gate=ffn1_w_gate, ffn1_w_up=ffn1_w_up, ffn1_w_down=ffn1_w_down,
               mix_norm=mix_norm, ffn2_norm=ffn2_norm, ffn2_w_gate=ffn2_w_gate, ffn2_w_up=ffn2_w_up,
               ffn2_w_down=ffn2_w_down, pool_w=pool_w, pool_scale=pool_scale, kv_norm=kv_norm, w_k=w_k, w_v=w_v,
               k_gain=k_gain, w_q=w_q, q_gain=q_gain, rel_bias=rel_bias, w_o=w_o)
    mom = dict(ffn1_norm=m_ffn1_norm, ffn1_w_gate=m_ffn1_w_gate, ffn1_w_up=m_ffn1_w_up, ffn1_w_down=m_ffn1_w_down,
               mix_norm=m_mix_norm, ffn2_norm=m_ffn2_norm, ffn2_w_gate=m_ffn2_w_gate, ffn2_w_up=m_ffn2_w_up,
               ffn2_w_down=m_ffn2_w_down, pool_w=m_pool_w, pool_scale=m_pool_scale, kv_norm=m_kv_norm, w_k=m_w_k,
               w_v=m_w_v, k_gain=m_k_gain, w_q=m_w_q, q_gain=m_q_gain, rel_bias=m_rel_bias, w_o=m_w_o)
    var = dict(ffn1_norm=v_ffn1_norm, ffn1_w_gate=v_ffn1_w_gate, ffn1_w_up=v_ffn1_w_up, ffn1_w_down=v_ffn1_w_down,
               mix_norm=v_mix_norm, ffn2_norm=v_ffn2_norm, ffn2_w_gate=v_ffn2_w_gate, ffn2_w_up=v_ffn2_w_up,
               ffn2_w_down=v_ffn2_w_down, pool_w=v_pool_w, pool_scale=v_pool_scale, kv_norm=v_kv_norm, w_k=v_w_k,
               w_v=v_w_v, k_gain=v_k_gain, w_q=v_w_q, q_gain=v_q_gain, rel_bias=v_rel_bias, w_o=v_w_o)
    chip = 2 * lax.axis_index("x") + lax.axis_index("y")
    core = lax.axis_index("c").astype(jnp.int32).reshape(1)

    sh = {}
    for name in _SHARDED:
        a = _as3d(name, wts[name])
        sh[_SHORT[name]] = _cast_pad(f"cast_{name}", a, _round_up(a.shape[1], LANES), a.shape[2])
    sh["pscale"] = _cast_pad("cast_pool_scale", pool_scale[None], 8, pool_scale.shape[1], F32)
    small = dict(n1=ffn1_norm, nm=mix_norm, n2=ffn2_norm, kvn=kv_norm, kg=k_gain, qg=q_gain, rel=rel_bias)

    loss, dx, rbufs, gs = _forward_backward(x[0], loss_target[0], sh, small, core)

    small_grads = [jnp.concatenate(gs["n1"]), jnp.concatenate(gs["nm"]), jnp.concatenate(gs["n2"]), gs["kvn"], gs["kg"],
                   jnp.concatenate(gs["qg"]), jnp.stack(gs["rel"]), jnp.concatenate(gs["pscale"]), loss[:, :1]]
    red = _unpack(_allreduce_small("allreduce_small", _pack(small_grads)), small_grads)
    g_small = dict(ffn1_norm=red[0], mix_norm=red[1], ffn2_norm=red[2], kv_norm=red[3].reshape(-1),
                   k_gain=red[4].reshape(-1), q_gain=red[5], rel_bias=red[6],
                   pool_scale=lax.dynamic_slice_in_dim(red[7], chip * pool_scale.shape[1], pool_scale.shape[1], 1))
    loss_out = red[8].reshape(())

    grad, delta, new_m, new_v = {}, {}, {}, {}
    for name in _SHARDED:
        w3 = _as3d(name, wts[name])
        outs = _adamw_sharded(f"adamw_{name}", [rbufs[(_SHORT[name], l)] for l in range(w3.shape[0])], w3,
                              _as3d(name, mom[name]), _as3d(name, var[name]))
        grad[name], delta[name], new_m[name], new_v[name] = [_from3d(name, o, wts[name].shape) for o in outs]
    sw = [wts[n] for n in _SMALL]
    sg = [g_small[n].reshape(wts[n].shape) for n in _SMALL]
    d_s, m_s, v_s = _adamw_small("adamw_small", _pack(sw), _pack(sg), _pack([mom[n] for n in _SMALL]),
                                 _pack([var[n] for n in _SMALL]))
    for n, g_n, d_n, m_n, v_n in zip(_SMALL, sg, _unpack(d_s, sw), _unpack(m_s, sw), _unpack(v_s, sw)):
        grad[n], delta[n], new_m[n], new_v[n] = g_n, d_n, m_n, v_n
    return (loss_out, dx[None], *[grad[n] for n in _WEIGHTS], *[delta[n] for n in _WEIGHTS],
            *[new_m[n] for n in _WEIGHTS], *[new_v[n] for n in _WEIGHTS])
```

```python
import functools

import jax
import jax.numpy as jnp
from jax import lax
from jax.experimental import pallas as pl
from jax.experimental.pallas import tpu as pltpu

F32 = jnp.float32
BF16 = jnp.bfloat16

EPS = 1e-6
CHUNK = 64
LEFT = 512
LEFT_CHUNKS = LEFT // CHUNK
REL_MAX = 128
POOL_WINDOWS = (2, 4, 8, 16)
HALO = 16
NEG_INF = -1e30
TQ = 256
TILES_PER_STEP = 2
BAND_W = TQ + LEFT
N_CHIPS = 4
LANES = 128
VMEM_LIMIT = 56 * 1024 * 1024

ADAM_LR = 0.001
ADAM_B1 = 0.9
ADAM_B2 = 0.999
ADAM_EPS = 1e-08
ADAM_WD = 0.01
ADAM_STEP = 10

MESH = pl.DeviceIdType.MESH
ANY = pl.BlockSpec(memory_space=pl.ANY)


def _round_up(n, m):
    return (n + m - 1) // m * m


def _tile(n, pref, unit=LANES):
    if n <= pref:
        return n
    t = pref // unit * unit
    while t >= unit:
        if n % t == 0:
            return t
        t -= unit
    return n


def _params(sem):
    return pltpu.CompilerParams(dimension_semantics=sem, vmem_limit_bytes=VMEM_LIMIT)


def _bs(block, imap, lead=None):
    if lead is None:
        return pl.BlockSpec(tuple(block), imap)
    return pl.BlockSpec((None,) + tuple(block), lambda *g: (lead,) + tuple(imap(*g)))


_DIMS = {
    "nn": (((1,), (0,)), ((), ())),
    "nt": (((1,), (1,)), ((), ())),
    "tn": (((0,), (0,)), ((), ())),
}


def _job_io(jobs):
    operands = [s for jb in jobs for s in jb.srcs]
    out_shape = [jax.ShapeDtypeStruct(s, dt) for jb in jobs for s, dt in jb.out_shapes]
    scratch = [s for jb in jobs for s in jb.sems]
    return operands, out_shape, scratch


def _job_phases(jobs, src_refs, out_refs, sem_refs):
    phases, a, b, c = [], 0, 0, 0
    for jb in jobs:
        ns, no, nm = len(jb.srcs), len(jb.out_shapes), len(jb.sems)
        phases.append(jb.phases(src_refs[a:a + ns], out_refs[b:b + no], sem_refs[c:c + nm]))
        a, b, c = a + ns, b + no, c + nm
    return phases


def _run_phases(phases, step, total):
    for n, at in enumerate((0, max(total - 1 - max(1, total // 8), 0), total - 1)):
        @pl.when(step == at)
        def _():
            for ph in phases:
                ph[n]()


def _mm(name, pairs, m, n, k, mode, outs, epilogue, extras=(), tm=512, tn=512, tk=512, jobs=()):
    tm, tn, tk = _tile(m, tm, LANES if mode == "tn" else 8), _tile(n, tn), _tile(k, tk)
    ni, nj, nk = m // tm, n // tn, k // tk
    n_pairs, n_extra, n_out = len(pairs), len(extras), len(outs)
    in_specs, operands = [], []
    for (a, la, oa), (b, lb, ob) in pairs:
        if mode == "tn":
            in_specs.append(_bs((tk, tm), lambda i, j, kk, o=oa: (kk + o, i), la))
        else:
            in_specs.append(_bs((tm, tk), lambda i, j, kk, o=oa: (i + o, kk), la))
        if mode == "nt":
            in_specs.append(_bs((tn, tk), lambda i, j, kk, o=ob: (j + o, kk), lb))
        else:
            in_specs.append(_bs((tk, tn), lambda i, j, kk, o=ob: (kk + o, j), lb))
        operands += [a, b]
    for e, le, kind in extras:
        if kind == "mn":
            in_specs.append(_bs((tm, tn), lambda i, j, kk: (i, j), le))
        else:
            in_specs.append(_bs((1, tn), lambda i, j, kk: (0, j), le))
        operands.append(e)
    out_specs, out_shape, flipped = [], [], []
    for o in outs:
        rows, dt, off = o[:3]
        flipped.append(len(o) > 3 and o[3])
        if flipped[-1]:
            out_specs.append(pl.BlockSpec((tn, tm), lambda i, j, kk: (j, i)))
            out_shape.append(jax.ShapeDtypeStruct((n, rows), dt))
        else:
            out_specs.append(pl.BlockSpec((tm, tn), lambda i, j, kk, o=off: (i + o, j)))
            out_shape.append(jax.ShapeDtypeStruct((rows, n), dt))
    dims = _DIMS[mode]
    job_ops, job_out, job_sems = _job_io(jobs)
    n_in = len(operands)

    def body(*refs):
        ab = refs[: 2 * n_pairs]
        ex = refs[2 * n_pairs: n_in]
        job_src = refs[n_in: n_in + len(job_ops)]
        out_refs = refs[n_in + len(job_ops): n_in + len(job_ops) + n_out]
        job_dst = refs[n_in + len(job_ops) + n_out: n_in + len(job_ops) + n_out + len(job_out)]
        kk = pl.program_id(2)
        if jobs:
            step = (pl.program_id(0) * nj + pl.program_id(1)) * nk + kk
            _run_phases(_job_phases(jobs, job_src, job_dst, refs[len(refs) - len(job_sems):]), step, ni * nj * nk)

        def partial():
            tot = None
            for p in range(n_pairs):
                a = ab[2 * p][...].astype(BF16)
                b = ab[2 * p + 1][...].astype(BF16)
                prod = lax.dot_general(a, b, dims, preferred_element_type=F32)
                tot = prod if tot is None else tot + prod
            return tot

        def finish(total):
            res = epilogue(total, *[e[...] for e in ex])
            for o_ref, o, flip in zip(out_refs, res, flipped):
                o_ref[...] = (o.T if flip else o).astype(o_ref.dtype)

        if nk == 1:
            finish(partial())
            return
        acc = refs[n_in + len(job_ops) + n_out + len(job_out)]

        @pl.when(kk == 0)
        def _():
            acc[...] = jnp.zeros_like(acc)

        acc[...] += partial()

        @pl.when(kk == nk - 1)
        def _():
            finish(acc[...])

    sem = ("arbitrary",) * 3 if jobs else ("parallel", "parallel", "arbitrary")
    res = pl.pallas_call(
        body, name=name, grid=(ni, nj, nk), in_specs=in_specs + [ANY] * len(job_ops),
        out_specs=out_specs + [ANY] * len(job_out), out_shape=out_shape + job_out,
        scratch_shapes=([pltpu.VMEM((tm, tn), F32)] if nk > 1 else []) + job_sems, compiler_params=_params(sem),
    )(*operands, *job_ops)
    return (res[:n_out], res[n_out:]) if jobs else res


def _op(a, lead=None, off=0):
    return (a, lead, off)


def _head_rms(blk, gain):
    r = lax.rsqrt(jnp.mean(blk * blk, axis=-1, keepdims=True) + EPS)
    return blk * r * gain, r


def _rms_fwd(name, x, gain_row):
    t, d = x.shape
    tm = _tile(t, 512)

    def body(x_ref, g_ref, h_ref, ht_ref):
        xv = x_ref[...]
        r = lax.rsqrt(jnp.mean(xv * xv, axis=-1, keepdims=True) + EPS)
        h = (xv * r * g_ref[...]).astype(BF16)
        h_ref[...] = h
        ht_ref[...] = h.T

    return pl.pallas_call(
        body, name=name, grid=(t // tm,),
        in_specs=[pl.BlockSpec((tm, d), lambda i: (i, 0)), pl.BlockSpec((1, d), lambda i: (0, 0))],
        out_specs=[pl.BlockSpec((tm, d), lambda i: (i, 0)), pl.BlockSpec((d, tm), lambda i: (0, i))],
        out_shape=[jax.ShapeDtypeStruct((t, d), BF16), jax.ShapeDtypeStruct((d, t), BF16)],
        compiler_params=_params(("parallel",)),
    )(x, gain_row)


def _rms_bwd_math(dh, xv, g):
    r = lax.rsqrt(jnp.mean(xv * xv, axis=-1, keepdims=True) + EPS)
    u = dh * g
    dx = r * u - xv * (r * r * r * jnp.mean(u * xv, axis=-1, keepdims=True))
    dg = jnp.sum(dh * xv * r, axis=0, keepdims=True)
    return dx, dg


def _rms_bwd(name, dh, x, gain_row, dres=None):
    t, d = x.shape
    tm = _tile(t, 256, 8)
    has_res = dres is not None

    def body(*refs):
        if has_res:
            dh_ref, x_ref, g_ref, dres_ref, dx_ref, dxb_ref, dg_ref = refs
        else:
            dh_ref, x_ref, g_ref, dx_ref, dxb_ref, dg_ref = refs
        dx, dg = _rms_bwd_math(dh_ref[...], x_ref[...], g_ref[...])
        if has_res:
            dx = dx + dres_ref[...]
        dx_ref[...] = dx
        dxb_ref[...] = dx.astype(BF16)

        @pl.when(pl.program_id(0) == 0)
        def _():
            dg_ref[...] = jnp.zeros_like(dg_ref)

        dg_ref[...] += dg

    row = pl.BlockSpec((tm, d), lambda i: (i, 0))
    vec = pl.BlockSpec((1, d), lambda i: (0, 0))
    return pl.pallas_call(
        body, name=name, grid=(t // tm,),
        in_specs=[row, row, vec] + ([row] if has_res else []),
        out_specs=[row, row, vec],
        out_shape=[jax.ShapeDtypeStruct((t, d), F32), jax.ShapeDtypeStruct((t, d), BF16),
                   jax.ShapeDtypeStruct((1, d), F32)],
        compiler_params=_params(("arbitrary",)),
    )(*([dh, x, gain_row] + ([dres] if has_res else [])))


def _ffn_up(name, h, wgt, wut, jobs=()):
    t, d = h.shape
    f = wgt.shape[0]
    tm, tn = _tile(t, 1024), _tile(f, 512)
    ni, nj = t // tm, f // tn
    job_ops, job_out, job_sems = _job_io(jobs)

    def body(*refs):
        h_ref, wg_ref, wu_ref = refs[:3]
        job_src = refs[3: 3 + len(job_ops)]
        g_ref, u_ref, a_ref, at_ref = refs[3 + len(job_ops): 7 + len(job_ops)]
        job_dst = refs[7 + len(job_ops): 7 + len(job_ops) + len(job_out)]
        if jobs:
            step = pl.program_id(0) * nj + pl.program_id(1)
            _run_phases(_job_phases(jobs, job_src, job_dst, refs[len(refs) - len(job_sems):]), step, ni * nj)
        hv = h_ref[...]
        g = lax.dot_general(hv, wg_ref[...], _DIMS["nt"], preferred_element_type=F32)
        u = lax.dot_general(hv, wu_ref[...], _DIMS["nt"], preferred_element_type=F32)
        sig = jax.nn.sigmoid(g)
        silu = g * sig
        g_ref[...] = silu.astype(BF16)
        u_ref[...] = (u * (sig * (1.0 + g * (1.0 - sig)))).astype(BF16)
        a = (silu * u).astype(BF16)
        a_ref[...] = a
        at_ref[...] = a.T

    wspec = pl.BlockSpec((tn, d), lambda i, j: (j, 0))
    ospec = pl.BlockSpec((tm, tn), lambda i, j: (i, j))
    res = pl.pallas_call(
        body, name=name, grid=(ni, nj),
        in_specs=[pl.BlockSpec((tm, d), lambda i, j: (i, 0)), wspec, wspec] + [ANY] * len(job_ops),
        out_specs=[ospec] * 3 + [pl.BlockSpec((tn, tm), lambda i, j: (j, i))] + [ANY] * len(job_out),
        out_shape=[jax.ShapeDtypeStruct((t, f), BF16)] * 3 + [jax.ShapeDtypeStruct((f, t), BF16)] + job_out,
        scratch_shapes=job_sems,
        compiler_params=_params(("arbitrary", "arbitrary") if jobs else ("parallel", "parallel")),
    )(h, wgt, wut, *job_ops)
    return res[:4], res[4:]


def _ffn_bwd(tag, dout, doutb, saved, gain_row, wgt, wut, wd, jobs, swap_extra):
    x, ht, g, u, at = saved
    t, d = x.shape
    f = wd.shape[0]
    jobs = dict(jobs)
    got = {}

    def mm(key, *args, **kw):
        jb = jobs.get(key, ())
        res = _mm(f"{tag}_bwd_{key}", *args, jobs=jb, **kw)
        if jb:
            res, got[key] = res
        return res

    def act_bwd(acc, sv, vv):
        da = 0.5 * acc
        return da * vv.astype(F32), da * sv.astype(F32)

    ident = lambda acc: (acc,)
    dg, du = mm("act", [(_op(doutb), _op(wd))], t, f, d, "nt", [(t, BF16, 0), (t, BF16, 0)],
                act_bwd, extras=[(g, None, "mn"), (u, None, "mn")], tm=1024, tn=512, tk=2048)
    (dwd,) = mm("wd", [(_op(at), _op(doutb))], f, d, t, "nn", [(f, F32, 0)], lambda acc: (0.5 * acc,),
                tm=1408, tn=1024, tk=2048)
    (dwgt,) = mm("wg", [(_op(ht), _op(dg))], d, f, t, "nn", [(d, F32, 0, True)], ident, tm=1024, tn=1408, tk=2048)
    (dwut,) = mm("wu", [(_op(ht), _op(du))], d, f, t, "nn", [(d, F32, 0, True)], ident, tm=1024, tn=1408, tk=2048)
    jobs["h"] = [_SwapJob(list(swap_extra) + [dwgt, dwut, dwd])]
    (dh,) = mm("h", [(_op(dg), _op(wgt)), (_op(du), _op(wut))], t, d, f, "nn", [(t, F32, 0)], ident,
               tm=1024, tn=1024, tk=1408)
    dx, dxb, dgain = _rms_bwd(f"{tag}_bwd_norm", dh, x, gain_row, dout)
    return dx, dxb, dgain, [dwgt, dwut, dwd], got


def _inv_count(t0, rows, window):
    tpos = t0 + lax.broadcasted_iota(jnp.int32, (rows, 1), 0)
    return 1.0 / jnp.minimum(tpos + 1, window).astype(F32)


def _pool_fwd(name, x, gain_row, wp, scale_row, layer):
    t, d = x.shape
    n_g = len(POOL_WINDOWS)
    p = d // n_g
    tm = _tile(t, 256, 8)
    hb = tm // HALO

    def body(x_ref, xp_ref, g_ref, wp_ref, s_ref, out_ref, diff_ref, ypre_ref):
        i = pl.program_id(0)
        gain = g_ref[...]

        def norm(v):
            return v * lax.rsqrt(jnp.mean(v * v, axis=-1, keepdims=True) + EPS) * gain

        xv = x_ref[...]
        h = norm(xv)
        hp = jnp.where(i > 0, norm(xp_ref[...]), 0.0)
        ext = jnp.concatenate([hp, h], axis=0)
        for gi, w in enumerate(POOL_WINDOWS):
            cols = slice(gi * p, (gi + 1) * p)
            s = ext[:, cols]
            k = 1
            while k < w:
                s = s + pltpu.roll(s, k, 0)
                k *= 2
            pooled = s[HALO:, :] * _inv_count(i * tm, tm, w)
            diff = (pooled - h[:, cols]).astype(BF16)
            y = jnp.dot(diff, wp_ref[gi], preferred_element_type=F32)
            diff_ref[:, cols] = diff
            ypre_ref[:, cols] = y.astype(BF16)
            out_ref[:, cols] = xv[:, cols] + y * s_ref[:, cols]

    row = pl.BlockSpec((tm, d), lambda i: (i, 0))
    vec = pl.BlockSpec((1, d), lambda i: (0, 0))
    return pl.pallas_call(
        body, name=name, grid=(t // tm,),
        in_specs=[row, pl.BlockSpec((HALO, d), lambda i: (jnp.maximum(i * hb - 1, 0), 0)), vec,
                  _bs((n_g, p, p), lambda i: (0, 0, 0), layer), vec],
        out_specs=[row, row, row],
        out_shape=[jax.ShapeDtypeStruct((t, d), F32), jax.ShapeDtypeStruct((t, d), BF16),
                   jax.ShapeDtypeStruct((t, d), BF16)],
        compiler_params=_params(("parallel",)),
    )(x, x, gain_row, wp, scale_row)


def _pool_bwd_mat(name, dout, diff, ypre, wp, scale_row, layer):
    t, d = dout.shape
    n_g = len(POOL_WINDOWS)
    p = d // n_g
    tm = _tile(t, 256, 8)

    def body(do_ref, diff_ref, ypre_ref, wp_ref, s_ref, dd_ref, dwp_ref, ds_ref):
        @pl.when(pl.program_id(0) == 0)
        def _():
            dwp_ref[...] = jnp.zeros_like(dwp_ref)
            ds_ref[...] = jnp.zeros_like(ds_ref)

        dov = do_ref[...]
        ds_ref[...] += jnp.sum(dov * ypre_ref[...].astype(F32), axis=0, keepdims=True)
        dy = (dov * s_ref[...]).astype(BF16)
        for gi in range(n_g):
            cols = slice(gi * p, (gi + 1) * p)
            dwp_ref[gi] += lax.dot_general(diff_ref[:, cols], dy[:, cols], _DIMS["tn"], preferred_element_type=F32)
            dd_ref[:, cols] = lax.dot_general(dy[:, cols], wp_ref[gi], _DIMS["nt"], preferred_element_type=F32)

    row = pl.BlockSpec((tm, d), lambda i: (i, 0))
    vec = pl.BlockSpec((1, d), lambda i: (0, 0))
    return pl.pallas_call(
        body, name=name, grid=(t // tm,),
        in_specs=[row, row, row, _bs((n_g, p, p), lambda i: (0, 0, 0), layer), vec],
        out_specs=[row, pl.BlockSpec((n_g, p, p), lambda i: (0, 0, 0)), vec],
        out_shape=[jax.ShapeDtypeStruct((t, d), F32), jax.ShapeDtypeStruct((n_g, p, p), F32),
                   jax.ShapeDtypeStruct((1, d), F32)],
        compiler_params=_params(("arbitrary",)),
    )(dout, diff, ypre, wp, scale_row)


def _pool_bwd_win(name, ddiff, x, gain_row, dout):
    t, d = x.shape
    n_g = len(POOL_WINDOWS)
    p = d // n_g
    tm = _tile(t, 256, 8)
    hb = tm // HALO
    n_t = t // tm
    rows = tm + HALO

    def body(e_ref, en_ref, x_ref, g_ref, do_ref, dx_ref, dxb_ref, dg_ref):
        i = pl.program_id(0)
        e = e_ref[...]
        en = jnp.where(i < n_t - 1, en_ref[...], 0.0)
        ext = jnp.concatenate([e, en], axis=0)
        parts = []
        for gi, w in enumerate(POOL_WINDOWS):
            cols = slice(gi * p, (gi + 1) * p)
            s = ext[:, cols] * _inv_count(i * tm, rows, w)
            k = 1
            while k < w:
                s = s + pltpu.roll(s, rows - k, 0)
                k *= 2
            parts.append(s[:tm, :] - e[:, cols])
        dh = jnp.concatenate(parts, axis=1)
        dx, dg = _rms_bwd_math(dh, x_ref[...], g_ref[...])
        dx = dx + do_ref[...]
        dx_ref[...] = dx
        dxb_ref[...] = dx.astype(BF16)

        @pl.when(i == 0)
        def _():
            dg_ref[...] = jnp.zeros_like(dg_ref)

        dg_ref[...] += dg

    row = pl.BlockSpec((tm, d), lambda i: (i, 0))
    vec = pl.BlockSpec((1, d), lambda i: (0, 0))
    return pl.pallas_call(
        body, name=name, grid=(n_t,),
        in_specs=[row, pl.BlockSpec((HALO, d), lambda i: (jnp.minimum((i + 1) * hb, t // HALO - 1), 0)), row, vec, row],
        out_specs=[row, row, vec],
        out_shape=[jax.ShapeDtypeStruct((t, d), F32), jax.ShapeDtypeStruct((t, d), BF16),
                   jax.ShapeDtypeStruct((1, d), F32)],
        compiler_params=_params(("arbitrary",)),
    )(ddiff, ddiff, x, gain_row, dout)


def _rel_index(n_rel):
    dd = lax.broadcasted_iota(jnp.int32, (n_rel, BAND_W), 1)
    e = lax.broadcasted_iota(jnp.int32, (n_rel, BAND_W), 0)
    idx = jnp.where(dd < LEFT + 2 * CHUNK, jnp.clip(LEFT - dd, -(CHUNK - 1), REL_MAX) + (CHUNK - 1), n_rel - 1)
    return (idx == e).astype(F32)


def _bias_rows(name, table):
    n_h, n_rel = table.shape

    def body(t_ref, o_ref):
        o_ref[...] = jnp.dot(t_ref[...], _rel_index(n_rel), preferred_element_type=F32,
                             precision=lax.Precision.HIGHEST)

    rows = pl.pallas_call(body, name=name, out_shape=jax.ShapeDtypeStruct((n_h, BAND_W), F32))(table)
    return rows.reshape(n_h, 1, BAND_W)


def _bias_rows_bwd(name, dbias, n_rel):
    n_h = dbias.shape[0]

    def body(db_ref, o_ref):
        rr = lax.broadcasted_iota(jnp.int32, (TQ, TQ), 0)
        cc = lax.broadcasted_iota(jnp.int32, (TQ, TQ), 1)
        flipped = jnp.dot((rr + cc == TQ - 1).astype(F32), db_ref[...], preferred_element_type=F32,
                          precision=lax.Precision.HIGHEST)
        unskew = pltpu.roll(flipped, BAND_W - TQ + 1, 1, stride=1, stride_axis=0)
        drow = jnp.sum(unskew, axis=0, keepdims=True)
        o_ref[...] = lax.dot_general(drow, _rel_index(n_rel), _DIMS["nt"], preferred_element_type=F32,
                                     precision=lax.Precision.HIGHEST)

    return pl.pallas_call(
        body, name=name, grid=(n_h,),
        in_specs=[pl.BlockSpec((None, TQ, BAND_W), lambda h: (h, 0, 0))],
        out_specs=pl.BlockSpec((None, 1, n_rel), lambda h: (h, 0, 0)),
        out_shape=jax.ShapeDtypeStruct((n_h, 1, n_rel), F32), compiler_params=_params(("parallel",)),
    )(dbias)


def _band_bias(row):
    tile = pltpu.roll(jnp.broadcast_to(row, (TQ, BAND_W)), 0, 1, stride=1, stride_axis=0)
    qc = lax.broadcasted_iota(jnp.int32, (TQ, BAND_W), 0) // CHUNK
    kc = lax.broadcasted_iota(jnp.int32, (TQ, BAND_W), 1) // CHUNK
    return jnp.where((kc >= qc) & (kc <= qc + LEFT_CHUNKS), tile, NEG_INF)


def _band_probs(qv, kb, bias, t0):
    s = lax.dot_general(qv, kb, _DIMS["nt"], preferred_element_type=F32) * (qv.shape[-1] ** -0.5) + bias
    col = lax.broadcasted_iota(jnp.int32, (TQ, BAND_W), 1)
    s = jnp.where(col >= LEFT - t0, s, NEG_INF)
    e = jnp.exp(s - jnp.max(s, axis=-1, keepdims=True))
    return e / jnp.sum(e, axis=-1, keepdims=True)


def _valid_rows(blk, t0):
    r = lax.broadcasted_iota(jnp.int32, (BAND_W, 1), 0)
    return jnp.where(r >= LEFT - t0, blk, jnp.zeros_like(blk))


def _attn_fwd(name, qr, gain_row, kpad, vpad, brow):
    t, d = qr.shape
    hd = gain_row.shape[-1]
    n_h = d // hd

    def body(q_ref, g_ref, k_ref, v_ref, b_ref, o_ref, ot_ref, bias_ref):
        i = pl.program_id(1)

        @pl.when(i == 0)
        def _():
            bias_ref[...] = _band_bias(b_ref[...])

        for sub in range(TILES_PER_STEP):
            rows = slice(sub * TQ, (sub + 1) * TQ)
            t0 = pl.multiple_of((i * TILES_PER_STEP + sub) * TQ, TQ)
            qv = _head_rms(q_ref[rows, :].astype(F32), g_ref[...])[0].astype(BF16)
            kb = _valid_rows(k_ref[pl.ds(t0, BAND_W), :], t0)
            vb = _valid_rows(v_ref[pl.ds(t0, BAND_W), :], t0)
            pr = _band_probs(qv, kb, bias_ref[...], t0)
            o = jnp.dot(pr.astype(BF16), vb, preferred_element_type=F32).astype(BF16)
            o_ref[rows, :] = o
            ot_ref[:, rows] = o.T

    tq = TILES_PER_STEP * TQ
    kv_spec = pl.BlockSpec((LEFT + t, hd), lambda h, i: (0, h))
    return pl.pallas_call(
        body, name=name, grid=(n_h, t // tq),
        in_specs=[pl.BlockSpec((tq, hd), lambda h, i: (i, h)), pl.BlockSpec((1, hd), lambda h, i: (0, 0)),
                  kv_spec, kv_spec, pl.BlockSpec((None, 1, BAND_W), lambda h, i: (h, 0, 0))],
        out_specs=[pl.BlockSpec((tq, hd), lambda h, i: (i, h)), pl.BlockSpec((hd, tq), lambda h, i: (h, i))],
        out_shape=[jax.ShapeDtypeStruct((t, d), BF16), jax.ShapeDtypeStruct((d, t), BF16)],
        scratch_shapes=[pltpu.VMEM((TQ, BAND_W), F32)],
        compiler_params=_params(("parallel", "arbitrary")),
    )(qr, gain_row, kpad, vpad, brow)


def _attn_bwd(name, qr, gain_row, kpad, vpad, brow, o, do, dk_in=None, dv_in=None):
    t, d = qr.shape
    hd = gain_row.shape[-1]
    n_h = d // hd
    scale = hd ** -0.5
    has_init = dk_in is not None

    def body(*refs):
        if has_init:
            (q_ref, g_ref, k_ref, v_ref, b_ref, o_ref, do_ref, dki_ref, dvi_ref,
             dq_ref, dk_ref, dv_ref, db_ref, dg_ref, bias_ref) = refs
        else:
            (q_ref, g_ref, k_ref, v_ref, b_ref, o_ref, do_ref,
             dq_ref, dk_ref, dv_ref, db_ref, dg_ref, bias_ref) = refs
        h = pl.program_id(0)
        i = pl.program_id(1)

        @pl.when(i == 0)
        def _():
            bias_ref[...] = _band_bias(b_ref[...])
            db_ref[...] = jnp.zeros_like(db_ref)
            if has_init:
                dk_ref[...] = dki_ref[...]
                dv_ref[...] = dvi_ref[...]
            else:
                dk_ref[...] = jnp.zeros_like(dk_ref)
                dv_ref[...] = jnp.zeros_like(dv_ref)

        @pl.when((i == 0) & (h == 0))
        def _():
            dg_ref[...] = jnp.zeros_like(dg_ref)

        gain = g_ref[...]
        parts = []
        for sub in range(TILES_PER_STEP):
            rows = slice(sub * TQ, (sub + 1) * TQ)
            t0 = pl.multiple_of((i * TILES_PER_STEP + sub) * TQ, TQ)
            qraw = q_ref[rows, :].astype(F32)
            qn, r = _head_rms(qraw, gain)
            qv = qn.astype(BF16)
            kb = _valid_rows(k_ref[pl.ds(t0, BAND_W), :], t0)
            vb = _valid_rows(v_ref[pl.ds(t0, BAND_W), :], t0)
            pr = _band_probs(qv, kb, bias_ref[...], t0)
            dov = do_ref[rows, :]
            delta = jnp.sum(dov.astype(F32) * o_ref[rows, :].astype(F32), axis=-1, keepdims=True)
            dp = lax.dot_general(dov, vb, _DIMS["nt"], preferred_element_type=F32)
            ds = pr * (dp - delta)
            dsb = ds.astype(BF16)
            dq = jnp.dot(dsb, kb, preferred_element_type=F32) * scale
            dkb = lax.dot_general(dsb, qv, _DIMS["tn"], preferred_element_type=F32) * scale
            dvb = lax.dot_general(pr.astype(BF16), dov, _DIMS["tn"], preferred_element_type=F32)
            u = dq * gain
            dq_ref[rows, :] = (r * u - qraw * (r * r * r * jnp.mean(u * qraw, axis=-1, keepdims=True))).astype(BF16)
            parts.append((t0, ds, dkb, dvb, jnp.sum(dq * qraw * r, axis=0, keepdims=True)))
        for t0, ds, dkb, dvb, dgain in parts:
            db_ref[...] += ds
            dk_ref[pl.ds(t0, BAND_W), :] += dkb
            dv_ref[pl.ds(t0, BAND_W), :] += dvb
            dg_ref[...] += dgain

    qspec = pl.BlockSpec((TILES_PER_STEP * TQ, hd), lambda h, i: (i, h))
    kv_spec = pl.BlockSpec((LEFT + t, hd), lambda h, i: (0, h))
    gspec = pl.BlockSpec((1, hd), lambda h, i: (0, 0))
    ins = [qr, gain_row, kpad, vpad, brow, o, do] + ([dk_in, dv_in] if has_init else [])
    return pl.pallas_call(
        body, name=name, grid=(n_h, t // (TILES_PER_STEP * TQ)),
        in_specs=[qspec, gspec, kv_spec, kv_spec, pl.BlockSpec((None, 1, BAND_W), lambda h, i: (h, 0, 0)), qspec, qspec]
        + ([kv_spec, kv_spec] if has_init else []),
        out_specs=[qspec, kv_spec, kv_spec, pl.BlockSpec((None, TQ, BAND_W), lambda h, i: (h, 0, 0)), gspec],
        out_shape=[jax.ShapeDtypeStruct((t, d), BF16), jax.ShapeDtypeStruct((LEFT + t, d), F32),
                   jax.ShapeDtypeStruct((LEFT + t, d), F32), jax.ShapeDtypeStruct((n_h, TQ, BAND_W), F32),
                   jax.ShapeDtypeStruct((1, hd), F32)],
        scratch_shapes=[pltpu.VMEM((TQ, BAND_W), F32)],
        compiler_params=_params(("arbitrary", "arbitrary")),
    )(*ins)


def _head_norm_bwd(name, dy, off_rows, yraw, gain_row):
    t, d = yraw.shape
    hd = gain_row.shape[-1]
    tm = _tile(t, 512, 8)
    off = off_rows // tm

    def body(dy_ref, y_ref, g_ref, dx_ref, dg_ref):
        @pl.when((pl.program_id(0) == 0) & (pl.program_id(1) == 0))
        def _():
            dg_ref[...] = jnp.zeros_like(dg_ref)

        gain = g_ref[...]
        yv = y_ref[...].astype(F32)
        dyv = dy_ref[...]
        r = lax.rsqrt(jnp.mean(yv * yv, axis=-1, keepdims=True) + EPS)
        u = dyv * gain
        dx_ref[...] = (r * u - yv * (r * r * r * jnp.mean(u * yv, axis=-1, keepdims=True))).astype(BF16)
        dg_ref[...] += jnp.sum(dyv * yv * r, axis=0, keepdims=True)

    blk = pl.BlockSpec((tm, hd), lambda i, h: (i, h))
    gspec = pl.BlockSpec((1, hd), lambda i, h: (0, 0))
    return pl.pallas_call(
        body, name=name, grid=(t // tm, d // hd),
        in_specs=[pl.BlockSpec((tm, hd), lambda i, h: (i + off, h)), blk, gspec],
        out_specs=[blk, gspec],
        out_shape=[jax.ShapeDtypeStruct((t, d), BF16), jax.ShapeDtypeStruct((1, hd), F32)],
        compiler_params=_params(("arbitrary", "arbitrary")),
    )(dy, yraw, gain_row)


def _head_norm_epilogue(hd):
    def epi(acc, gain):
        parts = [_head_rms(acc[:, c:c + hd], gain[:, c:c + hd])[0] for c in range(0, acc.shape[-1], hd)]
        return acc, jnp.concatenate(parts, axis=1)
    return epi


def _loss_head(name, y, target):
    t, d = y.shape
    tm = _tile(t, 512, 8)

    def body(y_ref, t_ref, dy_ref, dyb_ref, l_ref):
        @pl.when(pl.program_id(0) == 0)
        def _():
            l_ref[...] = jnp.zeros_like(l_ref)

        err = y_ref[...] - t_ref[...]
        dy = err * (1.0 / d)
        dy_ref[...] = dy
        dyb_ref[...] = dy.astype(BF16)
        per_tok = jnp.mean(err * err, axis=-1, keepdims=True)
        l_ref[...] += 0.5 * jnp.sum(per_tok, axis=0, keepdims=True)

    row = pl.BlockSpec((tm, d), lambda i: (i, 0))
    return pl.pallas_call(
        body, name=name, grid=(t // tm,), in_specs=[row, row],
        out_specs=[row, row, pl.BlockSpec((1, LANES), lambda i: (0, 0))],
        out_shape=[jax.ShapeDtypeStruct((t, d), F32), jax.ShapeDtypeStruct((t, d), BF16),
                   jax.ShapeDtypeStruct((1, LANES), F32)],
        compiler_params=_params(("arbitrary",)),
    )(y, target)


def _forward_backward(x, target, sh, small, core):
    t, d = x.shape
    n_layers = small["n1"].shape[0]
    n_a = sh["pool"].shape[0]
    n_b = sh["wq"].shape[0]
    hd = small["kg"].shape[0]
    n_rel = small["rel"].shape[-1]
    n_g = len(POOL_WINDOWS)
    p = d // n_g
    row = lambda v: v.reshape(1, -1)
    kg_row = row(small["kg"])
    big = dict(tm=1024, tn=1024)
    ident = lambda acc: (acc,)

    stages = [(l, s) for l in range(n_layers) for s in (1, 2)]
    ffn_keys = lambda k, names: [(f"{nm}{stages[k][1]}", stages[k][0]) for nm in names]
    up_keys = {k: ffn_keys(k + 1, "gu") for k in range(len(stages) - 1)}
    down_keys = {k: ffn_keys(k + 1, "d") for k in range(len(stages) - 1)}
    up_keys[0] = ffn_keys(0, "d") + up_keys[0]
    up_keys[2 * n_a - 2] = up_keys[2 * n_a - 2] + [("wk", 0), ("wv", 0)]
    for b in range(n_b):
        up_keys[2 * (n_a + b) - 1] = up_keys[2 * (n_a + b) - 1] + [("wq", b), ("wo", b)]
    first_keys = ffn_keys(0, "gu") + [("pool", l) for l in range(n_a)] + [("pscale", 0)]
    full = {}

    def gather(keys):
        return [_gather_job([(sh[nm], l) for nm, l in keys])] if keys else []

    full.update(zip(first_keys, _run_jobs("gather_first", gather(first_keys))))
    pscale = full[("pscale", 0)].reshape(N_CHIPS, 8, p)[:, :n_a].transpose(1, 0, 2).reshape(n_a, d)
    pool_full = lambda l: (full[("pool", l)].reshape(N_CHIPS, n_g, p // N_CHIPS, p).transpose(1, 0, 2, 3)
                           .reshape(n_g, p, p))

    saved, mixed = [], []
    kv_saved = None
    kpad = vpad = None
    for k, (l, s) in enumerate(stages):
        if s == 2:
            x_in = x
            if l < n_a:
                wp = pool_full(l)
                x, diff, ypre = _pool_fwd(f"l{l}_pool", x, row(small["nm"][l]), wp, row(pscale[l]), None)
                mixed.append((x_in, diff, ypre, wp))
            else:
                b = l - n_a
                h, ht = _rms_fwd(f"l{l}_mix_norm", x, row(small["nm"][l]))
                (qr,) = _mm(f"l{l}_q", [(_op(h), _op(full[("wq", b)]))], t, d, d, "nn", [(t, BF16, 0)], ident,
                            tk=2048, **big)
                brow = _bias_rows(f"l{l}_bias", small["rel"][b])
                o, ot = _attn_fwd(f"l{l}_attn", qr, row(small["qg"][b]), kpad, vpad, brow)
                (x,) = _mm(f"l{l}_o", [(_op(o), _op(full[("wo", b)]))], t, d, d, "nn", [(t, F32, 0)],
                           lambda acc, xv: (xv + acc,), extras=[(x_in, None, "mn")], tk=2048, **big)
                mixed.append((x_in, ht, qr, brow, o, ot))
        tag = f"l{l}_ffn{s}"
        h, ht = _rms_fwd(f"{tag}_norm", x, row(small[f"n{s}"][l]))
        (g, u, a, at), got = _ffn_up(f"{tag}_up", h, full[(f"g{s}", l)], full[(f"u{s}", l)], gather(up_keys.get(k, [])))
        full.update(zip(up_keys.get(k, []), got))
        jobs = gather(down_keys.get(k, []))
        res = _mm(f"{tag}_down", [(_op(a), _op(full[(f"d{s}", l)]))], t, d, a.shape[1], "nn", [(t, F32, 0)],
                  lambda acc, xv: (xv + 0.5 * acc,), extras=[(x, None, "mn")], tm=1024, tn=1024, tk=2816, jobs=jobs)
        (x_new,), got = res if jobs else (res, [])
        full.update(zip(down_keys.get(k, []), got))
        saved.append((x, ht, g, u, at))
        x = x_new
        if k == 2 * n_a - 1:
            hk, hkt = _rms_fwd("kv_norm", x, row(small["kvn"]))
            pad_blocks = LEFT // _tile(t, 512, 8)
            kr, kpad = _mm("kv_k", [(_op(hk), _op(full[("wk", 0)]))], t, d, d, "nn",
                           [(t, BF16, 0), (LEFT + t, BF16, pad_blocks)],
                           _head_norm_epilogue(hd), extras=[(jnp.tile(kg_row, (1, d // hd)), None, "row")],
                           tn=1024, tk=2048)
            (vpad,) = _mm("kv_v", [(_op(hk), _op(full[("wv", 0)]))], t, d, d, "nn", [(LEFT + t, BF16, pad_blocks)],
                          ident, tn=1024, tk=2048)
            kv_saved = (x, hkt, kr)

    dx, dxb, loss = _loss_head("loss_head", x, target)

    gs = {nm: [None] * n_layers for nm in ("n1", "nm", "n2")}
    gs.update(qg=[None] * n_b, rel=[None] * n_b, pscale=[None] * n_a)
    grads, sums, rbufs = [], [], {}
    dk = dv = None
    for k in reversed(range(len(stages))):
        l, s = stages[k]
        if k == 2 * n_a - 1:
            x_kv, hkt, kr = kv_saved
            dkr, gs["kg"] = _head_norm_bwd("kv_bwd_knorm", dk, LEFT, kr, kg_row)
            off_k = LEFT // _tile(t, 512)
            (gwk,) = _mm("kv_bwd_wk", [(_op(hkt), _op(dkr))], d, d, t, "nn", [(d, F32, 0)], ident, tk=2048, **big)
            (gwv,) = _mm("kv_bwd_wv", [(_op(hkt), _op(dv, None, off_k))], d, d, t, "nn", [(d, F32, 0)], ident,
                         tk=512, **big)
            grads += [(("wk", 0), gwk), (("wv", 0), gwv)]
            off_m = LEFT // _tile(t, 512, 8)
            (dhk,) = _mm("kv_bwd_h", [(_op(dkr), _op(full[("wk", 0)])), (_op(dv, None, off_m), _op(full[("wv", 0)]))],
                         t, d, d, "nt", [(t, F32, 0)], ident, tn=1024, tk=2048)
            dx, dxb, gs["kvn"] = _rms_bwd("kv_bwd_norm", dhk, x_kv, row(small["kvn"]), dx)
        tag = f"l{l}_ffn{s}"
        groups = [sums[n::3] for n in range(3)]
        jobs = {key: [_scatter_job([sm for _, sm in grp])] for key, grp in zip(("act", "wd", "wg"), groups) if grp}
        dx, dxb, gs[f"n{s}"][l], ffn_grads, got = _ffn_bwd(
            tag, dx, dxb, saved[k], row(small[f"n{s}"][l]), full[(f"g{s}", l)], full[(f"u{s}", l)],
            full[(f"d{s}", l)], jobs, [g for _, g in grads])
        for key, grp in zip(("act", "wd", "wg"), groups):
            rbufs.update(zip([kk for kk, _ in grp], got.get(key, [])))
        grads += list(zip(ffn_keys(k, "gud"), ffn_grads))
        sums = [(key, _half_sum(f"half_sum_{key[0]}_{key[1]}", g, xr, core))
                for (key, g), xr in zip(grads, got["h"])]
        grads = []
        if s == 2:
            if l < n_a:
                x_in, diff, ypre, wp = mixed[l]
                ddiff, gpool, gs["pscale"][l] = _pool_bwd_mat(f"l{l}_pool_bwd_mat", dx, diff, ypre, wp,
                                                              row(pscale[l]), None)
                grads.append((("pool", l), gpool.reshape(n_g, N_CHIPS, p // N_CHIPS, p).transpose(1, 0, 2, 3)
                              .reshape(n_g * p, p)))
                dx, dxb, gs["nm"][l] = _pool_bwd_win(f"l{l}_pool_bwd_win", ddiff, x_in, row(small["nm"][l]), dx)
            else:
                b = l - n_a
                x_in, ht, qr, brow, o, ot = mixed[l]
                (gwo,) = _mm(f"l{l}_bwd_wo", [(_op(ot), _op(dxb))], d, d, t, "nn", [(d, F32, 0)], ident, tk=2048, **big)
                (do,) = _mm(f"l{l}_bwd_o", [(_op(dxb), _op(full[("wo", b)]))], t, d, d, "nt", [(t, BF16, 0)], ident,
                            tk=2048, **big)
                dqr, dk, dv, dbias, gs["qg"][b] = _attn_bwd(f"l{l}_attn_bwd", qr, row(small["qg"][b]), kpad, vpad,
                                                             brow, o, do, dk, dv)
                gs["rel"][b] = _bias_rows_bwd(f"l{l}_bias_bwd", dbias, n_rel).reshape(-1, n_rel)
                (gwq,) = _mm(f"l{l}_bwd_wq", [(_op(ht), _op(dqr))], d, d, t, "nn", [(d, F32, 0)], ident, tk=2048, **big)
                grads += [(("wo", b), gwo), (("wq", b), gwq)]
                (dh,) = _mm(f"l{l}_bwd_qh", [(_op(dqr), _op(full[("wq", b)]))], t, d, d, "nt", [(t, F32, 0)], ident,
                            tk=2048, **big)
                dx, dxb, gs["nm"][l] = _rms_bwd(f"l{l}_mix_bwd_norm", dh, x_in, row(small["nm"][l]), dx)
    assert not grads
    return loss, dx, rbufs, gs, sums


def _cast_pad(name, wsh, rows_to, cols_to, dtype=BF16):
    n_l, r, c = wsh.shape
    if cols_to != c:
        tr, tc_in, tc_out = _tile(r, 256, 16), c, cols_to
    else:
        tr, tc_in, tc_out = r, _tile(c, 256), _tile(c, 256)
    r_out = rows_to if tr == r else tr

    def body(w_ref, o_ref):
        o_ref[...] = jnp.zeros_like(o_ref)
        o_ref[: w_ref.shape[0], : w_ref.shape[1]] = w_ref[...].astype(dtype)

    return pl.pallas_call(
        body, name=name, grid=(n_l, r // tr, c // tc_in),
        in_specs=[pl.BlockSpec((None, tr, tc_in), lambda l, i, j: (l, i, j))],
        out_specs=pl.BlockSpec((None, r_out, tc_out), lambda l, i, j: (l, i, j)),
        out_shape=jax.ShapeDtypeStruct((n_l, rows_to, cols_to), dtype),
        compiler_params=_params(("parallel", "parallel", "parallel")),
    )(wsh)


def _place():
    x, y, c = lax.axis_index("x"), lax.axis_index("y"), lax.axis_index("c")
    return x, y, c, 2 * x + y


def _piece(ref, chip, half, rows, cols):
    return ref.at[pl.ds(chip * rows, rows), pl.ds(half * (cols // 2), cols // 2)]


class _ExchangeJob:
    def __init__(self, srcs, out_shapes, plan):
        self.srcs, self.out_shapes, self.plan = list(srcs), list(out_shapes), list(plan)
        n_it = len(plan)
        self.sems = [pltpu.SemaphoreType.DMA((n_it, 7)), pltpu.SemaphoreType.DMA((n_it, 7)),
                     pltpu.SemaphoreType.DMA((n_it,))]

    def phases(self, src_refs, out_refs, sem_refs):
        send_sems, recv_sems, loc_sems = sem_refs
        plan = self.plan

        def ctx():
            x, y, c, me = _place()
            return c, me, (x, y, 1 - c), [(1 - x, y), (x, 1 - y), (1 - x, 1 - y)]

        def remote(src, dst, it, k, dev):
            return pltpu.make_async_remote_copy(src_ref=src, dst_ref=dst, send_sem=send_sems.at[it, k],
                                                recv_sem=recv_sems.at[it, k], device_id=dev, device_id_type=MESH)

        def first(it):
            c, me, sib, chips = ctx()
            si, oi, sp, dp = plan[it]
            src, out = src_refs[si], out_refs[oi]
            mine = sp(src, me, c)
            local = pltpu.make_async_copy(mine, dp(out, c, me), loc_sems.at[it])
            sends = [remote(sp(src, 2 * px + py, c), dp(out, c, me), it, k, (px, py, c))
                     for k, (px, py) in enumerate(chips)]
            sends.append(remote(mine, dp(out, c, me), it, 3, sib))
            return local, sends

        def passed_on(it, k):
            c, me, sib, chips = ctx()
            px, py = chips[k]
            landed = dp_of(it)(out_refs[plan[it][1]], c, 2 * px + py)
            return remote(landed, landed, it, k, (px, py, c)), remote(landed, landed, it, 4 + k, sib)

        def dp_of(it):
            return plan[it][3]

        def start():
            for it in range(len(plan)):
                local, sends = first(it)
                local.start()
                for cp in sends:
                    cp.start()

        def mid():
            for it in range(len(plan)):
                for k in range(3):
                    arrival, onward = passed_on(it, k)
                    arrival.wait_recv()
                    onward.start()

        def finish():
            c, me, sib, chips = ctx()
            for it in range(len(plan)):
                out = out_refs[plan[it][1]]
                theirs = dp_of(it)(out, 1 - c, me)
                remote(theirs, theirs, it, 3, sib).wait_recv()
                for k, (px, py) in enumerate(chips):
                    theirs = dp_of(it)(out, 1 - c, 2 * px + py)
                    remote(theirs, theirs, it, 4 + k, sib).wait_recv()
            for it in range(len(plan)):
                local, sends = first(it)
                for cp in sends:
                    cp.wait_send()
                for k in range(3):
                    passed_on(it, k)[1].wait_send()
                local.wait()

        return start, mid, finish


def _gather_job(items):
    srcs, out_shapes, plan = [], [], []
    for n, (sh, layer) in enumerate(items):
        _, r, cc = sh.shape
        srcs.append(sh)
        out_shapes.append(((N_CHIPS * r, cc), sh.dtype))
        sp = lambda ref, chip, c, layer=layer, cc=cc: ref.at[layer, :, pl.ds(c * (cc // 2), cc // 2)]
        dp = lambda ref, half, chip, r=r, cc=cc: _piece(ref, chip, half, r, cc)
        plan.append((n, n, sp, dp))
    return _ExchangeJob(srcs, out_shapes, plan)


def _scatter_job(sums):
    out_shapes, plan = [], []
    for n, s in enumerate(sums):
        r4, hc = s.shape
        r = r4 // N_CHIPS
        out_shapes.append(((2, N_CHIPS, r, hc), BF16))
        sp = lambda ref, chip, c, r=r: ref.at[pl.ds(chip * r, r), :]
        dp = lambda ref, half, chip: ref.at[half, chip]
        plan.append((n, n, sp, dp))
    return _ExchangeJob(sums, out_shapes, plan)


class _SwapJob:
    def __init__(self, grads):
        self.srcs = list(grads)
        self.out_shapes = [((g.shape[0], g.shape[1] // 2), F32) for g in grads]
        self.sems = [pltpu.SemaphoreType.DMA((len(grads),)), pltpu.SemaphoreType.DMA((len(grads),))]

    def phases(self, src_refs, out_refs, sem_refs):
        send_sems, recv_sems = sem_refs

        def copies():
            x, y, c, _ = _place()
            out = []
            for it, (g_ref, x_ref) in enumerate(zip(src_refs, out_refs)):
                hc = g_ref.shape[1] // 2
                out.append(pltpu.make_async_remote_copy(
                    src_ref=g_ref.at[:, pl.ds((1 - c) * hc, hc)], dst_ref=x_ref, send_sem=send_sems.at[it],
                    recv_sem=recv_sems.at[it], device_id=(x, y, 1 - c), device_id_type=MESH))
            return out

        def start():
            for cp in copies():
                cp.start()

        def finish():
            for cp in copies():
                cp.wait()

        return start, lambda: None, finish


def _run_jobs(name, jobs):
    job_ops, job_out, job_sems = _job_io(jobs)
    n_src, n_out = len(job_ops), len(job_out)

    def body(*refs):
        for ph in zip(*_job_phases(jobs, refs[:n_src], refs[n_src: n_src + n_out], refs[n_src + n_out:])):
            for fn in ph:
                fn()

    return pl.pallas_call(body, name=name, in_specs=[ANY] * n_src, out_specs=[ANY] * n_out, out_shape=job_out,
                          scratch_shapes=job_sems)(*job_ops)


def _half_sum(name, g, xr, core):
    r, cc = xr.shape
    tr, tc = _tile(r, 512, 16), _tile(cc, 1024)
    nc = cc // tc

    def body(c_ref, g_ref, x_ref, s_ref):
        s_ref[...] = (g_ref[...] + x_ref[...]).astype(BF16)

    return pl.pallas_call(
        body, name=name,
        grid_spec=pltpu.PrefetchScalarGridSpec(
            num_scalar_prefetch=1, grid=(r // tr, nc),
            in_specs=[pl.BlockSpec((tr, tc), lambda i, j, c_ref: (i, c_ref[0] * nc + j)),
                      pl.BlockSpec((tr, tc), lambda i, j, c_ref: (i, j))],
            out_specs=pl.BlockSpec((tr, tc), lambda i, j, c_ref: (i, j))),
        out_shape=jax.ShapeDtypeStruct((r, cc), BF16), compiler_params=_params(("parallel", "parallel")),
    )(core, g, xr)


def _allreduce_small(name, packed):
    rows = packed.shape[0]

    def body(p_ref, o_ref, slots, send_sems, recv_sems):
        x, y, c, _ = _place()
        me = 4 * x + 2 * y + c
        slots[me] = p_ref[...]
        copies = []
        for k in range(1, 8):
            peer = (x ^ (k >> 2), y ^ ((k >> 1) & 1), c ^ (k & 1))
            cp = pltpu.make_async_remote_copy(src_ref=p_ref, dst_ref=slots.at[me], send_sem=send_sems.at[k - 1],
                                              recv_sem=recv_sems.at[k - 1], device_id=peer, device_id_type=MESH)
            cp.start()
            copies.append(cp)
        for cp in copies:
            cp.wait()
        acc = slots[0]
        for k in range(1, 8):
            acc = acc + slots[k]
        o_ref[...] = acc

    return pl.pallas_call(
        body, name=name, out_shape=jax.ShapeDtypeStruct((rows, LANES), F32),
        in_specs=[pl.BlockSpec(memory_space=pltpu.VMEM)], out_specs=pl.BlockSpec(memory_space=pltpu.VMEM),
        scratch_shapes=[pltpu.VMEM((8, rows, LANES), F32), pltpu.SemaphoreType.DMA((7,)), pltpu.SemaphoreType.DMA((7,))],
    )(packed)


def _adamw_math(w, g, m, v):
    m = ADAM_B1 * m + (1.0 - ADAM_B1) * g
    v = ADAM_B2 * v + (1.0 - ADAM_B2) * (g * g)
    m_hat = m / (1.0 - ADAM_B1 ** ADAM_STEP)
    v_hat = v / (1.0 - ADAM_B2 ** ADAM_STEP)
    delta = -ADAM_LR * (m_hat / (jnp.sqrt(v_hat) + ADAM_EPS) + ADAM_WD * w)
    return delta, m, v


def _adamw_sharded(name, rbufs, w, m, v, jobs=()):
    n_l, r, cc = w.shape
    job_ops, job_out, job_sems = _job_io(jobs)
    rp, hc = rbufs[0].shape[-2:]
    tc = _tile(hc, 256)
    n_t = hc // tc
    wspec = pl.BlockSpec((None, r, tc), lambda l, h, i: (l, 0, h * n_t + i))

    def rspec(q):
        def imap(l, h, i):
            before, after = l < q, l > q
            return (jnp.where(before, 0, jnp.where(after, 1, h)), 0, 0,
                    jnp.where(before, 0, jnp.where(after, n_t - 1, i)))
        return pl.BlockSpec((None, N_CHIPS, rp, tc), imap)

    def body(*refs):
        r_refs = refs[:n_l]
        w_ref, m_ref, v_ref = refs[n_l: n_l + 3]
        job_src = refs[n_l + 3: n_l + 3 + len(job_ops)]
        g_out, d_out, m_out, v_out = refs[n_l + 3 + len(job_ops): n_l + 7 + len(job_ops)]
        job_dst = refs[n_l + 7 + len(job_ops): n_l + 7 + len(job_ops) + len(job_out)]
        rows = w_ref.shape[0]
        layer = pl.program_id(0)
        if jobs:
            step = (layer * 2 + pl.program_id(1)) * n_t + pl.program_id(2)
            _run_phases(_job_phases(jobs, job_src, job_dst, refs[len(refs) - len(job_sems):]), step, n_l * 2 * n_t)
        for q in range(n_l):
            @pl.when(layer == q)
            def _():
                g = r_refs[q][0, :rows, :].astype(F32)
                for k in range(1, N_CHIPS):
                    g = g + r_refs[q][k, :rows, :].astype(F32)
                g_out[...] = g

        g = g_out[...]
        delta, m_new, v_new = _adamw_math(w_ref[...], g, m_ref[...], v_ref[...])
        d_out[...] = delta
        m_out[...] = m_new
        v_out[...] = v_new

    res = pl.pallas_call(
        body, name=name, grid=(n_l, 2, n_t),
        in_specs=[rspec(q) for q in range(n_l)] + [wspec] * 3 + [ANY] * len(job_ops),
        out_specs=[wspec] * 4 + [ANY] * len(job_out),
        out_shape=[jax.ShapeDtypeStruct(w.shape, F32)] * 4 + job_out, scratch_shapes=job_sems,
        compiler_params=_params(("arbitrary", "arbitrary", "arbitrary")),
    )(*rbufs, w, m, v, *job_ops)
    return res[:4], res[4:]


def _adamw_small(name, w, g, m, v):
    def body(w_ref, g_ref, m_ref, v_ref, d_out, m_out, v_out):
        delta, m_new, v_new = _adamw_math(w_ref[...], g_ref[...], m_ref[...], v_ref[...])
        d_out[...] = delta
        m_out[...] = m_new
        v_out[...] = v_new

    return pl.pallas_call(body, name=name, out_shape=[jax.ShapeDtypeStruct(w.shape, F32)] * 3)(w, g, m, v)


def _pack(arrays):
    flat = jnp.concatenate([a.reshape(-1).astype(F32) for a in arrays])
    n = flat.shape[0]
    rows = _round_up(_round_up(n, LANES) // LANES, 8)
    return jnp.pad(flat, (0, rows * LANES - n)).reshape(rows, LANES)


def _unpack(packed, like):
    flat = packed.reshape(-1)
    out, pos = [], 0
    for a in like:
        out.append(flat[pos: pos + a.size].reshape(a.shape))
        pos += a.size
    return out


_SHARDED = ("ffn1_w_gate", "ffn1_w_up", "ffn1_w_down", "ffn2_w_gate", "ffn2_w_up", "ffn2_w_down", "pool_w",
            "w_k", "w_v", "w_q", "w_o")
_SHORT = dict(ffn1_w_gate="g1", ffn1_w_up="u1", ffn1_w_down="d1", ffn2_w_gate="g2", ffn2_w_up="u2", ffn2_w_down="d2",
              pool_w="pool", w_k="wk", w_v="wv", w_q="wq", w_o="wo")
_TRANSPOSED = ("ffn1_w_gate", "ffn1_w_up", "ffn2_w_gate", "ffn2_w_up")
_SMALL = ("ffn1_norm", "mix_norm", "ffn2_norm", "kv_norm", "k_gain", "q_gain", "rel_bias", "pool_scale")
_WEIGHTS = ("ffn1_norm", "ffn1_w_gate", "ffn1_w_up", "ffn1_w_down", "mix_norm", "ffn2_norm", "ffn2_w_gate",
            "ffn2_w_up", "ffn2_w_down", "pool_w", "pool_scale", "kv_norm", "w_k", "w_v", "k_gain", "w_q", "q_gain",
            "rel_bias", "w_o")


def _as3d(name, a):
    if name in _TRANSPOSED:
        return jnp.swapaxes(a, 1, 2)
    if name == "pool_w":
        return a.reshape(a.shape[0], a.shape[1] * a.shape[2], a.shape[3])
    if a.ndim == 2:
        return a[None]
    return a


def _from3d(name, a, shape):
    return jnp.swapaxes(a, 1, 2) if name in _TRANSPOSED else a.reshape(shape)


def kernel(x, ffn1_norm, ffn1_w_gate, ffn1_w_up, ffn1_w_down, mix_norm, ffn2_norm, ffn2_w_gate, ffn2_w_up, ffn2_w_down, pool_w, pool_scale, kv_norm, w_k, w_v, k_gain, w_q, q_gain, rel_bias, w_o, loss_target, m_ffn1_norm, m_ffn1_w_gate, m_ffn1_w_up, m_ffn1_w_down, m_mix_norm, m_ffn2_norm, m_ffn2_w_gate, m_ffn2_w_up, m_ffn2_w_down, m_pool_w, m_pool_scale, m_kv_norm, m_w_k, m_w_v, m_k_gain, m_w_q, m_q_gain, m_rel_bias, m_w_o, v_ffn1_norm, v_ffn1_w_gate, v_ffn1_w_up, v_ffn1_w_down, v_mix_norm, v_ffn2_norm, v_ffn2_w_gate, v_ffn2_w_up, v_ffn2_w_down, v_pool_w, v_pool_scale, v_kv_norm, v_w_k, v_w_v, v_k_gain, v_w_q, v_q_gain, v_rel_bias, v_w_o):
    wts = dict(ffn1_norm=ffn1_norm, ffn1_w_gate=ffn1_w_gate, ffn1_w_up=ffn1_w_up, ffn1_w_down=ffn1_w_down,
               mix_norm=mix_norm, ffn2_norm=ffn2_norm, ffn2_w_gate=ffn2_w_gate, ffn2_w_up=ffn2_w_up,
               ffn2_w_down=ffn2_w_down, pool_w=pool_w, pool_scale=pool_scale, kv_norm=kv_norm, w_k=w_k, w_v=w_v,
               k_gain=k_gain, w_q=w_q, q_gain=q_gain, rel_bias=rel_bias, w_o=w_o)
    mom = dict(ffn1_norm=m_ffn1_norm, ffn1_w_gate=m_ffn1_w_gate, ffn1_w_up=m_ffn1_w_up, ffn1_w_down=m_ffn1_w_down,
               mix_norm=m_mix_norm, ffn2_norm=m_ffn2_norm, ffn2_w_gate=m_ffn2_w_gate, ffn2_w_up=m_ffn2_w_up,
               ffn2_w_down=m_ffn2_w_down, pool_w=m_pool_w, pool_scale=m_pool_scale, kv_norm=m_kv_norm, w_k=m_w_k,
               w_v=m_w_v, k_gain=m_k_gain, w_q=m_w_q, q_gain=m_q_gain, rel_bias=m_rel_bias, w_o=m_w_o)
    var = dict(ffn1_norm=v_ffn1_norm, ffn1_w_gate=v_ffn1_w_gate, ffn1_w_up=v_ffn1_w_up, ffn1_w_down=v_ffn1_w_down,
               mix_norm=v_mix_norm, ffn2_norm=v_ffn2_norm, ffn2_w_gate=v_ffn2_w_gate, ffn2_w_up=v_ffn2_w_up,
               ffn2_w_down=v_ffn2_w_down, pool_w=v_pool_w, pool_scale=v_pool_scale, kv_norm=v_kv_norm, w_k=v_w_k,
               w_v=v_w_v, k_gain=v_k_gain, w_q=v_w_q, q_gain=v_q_gain, rel_bias=v_rel_bias, w_o=v_w_o)
    chip = 2 * lax.axis_index("x") + lax.axis_index("y")
    core = lax.axis_index("c").astype(jnp.int32).reshape(1)

    sh = {}
    for name in _SHARDED:
        a = _as3d(name, wts[name])
        sh[_SHORT[name]] = _cast_pad(f"cast_{name}", a, _round_up(a.shape[1], LANES), a.shape[2])
    sh["pscale"] = _cast_pad("cast_pool_scale", pool_scale[None], 8, pool_scale.shape[1], F32)
    small = dict(n1=ffn1_norm, nm=mix_norm, n2=ffn2_norm, kvn=kv_norm, kg=k_gain, qg=q_gain, rel=rel_bias)

    loss, dx, rbufs, gs, last_sums = _forward_backward(x[0], loss_target[0], sh, small, core)

    small_grads = [jnp.concatenate(gs["n1"]), jnp.concatenate(gs["nm"]), jnp.concatenate(gs["n2"]), gs["kvn"], gs["kg"],
                   jnp.concatenate(gs["qg"]), jnp.stack(gs["rel"]), jnp.concatenate(gs["pscale"]), loss[:, :1]]
    red = _unpack(_allreduce_small("allreduce_small", _pack(small_grads)), small_grads)
    g_small = dict(ffn1_norm=red[0], mix_norm=red[1], ffn2_norm=red[2], kv_norm=red[3].reshape(-1),
                   k_gain=red[4].reshape(-1), q_gain=red[5], rel_bias=red[6],
                   pool_scale=lax.dynamic_slice_in_dim(red[7], chip * pool_scale.shape[1], pool_scale.shape[1], 1))
    loss_out = red[8].reshape(())

    grad, delta, new_m, new_v = {}, {}, {}, {}
    waiting = {key for key, _ in last_sums}
    needs_last = lambda name: any((_SHORT[name], l) in waiting for l in range(_as3d(name, wts[name]).shape[0]))
    groups = [grp for grp in (last_sums[n::3] for n in range(3)) if grp]
    for name in sorted(_SHARDED, key=needs_last):
        w3 = _as3d(name, wts[name])
        grp = groups.pop() if groups and not needs_last(name) and w3.shape[0] == ffn1_norm.shape[0] else []
        outs, got = _adamw_sharded(f"adamw_{name}", [rbufs.get((_SHORT[name], l)) for l in range(w3.shape[0])], w3,
                                   _as3d(name, mom[name]), _as3d(name, var[name]),
                                   [_scatter_job([sm for _, sm in grp])] if grp else ())
        rbufs.update(zip([key for key, _ in grp], got))
        grad[name], delta[name], new_m[name], new_v[name] = [_from3d(name, o, wts[name].shape) for o in outs]
    assert not groups
    sw = [wts[n] for n in _SMALL]
    sg = [g_small[n].reshape(wts[n].shape) for n in _SMALL]
    d_s, m_s, v_s = _adamw_small("adamw_small", _pack(sw), _pack(sg), _pack([mom[n] for n in _SMALL]),
                                 _pack([var[n] for n in _SMALL]))
    for n, g_n, d_n, m_n, v_n in zip(_SMALL, sg, _unpack(d_s, sw), _unpack(m_s, sw), _unpack(v_s, sw)):
        grad[n], delta[n], new_m[n], new_v[n] = g_n, d_n, m_n, v_n
    return (loss_out, dx[None], *[grad[n] for n in _WEIGHTS], *[delta[n] for n in _WEIGHTS],
            *[new_m[n] for n in _WEIGHTS], *[new_v[n] for n in _WEIGHTS])
```

```python
import functools

import jax
import jax.numpy as jnp
from jax import lax
from jax.experimental import pallas as pl
from jax.experimental.pallas import tpu as pltpu

F32 = jnp.float32
BF16 = jnp.bfloat16

EPS = 1e-6
CHUNK = 64
LEFT = 512
LEFT_CHUNKS = LEFT // CHUNK
REL_MAX = 128
POOL_WINDOWS = (2, 4, 8, 16)
HALO = 16
NEG_INF = -1e30
TQ = 256
TILES_PER_STEP = 2
BAND_W = TQ + LEFT
N_CHIPS = 4
LANES = 128
VMEM_LIMIT = 56 * 1024 * 1024

ADAM_LR = 0.001
ADAM_B1 = 0.9
ADAM_B2 = 0.999
ADAM_EPS = 1e-08
ADAM_WD = 0.01
ADAM_STEP = 10

MESH = pl.DeviceIdType.MESH
ANY = pl.BlockSpec(memory_space=pl.ANY)


def _round_up(n, m):
    return (n + m - 1) // m * m


def _tile(n, pref, unit=LANES):
    if n <= pref:
        return n
    t = pref // unit * unit
    while t >= unit:
        if n % t == 0:
            return t
        t -= unit
    return n


def _params(sem):
    return pltpu.CompilerParams(dimension_semantics=sem, vmem_limit_bytes=VMEM_LIMIT)


def _bs(block, imap, lead=None):
    if lead is None:
        return pl.BlockSpec(tuple(block), imap)
    return pl.BlockSpec((None,) + tuple(block), lambda *g: (lead,) + tuple(imap(*g)))


_DIMS = {
    "nn": (((1,), (0,)), ((), ())),
    "nt": (((1,), (1,)), ((), ())),
    "tn": (((0,), (0,)), ((), ())),
}


def _job_io(jobs):
    operands = [s for jb in jobs for s in jb.srcs]
    out_shape = [jax.ShapeDtypeStruct(s, dt) for jb in jobs for s, dt in jb.out_shapes]
    scratch = [s for jb in jobs for s in jb.sems]
    return operands, out_shape, scratch


def _job_phases(jobs, src_refs, out_refs, sem_refs):
    phases, a, b, c = [], 0, 0, 0
    for jb in jobs:
        ns, no, nm = len(jb.srcs), len(jb.out_shapes), len(jb.sems)
        phases.append(jb.phases(src_refs[a:a + ns], out_refs[b:b + no], sem_refs[c:c + nm]))
        a, b, c = a + ns, b + no, c + nm
    return phases


def _run_phases(phases, step, total):
    for n, at in enumerate((0, max(total - 1 - max(1, total // 8), 0), total - 1)):
        @pl.when(step == at)
        def _():
            for ph in phases:
                ph[n]()


def _mm(name, pairs, m, n, k, mode, outs, epilogue, extras=(), tm=512, tn=512, tk=512, jobs=()):
    tm, tn, tk = _tile(m, tm, LANES if mode == "tn" else 8), _tile(n, tn), _tile(k, tk)
    ni, nj, nk = m // tm, n // tn, k // tk
    n_pairs, n_extra, n_out = len(pairs), len(extras), len(outs)
    in_specs, operands = [], []
    for (a, la, oa), (b, lb, ob) in pairs:
        if mode == "tn":
            in_specs.append(_bs((tk, tm), lambda i, j, kk, o=oa: (kk + o, i), la))
        else:
            in_specs.append(_bs((tm, tk), lambda i, j, kk, o=oa: (i + o, kk), la))
        if mode == "nt":
            in_specs.append(_bs((tn, tk), lambda i, j, kk, o=ob: (j + o, kk), lb))
        else:
            in_specs.append(_bs((tk, tn), lambda i, j, kk, o=ob: (kk + o, j), lb))
        operands += [a, b]
    for e, le, kind in extras:
        if kind == "mn":
            in_specs.append(_bs((tm, tn), lambda i, j, kk: (i, j), le))
        else:
            in_specs.append(_bs((1, tn), lambda i, j, kk: (0, j), le))
        operands.append(e)
    out_specs, out_shape, flipped = [], [], []
    for o in outs:
        rows, dt, off = o[:3]
        flipped.append(len(o) > 3 and o[3])
        if flipped[-1]:
            out_specs.append(pl.BlockSpec((tn, tm), lambda i, j, kk: (j, i)))
            out_shape.append(jax.ShapeDtypeStruct((n, rows), dt))
        else:
            out_specs.append(pl.BlockSpec((tm, tn), lambda i, j, kk, o=off: (i + o, j)))
            out_shape.append(jax.ShapeDtypeStruct((rows, n), dt))
    dims = _DIMS[mode]
    job_ops, job_out, job_sems = _job_io(jobs)
    n_in = len(operands)

    def body(*refs):
        ab = refs[: 2 * n_pairs]
        ex = refs[2 * n_pairs: n_in]
        job_src = refs[n_in: n_in + len(job_ops)]
        out_refs = refs[n_in + len(job_ops): n_in + len(job_ops) + n_out]
        job_dst = refs[n_in + len(job_ops) + n_out: n_in + len(job_ops) + n_out + len(job_out)]
        kk = pl.program_id(2)
        if jobs:
            step = (pl.program_id(0) * nj + pl.program_id(1)) * nk + kk
            _run_phases(_job_phases(jobs, job_src, job_dst, refs[len(refs) - len(job_sems):]), step, ni * nj * nk)

        def partial():
            tot = None
            for p in range(n_pairs):
                a = ab[2 * p][...].astype(BF16)
                b = ab[2 * p + 1][...].astype(BF16)
                prod = lax.dot_general(a, b, dims, preferred_element_type=F32)
                tot = prod if tot is None else tot + prod
            return tot

        def finish(total):
            res = epilogue(total, *[e[...] for e in ex])
            for o_ref, o, flip in zip(out_refs, res, flipped):
                o_ref[...] = (o.T if flip else o).astype(o_ref.dtype)

        if nk == 1:
            finish(partial())
            return
        acc = refs[n_in + len(job_ops) + n_out + len(job_out)]

        @pl.when(kk == 0)
        def _():
            acc[...] = jnp.zeros_like(acc)

        acc[...] += partial()

        @pl.when(kk == nk - 1)
        def _():
            finish(acc[...])

    sem = ("arbitrary",) * 3 if jobs else ("parallel", "parallel", "arbitrary")
    res = pl.pallas_call(
        body, name=name, grid=(ni, nj, nk), in_specs=in_specs + [ANY] * len(job_ops),
        out_specs=out_specs + [ANY] * len(job_out), out_shape=out_shape + job_out,
        scratch_shapes=([pltpu.VMEM((tm, tn), F32)] if nk > 1 else []) + job_sems, compiler_params=_params(sem),
    )(*operands, *job_ops)
    return (res[:n_out], res[n_out:]) if jobs else res


def _op(a, lead=None, off=0):
    return (a, lead, off)


def _head_rms(blk, gain):
    r = lax.rsqrt(jnp.mean(blk * blk, axis=-1, keepdims=True) + EPS)
    return blk * r * gain, r


def _rms_fwd(name, x, gain_row):
    t, d = x.shape
    tm = _tile(t, 512)

    def body(x_ref, g_ref, h_ref, ht_ref):
        xv = x_ref[...]
        r = lax.rsqrt(jnp.mean(xv * xv, axis=-1, keepdims=True) + EPS)
        h = (xv * r * g_ref[...]).astype(BF16)
        h_ref[...] = h
        ht_ref[...] = h.T

    return pl.pallas_call(
        body, name=name, grid=(t // tm,),
        in_specs=[pl.BlockSpec((tm, d), lambda i: (i, 0)), pl.BlockSpec((1, d), lambda i: (0, 0))],
        out_specs=[pl.BlockSpec((tm, d), lambda i: (i, 0)), pl.BlockSpec((d, tm), lambda i: (0, i))],
        out_shape=[jax.ShapeDtypeStruct((t, d), BF16), jax.ShapeDtypeStruct((d, t), BF16)],
        compiler_params=_params(("parallel",)),
    )(x, gain_row)


def _rms_bwd_math(dh, xv, g):
    r = lax.rsqrt(jnp.mean(xv * xv, axis=-1, keepdims=True) + EPS)
    u = dh * g
    dx = r * u - xv * (r * r * r * jnp.mean(u * xv, axis=-1, keepdims=True))
    dg = jnp.sum(dh * xv * r, axis=0, keepdims=True)
    return dx, dg


def _rms_bwd(name, dh, x, gain_row, dres=None):
    t, d = x.shape
    tm = _tile(t, 256, 8)
    has_res = dres is not None

    def body(*refs):
        if has_res:
            dh_ref, x_ref, g_ref, dres_ref, dx_ref, dxb_ref, dg_ref = refs
        else:
            dh_ref, x_ref, g_ref, dx_ref, dxb_ref, dg_ref = refs
        dx, dg = _rms_bwd_math(dh_ref[...], x_ref[...], g_ref[...])
        if has_res:
            dx = dx + dres_ref[...]
        dx_ref[...] = dx
        dxb_ref[...] = dx.astype(BF16)

        @pl.when(pl.program_id(0) == 0)
        def _():
            dg_ref[...] = jnp.zeros_like(dg_ref)

        dg_ref[...] += dg

    row = pl.BlockSpec((tm, d), lambda i: (i, 0))
    vec = pl.BlockSpec((1, d), lambda i: (0, 0))
    return pl.pallas_call(
        body, name=name, grid=(t // tm,),
        in_specs=[row, row, vec] + ([row] if has_res else []),
        out_specs=[row, row, vec],
        out_shape=[jax.ShapeDtypeStruct((t, d), F32), jax.ShapeDtypeStruct((t, d), BF16),
                   jax.ShapeDtypeStruct((1, d), F32)],
        compiler_params=_params(("arbitrary",)),
    )(*([dh, x, gain_row] + ([dres] if has_res else [])))


def _ffn_up(name, h, wgt, wut, jobs=()):
    t, d = h.shape
    f = wgt.shape[0]
    tm, tn = _tile(t, 1024), _tile(f, 512)
    ni, nj = t // tm, f // tn
    job_ops, job_out, job_sems = _job_io(jobs)

    def body(*refs):
        h_ref, wg_ref, wu_ref = refs[:3]
        job_src = refs[3: 3 + len(job_ops)]
        g_ref, u_ref, a_ref, at_ref = refs[3 + len(job_ops): 7 + len(job_ops)]
        job_dst = refs[7 + len(job_ops): 7 + len(job_ops) + len(job_out)]
        if jobs:
            step = pl.program_id(0) * nj + pl.program_id(1)
            _run_phases(_job_phases(jobs, job_src, job_dst, refs[len(refs) - len(job_sems):]), step, ni * nj)
        hv = h_ref[...]
        g = lax.dot_general(hv, wg_ref[...], _DIMS["nt"], preferred_element_type=F32)
        u = lax.dot_general(hv, wu_ref[...], _DIMS["nt"], preferred_element_type=F32)
        sig = jax.nn.sigmoid(g)
        silu = g * sig
        g_ref[...] = silu.astype(BF16)
        u_ref[...] = (u * (sig * (1.0 + g * (1.0 - sig)))).astype(BF16)
        a = (silu * u).astype(BF16)
        a_ref[...] = a
        at_ref[...] = a.T

    wspec = pl.BlockSpec((tn, d), lambda i, j: (j, 0))
    ospec = pl.BlockSpec((tm, tn), lambda i, j: (i, j))
    res = pl.pallas_call(
        body, name=name, grid=(ni, nj),
        in_specs=[pl.BlockSpec((tm, d), lambda i, j: (i, 0)), wspec, wspec] + [ANY] * len(job_ops),
        out_specs=[ospec] * 3 + [pl.BlockSpec((tn, tm), lambda i, j: (j, i))] + [ANY] * len(job_out),
        out_shape=[jax.ShapeDtypeStruct((t, f), BF16)] * 3 + [jax.ShapeDtypeStruct((f, t), BF16)] + job_out,
        scratch_shapes=job_sems,
        compiler_params=_params(("arbitrary", "arbitrary") if jobs else ("parallel", "parallel")),
    )(h, wgt, wut, *job_ops)
    return res[:4], res[4:]


def _ffn_bwd(tag, dout, doutb, saved, gain_row, wgt, wut, wd, jobs, swap_extra, early_core=None):
    x, ht, g, u, at = saved
    t, d = x.shape
    f = wd.shape[0]
    jobs = dict(jobs)
    got = {}

    def mm(key, *args, **kw):
        jb = jobs.get(key, ())
        res = _mm(f"{tag}_bwd_{key}", *args, jobs=jb, **kw)
        if jb:
            res, got[key] = res
        return res

    def act_bwd(acc, sv, vv):
        da = 0.5 * acc
        return da * vv.astype(F32), da * sv.astype(F32)

    ident = lambda acc: (acc,)
    dg, du = mm("act", [(_op(doutb), _op(wd))], t, f, d, "nt", [(t, BF16, 0), (t, BF16, 0)],
                act_bwd, extras=[(g, None, "mn"), (u, None, "mn")], tm=1024, tn=512, tk=2048)
    (dwd,) = mm("wd", [(_op(at), _op(doutb))], f, d, t, "nn", [(f, F32, 0)], lambda acc: (0.5 * acc,),
                tm=1408, tn=1024, tk=2048)
    (dwgt,) = mm("wg", [(_op(ht), _op(dg))], d, f, t, "nn", [(d, F32, 0, True)], ident, tm=1024, tn=1408, tk=2048)
    if early_core is not None:
        jobs["wu"] = [_SwapJob([dwgt, dwd])]
    (dwut,) = mm("wu", [(_op(ht), _op(du))], d, f, t, "nn", [(d, F32, 0, True)], ident, tm=1024, tn=1408, tk=2048)
    if early_core is not None:
        early = [_half_sum(f"{tag}_half_sum_{nm}", gr, xr, early_core)
                 for nm, gr, xr in zip(("g", "d"), (dwgt, dwd), got["wu"])]
        jobs["h"] = [_scatter_job(early), _SwapJob(list(swap_extra) + [dwut])]
    else:
        jobs["h"] = [_SwapJob(list(swap_extra) + [dwgt, dwut, dwd])]
    (dh,) = mm("h", [(_op(dg), _op(wgt)), (_op(du), _op(wut))], t, d, f, "nn", [(t, F32, 0)], ident,
               tm=1024, tn=1024, tk=1408)
    dx, dxb, dgain = _rms_bwd(f"{tag}_bwd_norm", dh, x, gain_row, dout)
    return dx, dxb, dgain, [dwgt, dwut, dwd], got


def _inv_count(t0, rows, window):
    tpos = t0 + lax.broadcasted_iota(jnp.int32, (rows, 1), 0)
    return 1.0 / jnp.minimum(tpos + 1, window).astype(F32)


def _pool_fwd(name, x, gain_row, wp, scale_row, layer):
    t, d = x.shape
    n_g = len(POOL_WINDOWS)
    p = d // n_g
    tm = _tile(t, 256, 8)
    hb = tm // HALO

    def body(x_ref, xp_ref, g_ref, wp_ref, s_ref, out_ref, diff_ref, ypre_ref):
        i = pl.program_id(0)
        gain = g_ref[...]

        def norm(v):
            return v * lax.rsqrt(jnp.mean(v * v, axis=-1, keepdims=True) + EPS) * gain

        xv = x_ref[...]
        h = norm(xv)
        hp = jnp.where(i > 0, norm(xp_ref[...]), 0.0)
        ext = jnp.concatenate([hp, h], axis=0)
        for gi, w in enumerate(POOL_WINDOWS):
            cols = slice(gi * p, (gi + 1) * p)
            s = ext[:, cols]
            k = 1
            while k < w:
                s = s + pltpu.roll(s, k, 0)
                k *= 2
            pooled = s[HALO:, :] * _inv_count(i * tm, tm, w)
            diff = (pooled - h[:, cols]).astype(BF16)
            y = jnp.dot(diff, wp_ref[gi], preferred_element_type=F32)
            diff_ref[:, cols] = diff
            ypre_ref[:, cols] = y.astype(BF16)
            out_ref[:, cols] = xv[:, cols] + y * s_ref[:, cols]

    row = pl.BlockSpec((tm, d), lambda i: (i, 0))
    vec = pl.BlockSpec((1, d), lambda i: (0, 0))
    return pl.pallas_call(
        body, name=name, grid=(t // tm,),
        in_specs=[row, pl.BlockSpec((HALO, d), lambda i: (jnp.maximum(i * hb - 1, 0), 0)), vec,
                  _bs((n_g, p, p), lambda i: (0, 0, 0), layer), vec],
        out_specs=[row, row, row],
        out_shape=[jax.ShapeDtypeStruct((t, d), F32), jax.ShapeDtypeStruct((t, d), BF16),
                   jax.ShapeDtypeStruct((t, d), BF16)],
        compiler_params=_params(("parallel",)),
    )(x, x, gain_row, wp, scale_row)


def _pool_bwd_mat(name, dout, diff, ypre, wp, scale_row, layer):
    t, d = dout.shape
    n_g = len(POOL_WINDOWS)
    p = d // n_g
    tm = _tile(t, 256, 8)

    def body(do_ref, diff_ref, ypre_ref, wp_ref, s_ref, dd_ref, dwp_ref, ds_ref):
        @pl.when(pl.program_id(0) == 0)
        def _():
            dwp_ref[...] = jnp.zeros_like(dwp_ref)
            ds_ref[...] = jnp.zeros_like(ds_ref)

        dov = do_ref[...]
        ds_ref[...] += jnp.sum(dov * ypre_ref[...].astype(F32), axis=0, keepdims=True)
        dy = (dov * s_ref[...]).astype(BF16)
        for gi in range(n_g):
            cols = slice(gi * p, (gi + 1) * p)
            dwp_ref[gi] += lax.dot_general(diff_ref[:, cols], dy[:, cols], _DIMS["tn"], preferred_element_type=F32)
            dd_ref[:, cols] = lax.dot_general(dy[:, cols], wp_ref[gi], _DIMS["nt"], preferred_element_type=F32)

    row = pl.BlockSpec((tm, d), lambda i: (i, 0))
    vec = pl.BlockSpec((1, d), lambda i: (0, 0))
    return pl.pallas_call(
        body, name=name, grid=(t // tm,),
        in_specs=[row, row, row, _bs((n_g, p, p), lambda i: (0, 0, 0), layer), vec],
        out_specs=[row, pl.BlockSpec((n_g, p, p), lambda i: (0, 0, 0)), vec],
        out_shape=[jax.ShapeDtypeStruct((t, d), F32), jax.ShapeDtypeStruct((n_g, p, p), F32),
                   jax.ShapeDtypeStruct((1, d), F32)],
        compiler_params=_params(("arbitrary",)),
    )(dout, diff, ypre, wp, scale_row)


def _pool_bwd_win(name, ddiff, x, gain_row, dout):
    t, d = x.shape
    n_g = len(POOL_WINDOWS)
    p = d // n_g
    tm = _tile(t, 256, 8)
    hb = tm // HALO
    n_t = t // tm
    rows = tm + HALO

    def body(e_ref, en_ref, x_ref, g_ref, do_ref, dx_ref, dxb_ref, dg_ref):
        i = pl.program_id(0)
        e = e_ref[...]
        en = jnp.where(i < n_t - 1, en_ref[...], 0.0)
        ext = jnp.concatenate([e, en], axis=0)
        parts = []
        for gi, w in enumerate(POOL_WINDOWS):
            cols = slice(gi * p, (gi + 1) * p)
            s = ext[:, cols] * _inv_count(i * tm, rows, w)
            k = 1
            while k < w:
                s = s + pltpu.roll(s, rows - k, 0)
                k *= 2
            parts.append(s[:tm, :] - e[:, cols])
        dh = jnp.concatenate(parts, axis=1)
        dx, dg = _rms_bwd_math(dh, x_ref[...], g_ref[...])
        dx = dx + do_ref[...]
        dx_ref[...] = dx
        dxb_ref[...] = dx.astype(BF16)

        @pl.when(i == 0)
        def _():
            dg_ref[...] = jnp.zeros_like(dg_ref)

        dg_ref[...] += dg

    row = pl.BlockSpec((tm, d), lambda i: (i, 0))
    vec = pl.BlockSpec((1, d), lambda i: (0, 0))
    return pl.pallas_call(
        body, name=name, grid=(n_t,),
        in_specs=[row, pl.BlockSpec((HALO, d), lambda i: (jnp.minimum((i + 1) * hb, t // HALO - 1), 0)), row, vec, row],
        out_specs=[row, row, vec],
        out_shape=[jax.ShapeDtypeStruct((t, d), F32), jax.ShapeDtypeStruct((t, d), BF16),
                   jax.ShapeDtypeStruct((1, d), F32)],
        compiler_params=_params(("arbitrary",)),
    )(ddiff, ddiff, x, gain_row, dout)


def _rel_index(n_rel):
    dd = lax.broadcasted_iota(jnp.int32, (n_rel, BAND_W), 1)
    e = lax.broadcasted_iota(jnp.int32, (n_rel, BAND_W), 0)
    idx = jnp.where(dd < LEFT + 2 * CHUNK, jnp.clip(LEFT - dd, -(CHUNK - 1), REL_MAX) + (CHUNK - 1), n_rel - 1)
    return (idx == e).astype(F32)


def _bias_rows(name, table):
    n_h, n_rel = table.shape

    def body(t_ref, o_ref):
        o_ref[...] = jnp.dot(t_ref[...], _rel_index(n_rel), preferred_element_type=F32,
                             precision=lax.Precision.HIGHEST)

    rows = pl.pallas_call(body, name=name, out_shape=jax.ShapeDtypeStruct((n_h, BAND_W), F32))(table)
    return rows.reshape(n_h, 1, BAND_W)


def _bias_rows_bwd(name, dbias, n_rel):
    n_h = dbias.shape[0]

    def body(db_ref, o_ref):
        rr = lax.broadcasted_iota(jnp.int32, (TQ, TQ), 0)
        cc = lax.broadcasted_iota(jnp.int32, (TQ, TQ), 1)
        flipped = jnp.dot((rr + cc == TQ - 1).astype(F32), db_ref[...], preferred_element_type=F32,
                          precision=lax.Precision.HIGHEST)
        unskew = pltpu.roll(flipped, BAND_W - TQ + 1, 1, stride=1, stride_axis=0)
        drow = jnp.sum(unskew, axis=0, keepdims=True)
        o_ref[...] = lax.dot_general(drow, _rel_index(n_rel), _DIMS["nt"], preferred_element_type=F32,
                                     precision=lax.Precision.HIGHEST)

    return pl.pallas_call(
        body, name=name, grid=(n_h,),
        in_specs=[pl.BlockSpec((None, TQ, BAND_W), lambda h: (h, 0, 0))],
        out_specs=pl.BlockSpec((None, 1, n_rel), lambda h: (h, 0, 0)),
        out_shape=jax.ShapeDtypeStruct((n_h, 1, n_rel), F32), compiler_params=_params(("parallel",)),
    )(dbias)


def _band_bias(row):
    tile = pltpu.roll(jnp.broadcast_to(row, (TQ, BAND_W)), 0, 1, stride=1, stride_axis=0)
    qc = lax.broadcasted_iota(jnp.int32, (TQ, BAND_W), 0) // CHUNK
    kc = lax.broadcasted_iota(jnp.int32, (TQ, BAND_W), 1) // CHUNK
    return jnp.where((kc >= qc) & (kc <= qc + LEFT_CHUNKS), tile, NEG_INF)


def _band_probs(qv, kb, bias, t0):
    s = lax.dot_general(qv, kb, _DIMS["nt"], preferred_element_type=F32) * (qv.shape[-1] ** -0.5) + bias
    col = lax.broadcasted_iota(jnp.int32, (TQ, BAND_W), 1)
    s = jnp.where(col >= LEFT - t0, s, NEG_INF)
    e = jnp.exp(s - jnp.max(s, axis=-1, keepdims=True))
    return e / jnp.sum(e, axis=-1, keepdims=True)


def _valid_rows(blk, t0):
    r = lax.broadcasted_iota(jnp.int32, (BAND_W, 1), 0)
    return jnp.where(r >= LEFT - t0, blk, jnp.zeros_like(blk))


def _attn_fwd(name, qr, gain_row, kpad, vpad, brow):
    t, d = qr.shape
    hd = gain_row.shape[-1]
    n_h = d // hd

    def body(q_ref, g_ref, k_ref, v_ref, b_ref, o_ref, ot_ref, bias_ref):
        i = pl.program_id(1)

        @pl.when(i == 0)
        def _():
            bias_ref[...] = _band_bias(b_ref[...])

        for sub in range(TILES_PER_STEP):
            rows = slice(sub * TQ, (sub + 1) * TQ)
            t0 = pl.multiple_of((i * TILES_PER_STEP + sub) * TQ, TQ)
            qv = _head_rms(q_ref[rows, :].astype(F32), g_ref[...])[0].astype(BF16)
            kb = _valid_rows(k_ref[pl.ds(t0, BAND_W), :], t0)
            vb = _valid_rows(v_ref[pl.ds(t0, BAND_W), :], t0)
            pr = _band_probs(qv, kb, bias_ref[...], t0)
            o = jnp.dot(pr.astype(BF16), vb, preferred_element_type=F32).astype(BF16)
            o_ref[rows, :] = o
            ot_ref[:, rows] = o.T

    tq = TILES_PER_STEP * TQ
    kv_spec = pl.BlockSpec((LEFT + t, hd), lambda h, i: (0, h))
    return pl.pallas_call(
        body, name=name, grid=(n_h, t // tq),
        in_specs=[pl.BlockSpec((tq, hd), lambda h, i: (i, h)), pl.BlockSpec((1, hd), lambda h, i: (0, 0)),
                  kv_spec, kv_spec, pl.BlockSpec((None, 1, BAND_W), lambda h, i: (h, 0, 0))],
        out_specs=[pl.BlockSpec((tq, hd), lambda h, i: (i, h)), pl.BlockSpec((hd, tq), lambda h, i: (h, i))],
        out_shape=[jax.ShapeDtypeStruct((t, d), BF16), jax.ShapeDtypeStruct((d, t), BF16)],
        scratch_shapes=[pltpu.VMEM((TQ, BAND_W), F32)],
        compiler_params=_params(("parallel", "arbitrary")),
    )(qr, gain_row, kpad, vpad, brow)


def _attn_bwd(name, qr, gain_row, kpad, vpad, brow, o, do, dk_in=None, dv_in=None):
    t, d = qr.shape
    hd = gain_row.shape[-1]
    n_h = d // hd
    scale = hd ** -0.5
    has_init = dk_in is not None

    def body(*refs):
        if has_init:
            (q_ref, g_ref, k_ref, v_ref, b_ref, o_ref, do_ref, dki_ref, dvi_ref,
             dq_ref, dk_ref, dv_ref, db_ref, dg_ref, bias_ref) = refs
        else:
            (q_ref, g_ref, k_ref, v_ref, b_ref, o_ref, do_ref,
             dq_ref, dk_ref, dv_ref, db_ref, dg_ref, bias_ref) = refs
        h = pl.program_id(0)
        i = pl.program_id(1)

        @pl.when(i == 0)
        def _():
            bias_ref[...] = _band_bias(b_ref[...])
            db_ref[...] = jnp.zeros_like(db_ref)
            if has_init:
                dk_ref[...] = dki_ref[...]
                dv_ref[...] = dvi_ref[...]
            else:
                dk_ref[...] = jnp.zeros_like(dk_ref)
                dv_ref[...] = jnp.zeros_like(dv_ref)

        @pl.when((i == 0) & (h == 0))
        def _():
            dg_ref[...] = jnp.zeros_like(dg_ref)

        gain = g_ref[...]
        parts = []
        for sub in range(TILES_PER_STEP):
            rows = slice(sub * TQ, (sub + 1) * TQ)
            t0 = pl.multiple_of((i * TILES_PER_STEP + sub) * TQ, TQ)
            qraw = q_ref[rows, :].astype(F32)
            qn, r = _head_rms(qraw, gain)
            qv = qn.astype(BF16)
            kb = _valid_rows(k_ref[pl.ds(t0, BAND_W), :], t0)
            vb = _valid_rows(v_ref[pl.ds(t0, BAND_W), :], t0)
            pr = _band_probs(qv, kb, bias_ref[...], t0)
            dov = do_ref[rows, :]
            delta = jnp.sum(dov.astype(F32) * o_ref[rows, :].astype(F32), axis=-1, keepdims=True)
            dp = lax.dot_general(dov, vb, _DIMS["nt"], preferred_element_type=F32)
            ds = pr * (dp - delta)
            dsb = ds.astype(BF16)
            dq = jnp.dot(dsb, kb, preferred_element_type=F32) * scale
            dkb = lax.dot_general(dsb, qv, _DIMS["tn"], preferred_element_type=F32) * scale
            dvb = lax.dot_general(pr.astype(BF16), dov, _DIMS["tn"], preferred_element_type=F32)
            u = dq * gain
            dq_ref[rows, :] = (r * u - qraw * (r * r * r * jnp.mean(u * qraw, axis=-1, keepdims=True))).astype(BF16)
            parts.append((t0, ds, dkb, dvb, jnp.sum(dq * qraw * r, axis=0, keepdims=True)))
        for t0, ds, dkb, dvb, dgain in parts:
            db_ref[...] += ds
            dk_ref[pl.ds(t0, BAND_W), :] += dkb
            dv_ref[pl.ds(t0, BAND_W), :] += dvb
            dg_ref[...] += dgain

    qspec = pl.BlockSpec((TILES_PER_STEP * TQ, hd), lambda h, i: (i, h))
    kv_spec = pl.BlockSpec((LEFT + t, hd), lambda h, i: (0, h))
    gspec = pl.BlockSpec((1, hd), lambda h, i: (0, 0))
    ins = [qr, gain_row, kpad, vpad, brow, o, do] + ([dk_in, dv_in] if has_init else [])
    return pl.pallas_call(
        body, name=name, grid=(n_h, t // (TILES_PER_STEP * TQ)),
        in_specs=[qspec, gspec, kv_spec, kv_spec, pl.BlockSpec((None, 1, BAND_W), lambda h, i: (h, 0, 0)), qspec, qspec]
        + ([kv_spec, kv_spec] if has_init else []),
        out_specs=[qspec, kv_spec, kv_spec, pl.BlockSpec((None, TQ, BAND_W), lambda h, i: (h, 0, 0)), gspec],
        out_shape=[jax.ShapeDtypeStruct((t, d), BF16), jax.ShapeDtypeStruct((LEFT + t, d), F32),
                   jax.ShapeDtypeStruct((LEFT + t, d), F32), jax.ShapeDtypeStruct((n_h, TQ, BAND_W), F32),
                   jax.ShapeDtypeStruct((1, hd), F32)],
        scratch_shapes=[pltpu.VMEM((TQ, BAND_W), F32)],
        compiler_params=_params(("arbitrary", "arbitrary")),
    )(*ins)


def _head_norm_bwd(name, dy, off_rows, yraw, gain_row):
    t, d = yraw.shape
    hd = gain_row.shape[-1]
    tm = _tile(t, 512, 8)
    off = off_rows // tm

    def body(dy_ref, y_ref, g_ref, dx_ref, dg_ref):
        @pl.when((pl.program_id(0) == 0) & (pl.program_id(1) == 0))
        def _():
            dg_ref[...] = jnp.zeros_like(dg_ref)

        gain = g_ref[...]
        yv = y_ref[...].astype(F32)
        dyv = dy_ref[...]
        r = lax.rsqrt(jnp.mean(yv * yv, axis=-1, keepdims=True) + EPS)
        u = dyv * gain
        dx_ref[...] = (r * u - yv * (r * r * r * jnp.mean(u * yv, axis=-1, keepdims=True))).astype(BF16)
        dg_ref[...] += jnp.sum(dyv * yv * r, axis=0, keepdims=True)

    blk = pl.BlockSpec((tm, hd), lambda i, h: (i, h))
    gspec = pl.BlockSpec((1, hd), lambda i, h: (0, 0))
    return pl.pallas_call(
        body, name=name, grid=(t // tm, d // hd),
        in_specs=[pl.BlockSpec((tm, hd), lambda i, h: (i + off, h)), blk, gspec],
        out_specs=[blk, gspec],
        out_shape=[jax.ShapeDtypeStruct((t, d), BF16), jax.ShapeDtypeStruct((1, hd), F32)],
        compiler_params=_params(("arbitrary", "arbitrary")),
    )(dy, yraw, gain_row)


def _head_norm_epilogue(hd):
    def epi(acc, gain):
        parts = [_head_rms(acc[:, c:c + hd], gain[:, c:c + hd])[0] for c in range(0, acc.shape[-1], hd)]
        return acc, jnp.concatenate(parts, axis=1)
    return epi


def _loss_head(name, y, target):
    t, d = y.shape
    tm = _tile(t, 512, 8)

    def body(y_ref, t_ref, dy_ref, dyb_ref, l_ref):
        @pl.when(pl.program_id(0) == 0)
        def _():
            l_ref[...] = jnp.zeros_like(l_ref)

        err = y_ref[...] - t_ref[...]
        dy = err * (1.0 / d)
        dy_ref[...] = dy
        dyb_ref[...] = dy.astype(BF16)
        per_tok = jnp.mean(err * err, axis=-1, keepdims=True)
        l_ref[...] += 0.5 * jnp.sum(per_tok, axis=0, keepdims=True)

    row = pl.BlockSpec((tm, d), lambda i: (i, 0))
    return pl.pallas_call(
        body, name=name, grid=(t // tm,), in_specs=[row, row],
        out_specs=[row, row, pl.BlockSpec((1, LANES), lambda i: (0, 0))],
        out_shape=[jax.ShapeDtypeStruct((t, d), F32), jax.ShapeDtypeStruct((t, d), BF16),
                   jax.ShapeDtypeStruct((1, LANES), F32)],
        compiler_params=_params(("arbitrary",)),
    )(y, target)


def _forward_backward(x, target, sh, small, core):
    t, d = x.shape
    n_layers = small["n1"].shape[0]
    n_a = sh["pool"].shape[0]
    n_b = sh["wq"].shape[0]
    hd = small["kg"].shape[0]
    n_rel = small["rel"].shape[-1]
    n_g = len(POOL_WINDOWS)
    p = d // n_g
    row = lambda v: v.reshape(1, -1)
    kg_row = row(small["kg"])
    big = dict(tm=1024, tn=1024)
    ident = lambda acc: (acc,)

    stages = [(l, s) for l in range(n_layers) for s in (1, 2)]
    ffn_keys = lambda k, names: [(f"{nm}{stages[k][1]}", stages[k][0]) for nm in names]
    up_keys = {k: ffn_keys(k + 1, "gu") for k in range(len(stages) - 1)}
    down_keys = {k: ffn_keys(k + 1, "d") for k in range(len(stages) - 1)}
    up_keys[0] = ffn_keys(0, "d") + up_keys[0]
    up_keys[2 * n_a - 2] = up_keys[2 * n_a - 2] + [("wk", 0), ("wv", 0)]
    for b in range(n_b):
        up_keys[2 * (n_a + b) - 1] = up_keys[2 * (n_a + b) - 1] + [("wq", b), ("wo", b)]
    first_keys = ffn_keys(0, "gu") + [("pool", l) for l in range(n_a)] + [("pscale", 0)]
    full = {}

    def gather(keys):
        return [_gather_job([(sh[nm], l) for nm, l in keys])] if keys else []

    full.update(zip(first_keys, _run_jobs("gather_first", gather(first_keys))))
    pscale = full[("pscale", 0)].reshape(N_CHIPS, 8, p)[:, :n_a].transpose(1, 0, 2).reshape(n_a, d)
    pool_full = lambda l: (full[("pool", l)].reshape(N_CHIPS, n_g, p // N_CHIPS, p).transpose(1, 0, 2, 3)
                           .reshape(n_g, p, p))

    saved, mixed = [], []
    kv_saved = None
    kpad = vpad = None
    for k, (l, s) in enumerate(stages):
        if s == 2:
            x_in = x
            if l < n_a:
                wp = pool_full(l)
                x, diff, ypre = _pool_fwd(f"l{l}_pool", x, row(small["nm"][l]), wp, row(pscale[l]), None)
                mixed.append((x_in, diff, ypre, wp))
            else:
                b = l - n_a
                h, ht = _rms_fwd(f"l{l}_mix_norm", x, row(small["nm"][l]))
                (qr,) = _mm(f"l{l}_q", [(_op(h), _op(full[("wq", b)]))], t, d, d, "nn", [(t, BF16, 0)], ident,
                            tk=2048, **big)
                brow = _bias_rows(f"l{l}_bias", small["rel"][b])
                o, ot = _attn_fwd(f"l{l}_attn", qr, row(small["qg"][b]), kpad, vpad, brow)
                (x,) = _mm(f"l{l}_o", [(_op(o), _op(full[("wo", b)]))], t, d, d, "nn", [(t, F32, 0)],
                           lambda acc, xv: (xv + acc,), extras=[(x_in, None, "mn")], tk=2048, **big)
                mixed.append((x_in, ht, qr, brow, o, ot))
        tag = f"l{l}_ffn{s}"
        h, ht = _rms_fwd(f"{tag}_norm", x, row(small[f"n{s}"][l]))
        (g, u, a, at), got = _ffn_up(f"{tag}_up", h, full[(f"g{s}", l)], full[(f"u{s}", l)], gather(up_keys.get(k, [])))
        full.update(zip(up_keys.get(k, []), got))
        jobs = gather(down_keys.get(k, []))
        res = _mm(f"{tag}_down", [(_op(a), _op(full[(f"d{s}", l)]))], t, d, a.shape[1], "nn", [(t, F32, 0)],
                  lambda acc, xv: (xv + 0.5 * acc,), extras=[(x, None, "mn")], tm=1024, tn=1024, tk=2816, jobs=jobs)
        (x_new,), got = res if jobs else (res, [])
        full.update(zip(down_keys.get(k, []), got))
        saved.append((x, ht, g, u, at))
        x = x_new
        if k == 2 * n_a - 1:
            hk, hkt = _rms_fwd("kv_norm", x, row(small["kvn"]))
            pad_blocks = LEFT // _tile(t, 512, 8)
            kr, kpad = _mm("kv_k", [(_op(hk), _op(full[("wk", 0)]))], t, d, d, "nn",
                           [(t, BF16, 0), (LEFT + t, BF16, pad_blocks)],
                           _head_norm_epilogue(hd), extras=[(jnp.tile(kg_row, (1, d // hd)), None, "row")],
                           tn=1024, tk=2048)
            (vpad,) = _mm("kv_v", [(_op(hk), _op(full[("wv", 0)]))], t, d, d, "nn", [(LEFT + t, BF16, pad_blocks)],
                          ident, tn=1024, tk=2048)
            kv_saved = (x, hkt, kr)

    dx, dxb, loss = _loss_head("loss_head", x, target)

    gs = {nm: [None] * n_layers for nm in ("n1", "nm", "n2")}
    gs.update(qg=[None] * n_b, rel=[None] * n_b, pscale=[None] * n_a)
    grads, sums, rbufs = [], [], {}
    dk = dv = None
    for k in reversed(range(len(stages))):
        l, s = stages[k]
        if k == 2 * n_a - 1:
            x_kv, hkt, kr = kv_saved
            dkr, gs["kg"] = _head_norm_bwd("kv_bwd_knorm", dk, LEFT, kr, kg_row)
            off_k = LEFT // _tile(t, 512)
            (gwk,) = _mm("kv_bwd_wk", [(_op(hkt), _op(dkr))], d, d, t, "nn", [(d, F32, 0)], ident, tk=2048, **big)
            (gwv,) = _mm("kv_bwd_wv", [(_op(hkt), _op(dv, None, off_k))], d, d, t, "nn", [(d, F32, 0)], ident,
                         tk=512, **big)
            grads += [(("wk", 0), gwk), (("wv", 0), gwv)]
            off_m = LEFT // _tile(t, 512, 8)
            (dhk,) = _mm("kv_bwd_h", [(_op(dkr), _op(full[("wk", 0)])), (_op(dv, None, off_m), _op(full[("wv", 0)]))],
                         t, d, d, "nt", [(t, F32, 0)], ident, tn=1024, tk=2048)
            dx, dxb, gs["kvn"] = _rms_bwd("kv_bwd_norm", dhk, x_kv, row(small["kvn"]), dx)
        tag = f"l{l}_ffn{s}"
        groups = [sums[n::3] for n in range(3)]
        jobs = {key: [_scatter_job([sm for _, sm in grp])] for key, grp in zip(("act", "wd", "wg"), groups) if grp}
        dx, dxb, gs[f"n{s}"][l], ffn_grads, got = _ffn_bwd(
            tag, dx, dxb, saved[k], row(small[f"n{s}"][l]), full[(f"g{s}", l)], full[(f"u{s}", l)],
            full[(f"d{s}", l)], jobs, [g for _, g in grads], core if k == 0 else None)
        for key, grp in zip(("act", "wd", "wg"), groups):
            rbufs.update(zip([kk for kk, _ in grp], got.get(key, [])))
        swapped = got["h"]
        if k == 0:
            key_g, key_u, key_d = ffn_keys(k, "gud")
            rbufs[key_g], rbufs[key_d], swapped = swapped[0], swapped[1], swapped[2:]
            grads.append((key_u, ffn_grads[1]))
        else:
            grads += list(zip(ffn_keys(k, "gud"), ffn_grads))
        sums = [(key, _half_sum(f"half_sum_{key[0]}_{key[1]}", g, xr, core))
                for (key, g), xr in zip(grads, swapped)]
        grads = []
        if s == 2:
            if l < n_a:
                x_in, diff, ypre, wp = mixed[l]
                ddiff, gpool, gs["pscale"][l] = _pool_bwd_mat(f"l{l}_pool_bwd_mat", dx, diff, ypre, wp,
                                                              row(pscale[l]), None)
                grads.append((("pool", l), gpool.reshape(n_g, N_CHIPS, p // N_CHIPS, p).transpose(1, 0, 2, 3)
                              .reshape(n_g * p, p)))
                dx, dxb, gs["nm"][l] = _pool_bwd_win(f"l{l}_pool_bwd_win", ddiff, x_in, row(small["nm"][l]), dx)
            else:
                b = l - n_a
                x_in, ht, qr, brow, o, ot = mixed[l]
                (gwo,) = _mm(f"l{l}_bwd_wo", [(_op(ot), _op(dxb))], d, d, t, "nn", [(d, F32, 0)], ident, tk=2048, **big)
                (do,) = _mm(f"l{l}_bwd_o", [(_op(dxb), _op(full[("wo", b)]))], t, d, d, "nt", [(t, BF16, 0)], ident,
                            tk=2048, **big)
                dqr, dk, dv, dbias, gs["qg"][b] = _attn_bwd(f"l{l}_attn_bwd", qr, row(small["qg"][b]), kpad, vpad,
                                                             brow, o, do, dk, dv)
                gs["rel"][b] = _bias_rows_bwd(f"l{l}_bias_bwd", dbias, n_rel).reshape(-1, n_rel)
                (gwq,) = _mm(f"l{l}_bwd_wq", [(_op(ht), _op(dqr))], d, d, t, "nn", [(d, F32, 0)], ident, tk=2048, **big)
                grads += [(("wo", b), gwo), (("wq", b), gwq)]
                (dh,) = _mm(f"l{l}_bwd_qh", [(_op(dqr), _op(full[("wq", b)]))], t, d, d, "nt", [(t, F32, 0)], ident,
                            tk=2048, **big)
                dx, dxb, gs["nm"][l] = _rms_bwd(f"l{l}_mix_bwd_norm", dh, x_in, row(small["nm"][l]), dx)
    assert not grads
    rbufs.update(zip([key for key, _ in sums], _run_jobs("scatter_last", [_scatter_job([sm for _, sm in sums])])))
    return loss, dx, rbufs, gs


def _cast_pad(name, wsh, rows_to, cols_to, dtype=BF16):
    n_l, r, c = wsh.shape
    if cols_to != c:
        tr, tc_in, tc_out = _tile(r, 256, 16), c, cols_to
    else:
        tr, tc_in, tc_out = r, _tile(c, 256), _tile(c, 256)
    r_out = rows_to if tr == r else tr

    def body(w_ref, o_ref):
        o_ref[...] = jnp.zeros_like(o_ref)
        o_ref[: w_ref.shape[0], : w_ref.shape[1]] = w_ref[...].astype(dtype)

    return pl.pallas_call(
        body, name=name, grid=(n_l, r // tr, c // tc_in),
        in_specs=[pl.BlockSpec((None, tr, tc_in), lambda l, i, j: (l, i, j))],
        out_specs=pl.BlockSpec((None, r_out, tc_out), lambda l, i, j: (l, i, j)),
        out_shape=jax.ShapeDtypeStruct((n_l, rows_to, cols_to), dtype),
        compiler_params=_params(("parallel", "parallel", "parallel")),
    )(wsh)


def _place():
    x, y, c = lax.axis_index("x"), lax.axis_index("y"), lax.axis_index("c")
    return x, y, c, 2 * x + y


def _piece(ref, chip, half, rows, cols):
    return ref.at[pl.ds(chip * rows, rows), pl.ds(half * (cols // 2), cols // 2)]


class _ExchangeJob:
    def __init__(self, srcs, out_shapes, plan):
        self.srcs, self.out_shapes, self.plan = list(srcs), list(out_shapes), list(plan)
        n_it = len(plan)
        self.sems = [pltpu.SemaphoreType.DMA((n_it, 7)), pltpu.SemaphoreType.DMA((n_it, 7)),
                     pltpu.SemaphoreType.DMA((n_it,))]

    def phases(self, src_refs, out_refs, sem_refs):
        send_sems, recv_sems, loc_sems = sem_refs
        plan = self.plan

        def ctx():
            x, y, c, me = _place()
            return c, me, (x, y, 1 - c), [(1 - x, y), (x, 1 - y), (1 - x, 1 - y)]

        def remote(src, dst, it, k, dev):
            return pltpu.make_async_remote_copy(src_ref=src, dst_ref=dst, send_sem=send_sems.at[it, k],
                                                recv_sem=recv_sems.at[it, k], device_id=dev, device_id_type=MESH)

        def first(it):
            c, me, sib, chips = ctx()
            si, oi, sp, dp = plan[it]
            src, out = src_refs[si], out_refs[oi]
            mine = sp(src, me, c)
            local = pltpu.make_async_copy(mine, dp(out, c, me), loc_sems.at[it])
            sends = [remote(sp(src, 2 * px + py, c), dp(out, c, me), it, k, (px, py, c))
                     for k, (px, py) in enumerate(chips)]
            sends.append(remote(mine, dp(out, c, me), it, 3, sib))
            return local, sends

        def passed_on(it, k):
            c, me, sib, chips = ctx()
            px, py = chips[k]
            landed = dp_of(it)(out_refs[plan[it][1]], c, 2 * px + py)
            return remote(landed, landed, it, k, (px, py, c)), remote(landed, landed, it, 4 + k, sib)

        def dp_of(it):
            return plan[it][3]

        def start():
            for it in range(len(plan)):
                local, sends = first(it)
                local.start()
                for cp in sends:
                    cp.start()

        def mid():
            for it in range(len(plan)):
                for k in range(3):
                    arrival, onward = passed_on(it, k)
                    arrival.wait_recv()
                    onward.start()

        def finish():
            c, me, sib, chips = ctx()
            for it in range(len(plan)):
                out = out_refs[plan[it][1]]
                theirs = dp_of(it)(out, 1 - c, me)
                remote(theirs, theirs, it, 3, sib).wait_recv()
                for k, (px, py) in enumerate(chips):
                    theirs = dp_of(it)(out, 1 - c, 2 * px + py)
                    remote(theirs, theirs, it, 4 + k, sib).wait_recv()
            for it in range(len(plan)):
                local, sends = first(it)
                for cp in sends:
                    cp.wait_send()
                for k in range(3):
                    passed_on(it, k)[1].wait_send()
                local.wait()

        return start, mid, finish


def _gather_job(items):
    srcs, out_shapes, plan = [], [], []
    for n, (sh, layer) in enumerate(items):
        _, r, cc = sh.shape
        srcs.append(sh)
        out_shapes.append(((N_CHIPS * r, cc), sh.dtype))
        sp = lambda ref, chip, c, layer=layer, cc=cc: ref.at[layer, :, pl.ds(c * (cc // 2), cc // 2)]
        dp = lambda ref, half, chip, r=r, cc=cc: _piece(ref, chip, half, r, cc)
        plan.append((n, n, sp, dp))
    return _ExchangeJob(srcs, out_shapes, plan)


def _scatter_job(sums):
    out_shapes, plan = [], []
    for n, s in enumerate(sums):
        r4, hc = s.shape
        r = r4 // N_CHIPS
        out_shapes.append(((2, N_CHIPS, r, hc), BF16))
        sp = lambda ref, chip, c, r=r: ref.at[pl.ds(chip * r, r), :]
        dp = lambda ref, half, chip: ref.at[half, chip]
        plan.append((n, n, sp, dp))
    return _ExchangeJob(sums, out_shapes, plan)


class _SwapJob:
    def __init__(self, grads):
        self.srcs = list(grads)
        self.out_shapes = [((g.shape[0], g.shape[1] // 2), F32) for g in grads]
        self.sems = [pltpu.SemaphoreType.DMA((len(grads),)), pltpu.SemaphoreType.DMA((len(grads),))]

    def phases(self, src_refs, out_refs, sem_refs):
        send_sems, recv_sems = sem_refs

        def copies():
            x, y, c, _ = _place()
            out = []
            for it, (g_ref, x_ref) in enumerate(zip(src_refs, out_refs)):
                hc = g_ref.shape[1] // 2
                out.append(pltpu.make_async_remote_copy(
                    src_ref=g_ref.at[:, pl.ds((1 - c) * hc, hc)], dst_ref=x_ref, send_sem=send_sems.at[it],
                    recv_sem=recv_sems.at[it], device_id=(x, y, 1 - c), device_id_type=MESH))
            return out

        def start():
            for cp in copies():
                cp.start()

        def finish():
            for cp in copies():
                cp.wait()

        return start, lambda: None, finish


def _run_jobs(name, jobs):
    job_ops, job_out, job_sems = _job_io(jobs)
    n_src, n_out = len(job_ops), len(job_out)

    def body(*refs):
        for ph in zip(*_job_phases(jobs, refs[:n_src], refs[n_src: n_src + n_out], refs[n_src + n_out:])):
            for fn in ph:
                fn()

    return pl.pallas_call(body, name=name, in_specs=[ANY] * n_src, out_specs=[ANY] * n_out, out_shape=job_out,
                          scratch_shapes=job_sems)(*job_ops)


def _half_sum(name, g, xr, core):
    r, cc = xr.shape
    tr, tc = _tile(r, 512, 16), _tile(cc, 1024)
    nc = cc // tc

    def body(c_ref, g_ref, x_ref, s_ref):
        s_ref[...] = (g_ref[...] + x_ref[...]).astype(BF16)

    return pl.pallas_call(
        body, name=name,
        grid_spec=pltpu.PrefetchScalarGridSpec(
            num_scalar_prefetch=1, grid=(r // tr, nc),
            in_specs=[pl.BlockSpec((tr, tc), lambda i, j, c_ref: (i, c_ref[0] * nc + j)),
                      pl.BlockSpec((tr, tc), lambda i, j, c_ref: (i, j))],
            out_specs=pl.BlockSpec((tr, tc), lambda i, j, c_ref: (i, j))),
        out_shape=jax.ShapeDtypeStruct((r, cc), BF16), compiler_params=_params(("parallel", "parallel")),
    )(core, g, xr)


def _allreduce_small(name, packed):
    rows = packed.shape[0]

    def body(p_ref, o_ref, slots, send_sems, recv_sems):
        x, y, c, _ = _place()
        me = 4 * x + 2 * y + c
        slots[me] = p_ref[...]
        copies = []
        for k in range(1, 8):
            peer = (x ^ (k >> 2), y ^ ((k >> 1) & 1), c ^ (k & 1))
            cp = pltpu.make_async_remote_copy(src_ref=p_ref, dst_ref=slots.at[me], send_sem=send_sems.at[k - 1],
                                              recv_sem=recv_sems.at[k - 1], device_id=peer, device_id_type=MESH)
            cp.start()
            copies.append(cp)
        for cp in copies:
            cp.wait()
        acc = slots[0]
        for k in range(1, 8):
            acc = acc + slots[k]
        o_ref[...] = acc

    return pl.pallas_call(
        body, name=name, out_shape=jax.ShapeDtypeStruct((rows, LANES), F32),
        in_specs=[pl.BlockSpec(memory_space=pltpu.VMEM)], out_specs=pl.BlockSpec(memory_space=pltpu.VMEM),
        scratch_shapes=[pltpu.VMEM((8, rows, LANES), F32), pltpu.SemaphoreType.DMA((7,)), pltpu.SemaphoreType.DMA((7,))],
    )(packed)


def _adamw_math(w, g, m, v):
    m = ADAM_B1 * m + (1.0 - ADAM_B1) * g
    v = ADAM_B2 * v + (1.0 - ADAM_B2) * (g * g)
    m_hat = m / (1.0 - ADAM_B1 ** ADAM_STEP)
    v_hat = v / (1.0 - ADAM_B2 ** ADAM_STEP)
    delta = -ADAM_LR * (m_hat / (jnp.sqrt(v_hat) + ADAM_EPS) + ADAM_WD * w)
    return delta, m, v


def _adamw_sharded(name, rbufs, w, m, v):
    n_l, r, cc = w.shape
    rp, hc = rbufs[0].shape[-2:]
    tc = _tile(hc, 256)
    n_t = hc // tc
    wspec = pl.BlockSpec((None, r, tc), lambda l, h, i: (l, 0, h * n_t + i))

    def rspec(q):
        def imap(l, h, i):
            before, after = l < q, l > q
            return (jnp.where(before, 0, jnp.where(after, 1, h)), 0, 0,
                    jnp.where(before, 0, jnp.where(after, n_t - 1, i)))
        return pl.BlockSpec((None, N_CHIPS, rp, tc), imap)

    def body(*refs):
        r_refs = refs[:n_l]
        w_ref, m_ref, v_ref, g_out, d_out, m_out, v_out = refs[n_l:]
        rows = w_ref.shape[0]
        layer = pl.program_id(0)
        for q in range(n_l):
            @pl.when(layer == q)
            def _():
                g = r_refs[q][0, :rows, :].astype(F32)
                for k in range(1, N_CHIPS):
                    g = g + r_refs[q][k, :rows, :].astype(F32)
                g_out[...] = g

        g = g_out[...]
        delta, m_new, v_new = _adamw_math(w_ref[...], g, m_ref[...], v_ref[...])
        d_out[...] = delta
        m_out[...] = m_new
        v_out[...] = v_new

    return pl.pallas_call(
        body, name=name, grid=(n_l, 2, n_t), in_specs=[rspec(q) for q in range(n_l)] + [wspec] * 3,
        out_specs=[wspec] * 4, out_shape=[jax.ShapeDtypeStruct(w.shape, F32)] * 4,
        compiler_params=_params(("arbitrary", "arbitrary", "arbitrary")),
    )(*rbufs, w, m, v)


def _adamw_small(name, w, g, m, v):
    def body(w_ref, g_ref, m_ref, v_ref, d_out, m_out, v_out):
        delta, m_new, v_new = _adamw_math(w_ref[...], g_ref[...], m_ref[...], v_ref[...])
        d_out[...] = delta
        m_out[...] = m_new
        v_out[...] = v_new

    return pl.pallas_call(body, name=name, out_shape=[jax.ShapeDtypeStruct(w.shape, F32)] * 3)(w, g, m, v)


def _pack(arrays):
    flat = jnp.concatenate([a.reshape(-1).astype(F32) for a in arrays])
    n = flat.shape[0]
    rows = _round_up(_round_up(n, LANES) // LANES, 8)
    return jnp.pad(flat, (0, rows * LANES - n)).reshape(rows, LANES)


def _unpack(packed, like):
    flat = packed.reshape(-1)
    out, pos = [], 0
    for a in like:
        out.append(flat[pos: pos + a.size].reshape(a.shape))
        pos += a.size
    return out


_SHARDED = ("ffn1_w_gate", "ffn1_w_up", "ffn1_w_down", "ffn2_w_gate", "ffn2_w_up", "ffn2_w_down", "pool_w",
            "w_k", "w_v", "w_q", "w_o")
_SHORT = dict(ffn1_w_gate="g1", ffn1_w_up="u1", ffn1_w_down="d1", ffn2_w_gate="g2", ffn2_w_up="u2", ffn2_w_down="d2",
              pool_w="pool", w_k="wk", w_v="wv", w_q="wq", w_o="wo")
_TRANSPOSED = ("ffn1_w_gate", "ffn1_w_up", "ffn2_w_gate", "ffn2_w_up")
_SMALL = ("ffn1_norm", "mix_norm", "ffn2_norm", "kv_norm", "k_gain", "q_gain", "rel_bias", "pool_scale")
_WEIGHTS = ("ffn1_norm", "ffn1_w_gate", "ffn1_w_up", "ffn1_w_down", "mix_norm", "ffn2_norm", "ffn2_w_gate",
            "ffn2_w_up", "ffn2_w_down", "pool_w", "pool_scale", "kv_norm", "w_k", "w_v", "k_gain", "w_q", "q_gain",
            "rel_bias", "w_o")


def _as3d(name, a):
    if name in _TRANSPOSED:
        return jnp.swapaxes(a, 1, 2)
    if name == "pool_w":
        return a.reshape(a.shape[0], a.shape[1] * a.shape[2], a.shape[3])
    if a.ndim == 2:
        return a[None]
    return a


def _from3d(name, a, shape):
    return jnp.swapaxes(a, 1, 2) if name in _TRANSPOSED else a.reshape(shape)


def kernel(x, ffn1_norm, ffn1_w_gate, ffn1_w_up, ffn1_w_down, mix_norm, ffn2_norm, ffn2_w_gate, ffn2_w_up, ffn2_w_down, pool_w, pool_scale, kv_norm, w_k, w_v, k_gain, w_q, q_gain, rel_bias, w_o, loss_target, m_ffn1_norm, m_ffn1_w_gate, m_ffn1_w_up, m_ffn1_w_down, m_mix_norm, m_ffn2_norm, m_ffn2_w_gate, m_ffn2_w_up, m_ffn2_w_down, m_pool_w, m_pool_scale, m_kv_norm, m_w_k, m_w_v, m_k_gain, m_w_q, m_q_gain, m_rel_bias, m_w_o, v_ffn1_norm, v_ffn1_w_gate, v_ffn1_w_up, v_ffn1_w_down, v_mix_norm, v_ffn2_norm, v_ffn2_w_gate, v_ffn2_w_up, v_ffn2_w_down, v_pool_w, v_pool_scale, v_kv_norm, v_w_k, v_w_v, v_k_gain, v_w_q, v_q_gain, v_rel_bias, v_w_o):
    wts = dict(ffn1_norm=ffn1_norm, ffn1_w_gate=ffn1_w_gate, ffn1_w_up=ffn1_w_up, ffn1_w_down=ffn1_w_down,
               mix_norm=mix_norm, ffn2_norm=ffn2_norm, ffn2_w_gate=ffn2_w_gate, ffn2_w_up=ffn2_w_up,
               ffn2_w_down=ffn2_w_down, pool_w=pool_w, pool_scale=pool_scale, kv_norm=kv_norm, w_k=w_k, w_v=w_v,
               k_gain=k_gain, w_q=w_q, q_gain=q_gain, rel_bias=rel_bias, w_o=w_o)
    mom = dict(ffn1_norm=m_ffn1_norm, ffn1_w_gate=m_ffn1_w_gate, ffn1_w_up=m_ffn1_w_up, ffn1_w_down=m_ffn1_w_down,
               mix_norm=m_mix_norm, ffn2_norm=m_ffn2_norm, ffn2_w_gate=m_ffn2_w_gate, ffn2_w_up=m_ffn2_w_up,
               ffn2_w_down=m_ffn2_w_down, pool_w=m_pool_w, pool_scale=m_pool_scale, kv_norm=m_kv_norm, w_k=m_w_k,
               w_v=m_w_v, k_gain=m_k_gain, w_q=m_w_q, q_gain=m_q_gain, rel_bias=m_rel_bias, w_o=m_w_o)
    var = dict(ffn1_norm=v_ffn1_norm, ffn1_w_gate=v_ffn1_w_gate, ffn1_w_up=v_ffn1_w_up, ffn1_w_down=v_ffn1_w_down,
               mix_norm=v_mix_norm, ffn2_norm=v_ffn2_norm, ffn2_w_gate=v_ffn2_w_gate, ffn2_w_up=v_ffn2_w_up,
               ffn2_w_down=v_ffn2_w_down, pool_w=v_pool_w, pool_scale=v_pool_scale, kv_norm=v_kv_norm, w_k=v_w_k,
               w_v=v_w_v, k_gain=v_k_gain, w_q=v_w_q, q_gain=v_q_gain, rel_bias=v_rel_bias, w_o=v_w_o)
    chip = 2 * lax.axis_index("x") + lax.axis_index("y")
    core = lax.axis_index("c").astype(jnp.int32).reshape(1)

    sh = {}
    for name in _SHARDED:
        a = _as3d(name, wts[name])
        sh[_SHORT[name]] = _cast_pad(f"cast_{name}", a, _round_up(a.shape[1], LANES), a.shape[2])
    sh["pscale"] = _cast_pad("cast_pool_scale", pool_scale[None], 8, pool_scale.shape[1], F32)
    small = dict(n1=ffn1_norm, nm=mix_norm, n2=ffn2_norm, kvn=kv_norm, kg=k_gain, qg=q_gain, rel=rel_bias)

    loss, dx, rbufs, gs = _forward_backward(x[0], loss_target[0], sh, small, core)

    small_grads = [jnp.concatenate(gs["n1"]), jnp.concatenate(gs["nm"]), jnp.concatenate(gs["n2"]), gs["kvn"], gs["kg"],
                   jnp.concatenate(gs["qg"]), jnp.stack(gs["rel"]), jnp.concatenate(gs["pscale"]), loss[:, :1]]
    red = _unpack(_allreduce_small("allreduce_small", _pack(small_grads)), small_grads)
    g_small = dict(ffn1_norm=red[0], mix_norm=red[1], ffn2_norm=red[2], kv_norm=red[3].reshape(-1),
                   k_gain=red[4].reshape(-1), q_gain=red[5], rel_bias=red[6],
                   pool_scale=lax.dynamic_slice_in_dim(red[7], chip * pool_scale.shape[1], pool_scale.shape[1], 1))
    loss_out = red[8].reshape(())

    grad, delta, new_m, new_v = {}, {}, {}, {}
    for name in _SHARDED:
        w3 = _as3d(name, wts[name])
        outs = _adamw_sharded(f"adamw_{name}", [rbufs[(_SHORT[name], l)] for l in range(w3.shape[0])], w3,
                              _as3d(name, mom[name]), _as3d(name, var[name]))
        grad[name], delta[name], new_m[name], new_v[name] = [_from3d(name, o, wts[name].shape) for o in outs]
    sw = [wts[n] for n in _SMALL]
    sg = [g_small[n].reshape(wts[n].shape) for n in _SMALL]
    d_s, m_s, v_s = _adamw_small("adamw_small", _pack(sw), _pack(sg), _pack([mom[n] for n in _SMALL]),
                                 _pack([var[n] for n in _SMALL]))
    for n, g_n, d_n, m_n, v_n in zip(_SMALL, sg, _unpack(d_s, sw), _unpack(m_s, sw), _unpack(v_s, sw)):
        grad[n], delta[n], new_m[n], new_v[n] = g_n, d_n, m_n, v_n
    return (loss_out, dx[None], *[grad[n] for n in _WEIGHTS], *[delta[n] for n in _WEIGHTS],
            *[new_m[n] for n in _WEIGHTS], *[new_v[n] for n in _WEIGHTS])
```
